```python
import jax, jax.numpy as jnp
from jax import lax
import numpy as np

D_MODEL = 1024
BATCH = 4
SEQ = 4096
DEPTH = 1

MEM_LEN = 256
RMS_EPS = 1e-6
F_GROUPS = 8
F_GROUP_DIM = 64
F_WIDTH = F_GROUPS * F_GROUP_DIM
MLA_HEADS = 8
QK_NOPE_DIM = 64
QK_ROPE_DIM = 32
V_HEAD_DIM = 64
Q_LORA_RANK = 384
KV_LORA_RANK = 256
ROPE_THETA = 10000.0
Q_BLOCK = 128
MLA_OUT = MLA_HEADS * V_HEAD_DIM
MIX_WIDTH = F_WIDTH + MLA_OUT
IN_WIDTH = F_WIDTH + Q_LORA_RANK + KV_LORA_RANK + QK_ROPE_DIM
IN_SPLITS = (F_WIDTH, F_WIDTH + Q_LORA_RANK, F_WIDTH + Q_LORA_RANK + KV_LORA_RANK)
MEM_HEADS = 4
MEM_HEAD_DIM = D_MODEL // MEM_HEADS
N_EXPERTS = 16
CAPACITY_FACTOR = 2
D_EXPERT = D_MODEL

kernel_name = "hybrid_fourier_mla_ec_moe_encoder"


def rms_norm(x, g):
    xf = x.astype(jnp.float32)
    y = xf * lax.rsqrt(jnp.mean(xf * xf, axis=-1, keepdims=True) + RMS_EPS)
    return (y * g.astype(jnp.float32)).astype(x.dtype)


def rope_tables(positions):
    freqs = 1.0 / (ROPE_THETA ** (jnp.arange(0, QK_ROPE_DIM, 2, dtype=jnp.float32) / QK_ROPE_DIM))
    ang = positions.astype(jnp.float32)[..., None] * freqs
    return jnp.cos(ang), jnp.sin(ang)


def apply_rope(x, cos, sin):
    xf = x.astype(jnp.float32)
    x1, x2 = jnp.split(xf, 2, axis=-1)
    out = jnp.concatenate([x1 * cos - x2 * sin, x1 * sin + x2 * cos], axis=-1)
    return out.astype(x.dtype)


def fourier_mix(u_f, w_fourier):
    B, S, _ = u_f.shape
    ug = u_f.reshape(B, S, F_GROUPS, F_GROUP_DIM).astype(jnp.float32)
    z = jnp.fft.fft2(ug, axes=(1, 3), norm="ortho").real.astype(u_f.dtype)
    y = jnp.einsum("bsgc,gce->bsge", z, w_fourier)
    return y.reshape(B, S, F_WIDTH)


def mla_attention(u_q, u_kv, u_kr, cos, sin, g_q_lat, w_q_up, g_kv_lat, w_kv_up):
    B, S, _ = u_q.shape
    q = (rms_norm(u_q, g_q_lat) @ w_q_up).reshape(B, S, MLA_HEADS, QK_NOPE_DIM + QK_ROPE_DIM)
    q_nope, q_rope = q[..., :QK_NOPE_DIM], q[..., QK_NOPE_DIM:]
    q_rope = apply_rope(q_rope, cos[:, :, None, :], sin[:, :, None, :])
    kv = (rms_norm(u_kv, g_kv_lat) @ w_kv_up).reshape(B, S, MLA_HEADS, QK_NOPE_DIM + V_HEAD_DIM)
    k_nope, v = kv[..., :QK_NOPE_DIM], kv[..., QK_NOPE_DIM:]
    k_rope = apply_rope(u_kr, cos, sin)
    scale = (QK_NOPE_DIM + QK_ROPE_DIM) ** -0.5
    nb = S // Q_BLOCK
    qn = q_nope.reshape(B, nb, Q_BLOCK, MLA_HEADS, QK_NOPE_DIM).swapaxes(0, 1)
    qr = q_rope.reshape(B, nb, Q_BLOCK, MLA_HEADS, QK_ROPE_DIM).swapaxes(0, 1)

    def block(args):
        qn_b, qr_b = args
        s = (jnp.einsum("bqhd,bkhd->bhqk", qn_b, k_nope)
             + jnp.einsum("bqhd,bkd->bhqk", qr_b, k_rope))
        p = jax.nn.softmax(s.astype(jnp.float32) * scale, axis=-1).astype(v.dtype)
        return jnp.einsum("bhqk,bkhd->bqhd", p, v)

    o = lax.map(block, (qn, qr))
    return o.swapaxes(0, 1).reshape(B, S, MLA_OUT)


def memory_cross_attention(hq, mn, w_mem_q, w_mem_kv, w_mem_o):
    B, S, D = hq.shape
    M = mn.shape[1]
    q = (hq @ w_mem_q).reshape(B, S, MEM_HEADS, MEM_HEAD_DIM)
    k, v = jnp.split(mn @ w_mem_kv, 2, axis=-1)
    k = k.reshape(B, M, MEM_HEADS, MEM_HEAD_DIM)
    v = v.reshape(B, M, MEM_HEADS, MEM_HEAD_DIM)
    s = jnp.einsum("bshd,bmhd->bhsm", q, k).astype(jnp.float32) * (MEM_HEAD_DIM ** -0.5)
    p = jax.nn.softmax(s, axis=-1).astype(v.dtype)
    o = jnp.einsum("bhsm,bmhd->bshd", p, v).reshape(B, S, D)
    return o @ w_mem_o


def expert_choice_moe(h, w_router, w_gate, w_up, w_down):
    B, S, D = h.shape
    cap = CAPACITY_FACTOR * S // N_EXPERTS
    aff = jax.nn.softmax((h @ w_router).astype(jnp.float32), axis=-1)
    gates, idx = lax.top_k(aff.swapaxes(1, 2), cap)
    xin = jax.vmap(lambda hb, ib: hb[ib])(h, idx)
    a = jnp.einsum("becd,edf->becf", xin, w_gate)
    b = jnp.einsum("becd,edf->becf", xin, w_up)
    y = jnp.einsum("becf,efd->becd", jax.nn.silu(a) * b, w_down)
    y = y * gates.astype(y.dtype)[..., None]
    out = jax.vmap(lambda ib, yb: jnp.zeros((S, D), yb.dtype).at[ib].add(yb))(
        idx.reshape(B, -1), y.reshape(B, -1, D))
    return out


def setup_inputs(seed: int = 0) -> dict:
    key = jax.random.key(seed)
    ks = jax.random.split(key, 24)
    f32 = jnp.float32

    def nrm(k, shape, fan_in):
        return jax.random.normal(k, shape, f32) * (fan_in ** -0.5)

    def gain(k, shape):
        return 1.0 + 0.02 * jax.random.normal(k, shape, f32)

    L = DEPTH
    x = jax.random.normal(ks[0], (BATCH, SEQ, D_MODEL), f32)
    mem = jax.random.normal(ks[1], (BATCH, MEM_LEN, D_MODEL), f32)
    offs = jax.random.randint(ks[2], (BATCH, 1), 0, 1024, dtype=jnp.int32)
    positions = offs + jnp.arange(SEQ, dtype=jnp.int32)[None, :]
    return {
        "x": x,
        "mem": mem,
        "positions": positions,
        "g_mix": gain(ks[3], (L, D_MODEL)),
        "w_in": nrm(ks[4], (L, D_MODEL, IN_WIDTH), D_MODEL),
        "g_q_lat": gain(ks[5], (L, Q_LORA_RANK)),
        "w_q_up": nrm(ks[6], (L, Q_LORA_RANK, MLA_HEADS * (QK_NOPE_DIM + QK_ROPE_DIM)), Q_LORA_RANK),
        "g_kv_lat": gain(ks[7], (L, KV_LORA_RANK)),
        "w_kv_up": nrm(ks[8], (L, KV_LORA_RANK, MLA_HEADS * (QK_NOPE_DIM + V_HEAD_DIM)), KV_LORA_RANK),
        "w_fourier": nrm(ks[9], (L, F_GROUPS, F_GROUP_DIM, F_GROUP_DIM), F_GROUP_DIM),
        "w_out": nrm(ks[10], (L, MIX_WIDTH, D_MODEL), MIX_WIDTH),
        "g_mem_q": gain(ks[11], (L, D_MODEL)),
        "g_mem_kv": gain(ks[12], (L, D_MODEL)),
        "w_mem_q": nrm(ks[13], (L, D_MODEL, D_MODEL), D_MODEL),
        "w_mem_kv": nrm(ks[14], (L, D_MODEL, 2 * D_MODEL), D_MODEL),
        "w_mem_o": nrm(ks[15], (L, D_MODEL, D_MODEL), D_MODEL),
        "g_ffn": gain(ks[16], (L, D_MODEL)),
        "w_router": nrm(ks[17], (L, D_MODEL, N_EXPERTS), D_MODEL),
        "w_exp_gate": nrm(ks[18], (L, N_EXPERTS, D_MODEL, D_EXPERT), D_MODEL),
        "w_exp_up": nrm(ks[19], (L, N_EXPERTS, D_MODEL, D_EXPERT), D_MODEL),
        "w_exp_down": nrm(ks[20], (L, N_EXPERTS, D_EXPERT, D_MODEL), D_EXPERT),
        "g_final": gain(ks[21], (D_MODEL,)),
    }


def reference(x, mem, positions, g_mix, w_in, g_q_lat, w_q_up, g_kv_lat, w_kv_up, w_fourier,
              w_out, g_mem_q, g_mem_kv, w_mem_q, w_mem_kv, w_mem_o, g_ffn, w_router,
              w_exp_gate, w_exp_up, w_exp_down, g_final):
    cos, sin = rope_tables(positions)
    for l in range(DEPTH):
        h = rms_norm(x, g_mix[l])
        u = h @ w_in[l]
        u_f, u_q, u_kv, u_kr = jnp.split(u, IN_SPLITS, axis=-1)
        y_f = fourier_mix(u_f, w_fourier[l])
        y_a = mla_attention(u_q, u_kv, u_kr, cos, sin, g_q_lat[l], w_q_up[l], g_kv_lat[l], w_kv_up[l])
        x = x + jnp.concatenate([y_f, y_a], axis=-1) @ w_out[l]
        x = x + memory_cross_attention(rms_norm(x, g_mem_q[l]), rms_norm(mem, g_mem_kv[l]),
                                       w_mem_q[l], w_mem_kv[l], w_mem_o[l])
        x = x + expert_choice_moe(rms_norm(x, g_ffn[l]), w_router[l],
                                  w_exp_gate[l], w_exp_up[l], w_exp_down[l])
    return rms_norm(x, g_final)
```

```python
import functools

import numpy as np
import jax
import jax.numpy as jnp
from jax import lax
from jax.experimental import pallas as pl
from jax.experimental.pallas import tpu as pltpu

F32 = jnp.float32
BF16 = jnp.bfloat16
I32 = jnp.int32

D_MODEL = 1024
BATCH = 4
SEQ = 4096
TOKENS = BATCH * SEQ
MEM_LEN = 256
RMS_EPS = 1e-6
F_GROUPS = 8
F_GROUP_DIM = 64
F_WIDTH = F_GROUPS * F_GROUP_DIM
MLA_HEADS = 8
QK_NOPE_DIM = 64
QK_ROPE_DIM = 32
V_HEAD_DIM = 64
Q_LORA_RANK = 384
KV_LORA_RANK = 256
ROPE_THETA = 10000.0
MEM_HEADS = 4
MEM_HEAD_DIM = D_MODEL // MEM_HEADS
N_EXPERTS = 16
CAPACITY = 2 * SEQ // N_EXPERTS

LANES = 128
HEAD_PAD = 128
IN_PAD = 1280
HALF = SEQ // 2
QUARTER = SEQ // 4
VMEM_LIMIT = 56 * 1024 * 1024

TM_IN = 512
TM_MIX = 512
TQ = 512
TOKEN_CHUNK = 256
OFFS_STRIDE = SEQ // TOKEN_CHUNK + 1
GATHER_WIN = 128
GATHER_WIN_SHIFT = 7
SCATTER_WIN = 256
SCATTER_ALIGN_SHIFT = 7


def _rms(x, g):
    return x * lax.rsqrt(jnp.mean(x * x, axis=-1, keepdims=True) + RMS_EPS) * g


def _params(*sem):
    return pltpu.CompilerParams(dimension_semantics=sem, vmem_limit_bytes=VMEM_LIMIT)


def _rope_kernel(pos_ref, freq_ref, cos_ref, sin_ref):
    ang = pos_ref[...] * freq_ref[...]
    cos_ref[...] = jnp.cos(ang)
    sin_ref[...] = jnp.sin(ang)


def _rope_tables(positions):
    half = QK_ROPE_DIM // 2
    freqs = 1.0 / (ROPE_THETA ** (jnp.arange(0, QK_ROPE_DIM, 2, dtype=F32) / QK_ROPE_DIM))
    rows = TOKENS * half // LANES
    pos = jnp.repeat(positions.astype(F32).reshape(-1), half).reshape(rows, LANES)
    freq = jnp.tile(freqs, LANES // half).reshape(1, LANES)
    cos, sin = pl.pallas_call(
        _rope_kernel,
        out_shape=(jax.ShapeDtypeStruct((rows, LANES), F32),) * 2,
        name="rope_tables",
    )(pos, freq)
    return cos.reshape(TOKENS, half), sin.reshape(TOKENS, half)


def _rope_expanders():
    half = QK_ROPE_DIM // 2
    lane = np.arange(LANES)
    hit = (lane[None, :] % half) == np.arange(half)[:, None]
    is_sin = (lane // QK_ROPE_DIM) % 2 == 1
    return (jnp.asarray((hit & ~is_sin[None, :]).astype(np.float32)),
            jnp.asarray((hit & is_sin[None, :]).astype(np.float32)))


def _chan_kernel(cbd_ref, sbd_ref, w_ref, ab_ref):
    w = w_ref[...]
    ab_ref[:, :F_WIDTH] = jnp.dot(cbd_ref[...], w, precision=lax.Precision.HIGHEST,
                                  preferred_element_type=F32).astype(BF16)
    ab_ref[:, F_WIDTH:] = jnp.dot(sbd_ref[...], w, precision=lax.Precision.HIGHEST,
                                  preferred_element_type=F32).astype(BF16)


def _channel_mats(w_fourier):
    c = np.arange(F_GROUP_DIM)
    ang = 2.0 * np.pi * np.outer(c, c) / F_GROUP_DIM
    scale = F_GROUP_DIM ** -0.5
    eye = np.eye(F_GROUPS)
    cbd = np.kron(eye, np.cos(ang) * scale).astype(np.float32)
    sbd = np.kron(eye, np.sin(ang) * scale).astype(np.float32)
    wbd = (jnp.eye(F_GROUPS, dtype=F32)[:, None, :, None] * w_fourier[:, :, None, :]).reshape(F_WIDTH, F_WIDTH)
    return pl.pallas_call(
        _chan_kernel,
        out_shape=jax.ShapeDtypeStruct((F_WIDTH, 2 * F_WIDTH), BF16),
        name="channel_dft_fold",
    )(jnp.asarray(cbd), jnp.asarray(sbd), wbd)


def _inproj_kernel(x_ref, gmix_ref, win_ref, gq_ref, wq_ref, gkv_ref, wkv_ref, ab_ref, cos_ref, sin_ref,
                   ec_ref, es_ref, v12_ref, q_ref, k_ref, v_ref, vscr_ref):
    h = _rms(x_ref[...], gmix_ref[...]).astype(BF16)
    u = jnp.dot(h, win_ref[...], preferred_element_type=F32)

    v12 = jnp.dot(u[:, :F_WIDTH].astype(BF16), ab_ref[...], preferred_element_type=F32)
    half_rows = vscr_ref.shape[1] // 2
    for c in range(vscr_ref.shape[0]):
        cols = slice(LANES * c, LANES * (c + 1))
        vscr_ref[c] = v12[:, cols]
        v12_ref[0, :, cols] = vscr_ref[c, pl.ds(0, half_rows, stride=2), :].astype(BF16)
        v12_ref[1, :, cols] = vscr_ref[c, pl.ds(1, half_rows, stride=2), :].astype(BF16)

    tab = (jnp.dot(cos_ref[...], ec_ref[...], precision=lax.Precision.HIGHEST, preferred_element_type=F32)
           + jnp.dot(sin_ref[...], es_ref[...], precision=lax.Precision.HIGHEST, preferred_element_type=F32))
    lane = lax.broadcasted_iota(I32, tab.shape, 1)
    c1 = jnp.where(lane < 64, 1.0, jnp.where(lane < 96, tab, 0.0))
    c2 = jnp.where(lane >= 96, tab, 0.0)

    q0 = F_WIDTH
    qn = _rms(u[:, q0:q0 + Q_LORA_RANK], gq_ref[...]).astype(BF16)
    qa = jnp.dot(qn, wq_ref[...], preferred_element_type=F32)
    for hd in range(MLA_HEADS):
        blk = qa[:, HEAD_PAD * hd:HEAD_PAD * (hd + 1)]
        q_ref[:, HEAD_PAD * hd:HEAD_PAD * (hd + 1)] = (
            blk * c1 + pltpu.roll(blk * c2, 96, 1)).astype(BF16)

    kv0 = q0 + Q_LORA_RANK
    kvn = _rms(u[:, kv0:kv0 + KV_LORA_RANK], gkv_ref[...]).astype(BF16)
    kv = jnp.dot(kvn, wkv_ref[...], preferred_element_type=F32)
    kr0 = kv0 + KV_LORA_RANK
    t = u[:, kr0:kr0 + LANES] * jnp.where(lane < 64, tab, 0.0)
    kr = t + pltpu.roll(t, 96, 1)
    kr = jnp.where((lane >= 64) & (lane < 96), pltpu.roll(kr, 64, 1), 0.0)
    for hd in range(MLA_HEADS):
        k_ref[:, HEAD_PAD * hd:HEAD_PAD * (hd + 1)] = (
            kv[:, HEAD_PAD * hd:HEAD_PAD * (hd + 1)] + kr).astype(BF16)
    v_ref[...] = kv[:, MLA_HEADS * HEAD_PAD:].astype(BF16)


def _in_projection(x2d, g_mix, w_in, g_q, w_q_up, g_kv, w_kv_up, ab, cos, sin):
    w_kr = w_in[:, -QK_ROPE_DIM:]
    half = QK_ROPE_DIM // 2
    w_kr_rot = jnp.concatenate([-w_kr[:, half:], w_kr[:, :half]], axis=1)
    win = jnp.concatenate(
        [w_in, w_kr_rot, jnp.zeros((D_MODEL, IN_PAD - w_in.shape[1] - QK_ROPE_DIM), F32)], axis=1).astype(BF16)
    scale = (QK_NOPE_DIM + QK_ROPE_DIM) ** -0.5
    wq = w_q_up.reshape(Q_LORA_RANK, MLA_HEADS, QK_NOPE_DIM + QK_ROPE_DIM) * scale
    wq_rope = wq[:, :, QK_NOPE_DIM:]
    wq_rot = jnp.concatenate([-wq_rope[:, :, half:], wq_rope[:, :, :half]], axis=2)
    wq = jnp.concatenate([wq, wq_rot], axis=2).reshape(Q_LORA_RANK, MLA_HEADS * HEAD_PAD).astype(BF16)
    wkv = w_kv_up.reshape(KV_LORA_RANK, MLA_HEADS, QK_NOPE_DIM + V_HEAD_DIM)
    wk = jnp.concatenate([wkv[:, :, :QK_NOPE_DIM],
                          jnp.zeros((KV_LORA_RANK, MLA_HEADS, HEAD_PAD - QK_NOPE_DIM), F32)], axis=2)
    wkv = jnp.concatenate([wk.reshape(KV_LORA_RANK, MLA_HEADS * HEAD_PAD),
                           wkv[:, :, QK_NOPE_DIM:].reshape(KV_LORA_RANK, MLA_HEADS * V_HEAD_DIM)],
                          axis=1).astype(BF16)

    tm = TM_IN
    ec, es = _rope_expanders()
    full = lambda shape: pl.BlockSpec(shape, lambda i: (0,) * len(shape))
    tile = lambda w: pl.BlockSpec((tm, w), lambda i: (i, 0))
    per_half = HALF // tm
    per_seq = SEQ // tm
    v12_spec = pl.BlockSpec((None, 2, None, tm // 2, 2 * F_WIDTH),
                            lambda i: (i // per_seq, 0, (i % per_seq) // per_half, i % per_half, 0))
    return pl.pallas_call(
        _inproj_kernel,
        grid=(TOKENS // tm,),
        in_specs=[tile(D_MODEL), full((1, D_MODEL)), full(win.shape), full((1, Q_LORA_RANK)), full(wq.shape),
                  full((1, KV_LORA_RANK)), full(wkv.shape), full(ab.shape), tile(QK_ROPE_DIM // 2),
                  tile(QK_ROPE_DIM // 2), full(ec.shape), full(es.shape)],
        out_specs=[v12_spec, tile(MLA_HEADS * HEAD_PAD), tile(MLA_HEADS * HEAD_PAD),
                   tile(MLA_HEADS * V_HEAD_DIM)],
        out_shape=[jax.ShapeDtypeStruct((BATCH, 2, 2, QUARTER, 2 * F_WIDTH), BF16),
                   jax.ShapeDtypeStruct((TOKENS, MLA_HEADS * HEAD_PAD), BF16),
                   jax.ShapeDtypeStruct((TOKENS, MLA_HEADS * HEAD_PAD), BF16),
                   jax.ShapeDtypeStruct((TOKENS, MLA_HEADS * V_HEAD_DIM), BF16)],
        scratch_shapes=[pltpu.VMEM((2 * F_WIDTH // LANES, tm, LANES), F32)],
        compiler_params=_params("parallel"),
        name="in_projection",
    )(x2d, g_mix.reshape(1, -1), win, g_q.reshape(1, -1), wq, g_kv.reshape(1, -1), wkv, ab, cos, sin, ec, es)


def _seq_dft_kernel(v_ref, m_ref, y_ref):
    sign = jnp.where(pl.program_id(1) == 0, 1.0, -1.0)
    parts = []
    for q in range(2):
        ab = (v_ref[q, 0].astype(F32) + sign * v_ref[q, 1].astype(F32)).astype(BF16)
        parts.append(jnp.dot(m_ref[q, :, :QUARTER], ab[:, :F_WIDTH], preferred_element_type=F32)
                     + jnp.dot(m_ref[q, :, QUARTER:], ab[:, F_WIDTH:], preferred_element_type=F32))
    y_ref[0] = (parts[0] + parts[1]).astype(BF16)
    y_ref[1] = (parts[0] - parts[1]).astype(BF16)


def _seq_dft_mats():
    i = np.arange(QUARTER)
    out = np.zeros((2, 2, QUARTER, 2 * QUARTER), np.float32)
    for p in range(2):
        for q in range(2):
            prod = np.outer(2 * i + p, 2 * i + q) % SEQ
            ang = 2.0 * np.pi * prod / SEQ
            out[p, q, :, :QUARTER] = np.cos(ang) / np.sqrt(SEQ)
            out[p, q, :, QUARTER:] = -np.sin(ang) / np.sqrt(SEQ)
    return out


def _sequence_dft(v):
    mats = jnp.asarray(_seq_dft_mats()).astype(BF16)
    return pl.pallas_call(
        _seq_dft_kernel,
        grid=(BATCH, 2),
        in_specs=[pl.BlockSpec((None, 2, 2, QUARTER, 2 * F_WIDTH), lambda b, p: (b, 0, 0, 0, 0)),
                  pl.BlockSpec((None, 2, QUARTER, 2 * QUARTER), lambda b, p: (p, 0, 0, 0))],
        out_specs=pl.BlockSpec((None, None, 2, QUARTER, F_WIDTH), lambda b, p: (b, p, 0, 0, 0)),
        out_shape=jax.ShapeDtypeStruct((BATCH, 2, 2, QUARTER, F_WIDTH), BF16),
        compiler_params=_params("parallel", "arbitrary"),
        name="sequence_dft",
    )(v, mats)


def _mla_kernel(q_ref, k_ref, v_ref, o_ref):
    v = v_ref[...]
    outs = []
    for j in range(2):
        q = q_ref[:, HEAD_PAD * j:HEAD_PAD * (j + 1)]
        k = k_ref[:, HEAD_PAD * j:HEAD_PAD * (j + 1)]
        s = lax.dot_general(q, k, (((1,), (1,)), ((), ())), preferred_element_type=F32)
        p = jnp.exp(s - jnp.max(s, axis=1, keepdims=True))
        l = jnp.sum(p, axis=1, keepdims=True)
        outs.append(jnp.dot(p.astype(BF16), v, preferred_element_type=F32) / l)
    lane = lax.broadcasted_iota(I32, outs[0].shape, 1)
    o_ref[...] = jnp.where(lane < V_HEAD_DIM, outs[0], outs[1]).astype(BF16)


def _mla_attention(q, k, v):
    q = q.reshape(BATCH, SEQ, -1)
    k = k.reshape(BATCH, SEQ, -1)
    v = v.reshape(BATCH, SEQ, -1)
    out = pl.pallas_call(
        _mla_kernel,
        grid=(BATCH, MLA_HEADS // 2, SEQ // TQ),
        in_specs=[pl.BlockSpec((None, TQ, 2 * HEAD_PAD), lambda b, hp, i: (b, i, hp)),
                  pl.BlockSpec((None, SEQ, 2 * HEAD_PAD), lambda b, hp, i: (b, 0, hp)),
                  pl.BlockSpec((None, SEQ, 2 * V_HEAD_DIM), lambda b, hp, i: (b, 0, hp))],
        out_specs=pl.BlockSpec((None, TQ, 2 * V_HEAD_DIM), lambda b, hp, i: (b, i, hp)),
        out_shape=jax.ShapeDtypeStruct((BATCH, SEQ, MLA_HEADS * V_HEAD_DIM), BF16),
        compiler_params=_params("parallel", "parallel", "arbitrary"),
        name="mla_attention",
    )(q, k, v)
    return out.reshape(TOKENS, MLA_HEADS * V_HEAD_DIM)


def _memkv_kernel(mem_ref, g_ref, w_ref, k_ref, v_ref):
    mn = _rms(mem_ref[...], g_ref[...]).astype(BF16)
    kv = jnp.dot(mn, w_ref[...], preferred_element_type=F32)
    k_ref[...] = kv[:, :D_MODEL].astype(BF16)
    v_ref[...] = kv[:, D_MODEL:].astype(BF16)


def _memory_kv(mem, g_mem_kv, w_mem_kv):
    blk = pl.BlockSpec((None, MEM_LEN, D_MODEL), lambda b: (b, 0, 0))
    return pl.pallas_call(
        _memkv_kernel,
        grid=(BATCH,),
        in_specs=[blk, pl.BlockSpec((1, D_MODEL), lambda b: (0, 0)),
                  pl.BlockSpec((D_MODEL, 2 * D_MODEL), lambda b: (0, 0))],
        out_specs=[blk, blk],
        out_shape=[jax.ShapeDtypeStruct((BATCH, MEM_LEN, D_MODEL), BF16)] * 2,
        compiler_params=_params("parallel"),
        name="memory_kv",
    )(mem, g_mem_kv.reshape(1, -1), w_mem_kv.astype(BF16))


def _mix_kernel(x_ref, yf_ref, ya_ref, wo_ref, gq_ref, wmq_ref, mk_ref, mv_ref, wmo_ref, gf_ref, wrh_ref,
                wrl_ref, x2_ref, hext_ref, aff_ref, zscr_ref):
    half_rows = zscr_ref.shape[1] // 2
    wo_f = wo_ref[:F_WIDTH, :]
    z_even = jnp.dot(yf_ref[0], wo_f, preferred_element_type=F32)
    z_odd = jnp.dot(yf_ref[1], wo_f, preferred_element_type=F32)
    for c in range(zscr_ref.shape[0]):
        cols = slice(LANES * c, LANES * (c + 1))
        zscr_ref[c, pl.ds(0, half_rows, stride=2), :] = z_even[:, cols]
        zscr_ref[c, pl.ds(1, half_rows, stride=2), :] = z_odd[:, cols]
    z = jnp.concatenate([zscr_ref[c] for c in range(zscr_ref.shape[0])], axis=1)
    x1 = x_ref[...] + z + jnp.dot(ya_ref[...], wo_ref[F_WIDTH:, :], preferred_element_type=F32)

    hq = _rms(x1, gq_ref[...]).astype(BF16)
    qm = (jnp.dot(hq, wmq_ref[...], preferred_element_type=F32) * (MEM_HEAD_DIM ** -0.5)).astype(BF16)
    heads = []
    for hd in range(MEM_HEADS):
        sl = slice(MEM_HEAD_DIM * hd, MEM_HEAD_DIM * (hd + 1))
        s = lax.dot_general(qm[:, sl], mk_ref[:, sl], (((1,), (1,)), ((), ())), preferred_element_type=F32)
        p = jnp.exp(s - jnp.max(s, axis=1, keepdims=True))
        l = jnp.sum(p, axis=1, keepdims=True)
        heads.append((jnp.dot(p.astype(BF16), mv_ref[:, sl], preferred_element_type=F32) / l).astype(BF16))
    o = jnp.concatenate(heads, axis=1)
    x2 = x1 + jnp.dot(o, wmo_ref[...], preferred_element_type=F32)
    x2_ref[...] = x2

    h3 = _rms(x2, gf_ref[...])
    h3_hi = h3.astype(BF16)
    hext_ref[:, :D_MODEL] = h3_hi
    h3_lo = (h3 - h3_hi.astype(F32)).astype(BF16)
    logits = (jnp.dot(h3_hi, wrh_ref[...], preferred_element_type=F32)
              + jnp.dot(h3_lo, wrh_ref[...], preferred_element_type=F32)
              + jnp.dot(h3_hi, wrl_ref[...], preferred_element_type=F32))
    lane = lax.broadcasted_iota(I32, logits.shape, 1)
    logits = jnp.where(lane < N_EXPERTS, logits, -jnp.inf)
    e = jnp.exp(logits - jnp.max(logits, axis=1, keepdims=True))
    aff = e / jnp.sum(e, axis=1, keepdims=True)
    aff_ref[...] = aff
    hi = aff.astype(BF16)
    r1 = aff - hi.astype(F32)
    mid = r1.astype(BF16)
    lo = (r1 - mid.astype(F32)).astype(BF16)
    hext_ref[:, D_MODEL:] = jnp.where(
        lane < N_EXPERTS, hi,
        jnp.where(lane < 2 * N_EXPERTS, pltpu.roll(mid.astype(F32), N_EXPERTS, 1).astype(BF16),
                  pltpu.roll(lo.astype(F32), 2 * N_EXPERTS, 1).astype(BF16)))


def _mixing(x2d, y_f, y_a, w_out, g_mem_q, w_mem_q, mk, mv, w_mem_o, g_ffn, w_router):
    tm = TM_MIX
    wr = jnp.concatenate([w_router, jnp.zeros((D_MODEL, LANES - N_EXPERTS), F32)], axis=1)
    wr_hi = wr.astype(BF16)
    wr_lo = (wr - wr_hi.astype(F32)).astype(BF16)
    full = lambda shape: pl.BlockSpec(shape, lambda i: (0,) * len(shape))
    tile = lambda w: pl.BlockSpec((tm, w), lambda i: (i, 0))
    per_half = HALF // tm
    per_seq = SEQ // tm
    per_batch = pl.BlockSpec((None, MEM_LEN, D_MODEL), lambda i: (i // per_seq, 0, 0))
    yf_spec = pl.BlockSpec((None, 2, None, tm // 2, F_WIDTH),
                           lambda i: (i // per_seq, 0, (i % per_seq) // per_half, i % per_half, 0))
    return pl.pallas_call(
        _mix_kernel,
        grid=(TOKENS // tm,),
        in_specs=[tile(D_MODEL), yf_spec, tile(F_WIDTH), full((D_MODEL, D_MODEL)), full((1, D_MODEL)),
                  full((D_MODEL, D_MODEL)), per_batch, per_batch, full((D_MODEL, D_MODEL)), full((1, D_MODEL)),
                  full((D_MODEL, LANES)), full((D_MODEL, LANES))],
        out_specs=[tile(D_MODEL), tile(D_MODEL + LANES), tile(LANES)],
        out_shape=[jax.ShapeDtypeStruct((TOKENS, D_MODEL), F32),
                   jax.ShapeDtypeStruct((TOKENS, D_MODEL + LANES), BF16),
                   jax.ShapeDtypeStruct((TOKENS, LANES), F32)],
        scratch_shapes=[pltpu.VMEM((D_MODEL // LANES, tm, LANES), F32)],
        compiler_params=_params("parallel"),
        name="mix_memattn_router",
    )(x2d, y_f, y_a, w_out.astype(BF16), g_mem_q.reshape(1, -1), w_mem_q.astype(BF16), mk, mv,
      w_mem_o.astype(BF16), g_ffn.reshape(1, -1), wr_hi, wr_lo)


def _topk_kernel(aff_ref, slot_ref, offs_ref):
    aff = aff_ref[...]
    rows = aff.shape[0]

    thr = jnp.zeros((rows, 1), I32)
    for bit in range(30, -1, -1):
        cand = thr | (1 << bit)
        cnt = jnp.sum(jnp.where(aff >= pltpu.bitcast(cand, F32), 1.0, 0.0), axis=1, keepdims=True)
        thr = jnp.where(cnt >= CAPACITY, cand, thr)
    thr_f = pltpu.bitcast(thr, F32)

    chunk = 256
    r = lax.broadcasted_iota(I32, (chunk, chunk), 0)
    c = lax.broadcasted_iota(I32, (chunk, chunk), 1)
    tri = jnp.where(r < c, 1.0, 0.0).astype(BF16)

    def exclusive_count(mask):
        off = jnp.zeros((rows, 1), F32)
        outs = []
        for j in range(SEQ // chunk):
            mj = mask[:, chunk * j:chunk * (j + 1)]
            outs.append(jnp.dot(mj.astype(BF16), tri, preferred_element_type=F32) + off)
            off = off + jnp.sum(mj, axis=1, keepdims=True)
        return jnp.concatenate(outs, axis=1), off

    gt = aff > thr_f
    tie = jnp.where(aff == thr_f, 1.0, 0.0)
    n_gt = jnp.sum(jnp.where(gt, 1.0, 0.0), axis=1, keepdims=True)
    tie_rank, _ = exclusive_count(tie)
    sel = jnp.where(gt | ((tie > 0.0) & (tie_rank < CAPACITY - n_gt)), 1.0, 0.0)
    slot, _ = exclusive_count(sel)
    slot_ref[...] = jnp.where(sel > 0.0, slot.astype(I32), -1)
    tok = lax.broadcasted_iota(I32, (SEQ, LANES), 0)
    j = lax.broadcasted_iota(I32, (SEQ, LANES), 1)
    before = jnp.where(tok < j * TOKEN_CHUNK, 1.0, 0.0).astype(BF16)
    offs_ref[...] = jnp.dot(sel.astype(BF16), before, preferred_element_type=F32).astype(I32)


def _expert_slots(aff):
    aff_t = aff[:, :N_EXPERTS].reshape(BATCH, SEQ, N_EXPERTS).transpose(0, 2, 1).reshape(BATCH * N_EXPERTS, SEQ)
    slots, offs = pl.pallas_call(
        _topk_kernel,
        out_shape=[jax.ShapeDtypeStruct((BATCH * N_EXPERTS, SEQ), I32),
                   jax.ShapeDtypeStruct((BATCH * N_EXPERTS, LANES), I32)],
        compiler_params=pltpu.CompilerParams(vmem_limit_bytes=VMEM_LIMIT),
        name="expert_topk",
    )(aff_t)
    return slots, offs[:, :OFFS_STRIDE].reshape(-1)


def _expert_kernel(offs_ref, slot_ref, h_ref, wg_ref, wu_ref, wd_ref, y_ref, acc_ref):
    base = (pl.program_id(1) * N_EXPERTS + pl.program_id(0)) * OFFS_STRIDE
    acc_ref[...] = jnp.zeros_like(acc_ref)
    row = lax.broadcasted_iota(I32, (GATHER_WIN, TOKEN_CHUNK), 0)

    def gather_chunk(j, carry):
        first = offs_ref[base + j]
        end = offs_ref[base + j + 1]
        w_first = first >> GATHER_WIN_SHIFT
        n_win = jnp.where(end > first, ((end - 1) >> GATHER_WIN_SHIFT) - w_first + 1, 0)
        t0 = pl.multiple_of(j * TOKEN_CHUNK, TOKEN_CHUNK)
        slot_c = slot_ref[:, pl.ds(t0, TOKEN_CHUNK)]

        def gather_window(i, c):
            r0 = pl.multiple_of((w_first + i) * GATHER_WIN, GATHER_WIN)
            onehot = jnp.where(row + r0 == slot_c, 1.0, 0.0).astype(BF16)
            acc_ref[pl.ds(r0, GATHER_WIN), :] += jnp.dot(onehot, h_ref[pl.ds(t0, TOKEN_CHUNK), :],
                                                          preferred_element_type=F32)
            return c

        lax.fori_loop(0, n_win, gather_window, 0)
        return carry

    lax.fori_loop(0, SEQ // TOKEN_CHUNK, gather_chunk, 0)
    xin = acc_ref[:, :D_MODEL].astype(BF16)
    ext = acc_ref[:, D_MODEL:]
    lane = lax.broadcasted_iota(I32, ext.shape, 1)
    e = pl.program_id(0)
    mine = (lane == e) | (lane == e + N_EXPERTS) | (lane == e + 2 * N_EXPERTS)
    gate = jnp.sum(jnp.where(mine, ext, 0.0), axis=1, keepdims=True)

    a = jnp.dot(xin, wg_ref[...], preferred_element_type=F32)
    b = jnp.dot(xin, wu_ref[...], preferred_element_type=F32)
    hid = (a / (1.0 + jnp.exp(-a)) * b).astype(BF16)
    y = jnp.dot(hid, wd_ref[...], preferred_element_type=F32)
    y_ref[...] = (y * gate).astype(BF16)


def _experts(offs, slots, h_ext, w_gate, w_up, w_down):
    slots = slots.reshape(BATCH, N_EXPERTS, 1, SEQ)
    h_ext = h_ext.reshape(BATCH, SEQ, D_MODEL + LANES)
    wspec = pl.BlockSpec((None, D_MODEL, D_MODEL), lambda e, b, offs: (e, 0, 0))
    return pl.pallas_call(
        _expert_kernel,
        grid_spec=pltpu.PrefetchScalarGridSpec(
            num_scalar_prefetch=1,
            grid=(N_EXPERTS, BATCH),
            in_specs=[pl.BlockSpec((None, None, 1, SEQ), lambda e, b, offs: (b, e, 0, 0)),
                      pl.BlockSpec((None, SEQ, D_MODEL + LANES), lambda e, b, offs: (b, 0, 0)),
                      wspec, wspec, wspec],
            out_specs=pl.BlockSpec((None, None, CAPACITY, D_MODEL), lambda e, b, offs: (b, e, 0, 0)),
            scratch_shapes=[pltpu.VMEM((CAPACITY, D_MODEL + LANES), F32)]),
        out_shape=jax.ShapeDtypeStruct((BATCH, N_EXPERTS, CAPACITY, D_MODEL), BF16),
        compiler_params=_params("arbitrary", "arbitrary"),
        name="expert_ffn",
    )(offs, slots, h_ext, w_gate.astype(BF16), w_up.astype(BF16), w_down.astype(BF16))


def _combine_kernel(offs_ref, x2_ref, slot_ref, y_ref, g_ref, o_ref):
    b = pl.program_id(0)
    j = pl.program_id(1)
    lane = lax.broadcasted_iota(I32, (TOKEN_CHUNK, SCATTER_WIN), 1)
    slot = slot_ref[...]

    def window(e):
        base = (b * N_EXPERTS + e) * OFFS_STRIDE + j
        first = offs_ref[base]
        start = jnp.minimum((first >> SCATTER_ALIGN_SHIFT) << SCATTER_ALIGN_SHIFT, CAPACITY - SCATTER_WIN)
        return pl.multiple_of(start, 1 << SCATTER_ALIGN_SHIFT), offs_ref[base + 1]

    x3 = x2_ref[...]
    for e in range(N_EXPERTS):
        start, _ = window(e)
        onehot = jnp.where(lane + start == slot[:, e:e + 1], 1.0, 0.0).astype(BF16)
        x3 = x3 + jnp.dot(onehot, y_ref[pl.ds(e * CAPACITY + start, SCATTER_WIN), :],
                          preferred_element_type=F32)
    o_ref[...] = x3

    tail = CAPACITY - SCATTER_WIN
    for e in range(N_EXPERTS):
        start, end = window(e)

        @pl.when(end > start + SCATTER_WIN)
        def _():
            sl = slot[:, e:e + 1]
            onehot = jnp.where((lane + tail == sl) & (sl >= start + SCATTER_WIN), 1.0, 0.0).astype(BF16)
            o_ref[...] += jnp.dot(onehot, y_ref[e * CAPACITY + tail:(e + 1) * CAPACITY, :],
                                  preferred_element_type=F32)

    o_ref[...] = _rms(o_ref[...], g_ref[...])


def _combine(offs, x2, slots, y, g_final):
    slots_t = slots.reshape(BATCH, N_EXPERTS, SEQ).transpose(0, 2, 1)
    y = y.reshape(BATCH, N_EXPERTS * CAPACITY, D_MODEL)
    x2 = x2.reshape(BATCH, SEQ, D_MODEL)
    tm = TOKEN_CHUNK
    return pl.pallas_call(
        _combine_kernel,
        grid_spec=pltpu.PrefetchScalarGridSpec(
            num_scalar_prefetch=1,
            grid=(BATCH, SEQ // tm),
            in_specs=[pl.BlockSpec((None, tm, D_MODEL), lambda b, i, offs: (b, i, 0)),
                      pl.BlockSpec((None, tm, N_EXPERTS), lambda b, i, offs: (b, i, 0)),
                      pl.BlockSpec((None, N_EXPERTS * CAPACITY, D_MODEL), lambda b, i, offs: (b, 0, 0)),
                      pl.BlockSpec((1, D_MODEL), lambda b, i, offs: (0, 0))],
            out_specs=pl.BlockSpec((None, tm, D_MODEL), lambda b, i, offs: (b, i, 0))),
        out_shape=jax.ShapeDtypeStruct((BATCH, SEQ, D_MODEL), F32),
        compiler_params=_params("parallel", "arbitrary"),
        name="combine_final_norm",
    )(offs, x2, slots_t, y, g_final.reshape(1, -1))


def kernel(x, mem, positions, g_mix, w_in, g_q_lat, w_q_up, g_kv_lat, w_kv_up, w_fourier, w_out, g_mem_q,
           g_mem_kv, w_mem_q, w_mem_kv, w_mem_o, g_ffn, w_router, w_exp_gate, w_exp_up, w_exp_down, g_final):
    assert x.shape == (BATCH, SEQ, D_MODEL) and g_mix.shape[0] == 1
    x2d = x.reshape(TOKENS, D_MODEL)
    cos, sin = _rope_tables(positions)
    ab = _channel_mats(w_fourier[0])
    v12, q, k, v = _in_projection(x2d, g_mix[0], w_in[0], g_q_lat[0], w_q_up[0], g_kv_lat[0], w_kv_up[0], ab,
                                  cos, sin)
    y_f = _sequence_dft(v12)
    y_a = _mla_attention(q, k, v)
    mk, mv = _memory_kv(mem, g_mem_kv[0], w_mem_kv[0])
    x2, h_ext, aff = _mixing(x2d, y_f, y_a, w_out[0], g_mem_q[0], w_mem_q[0], mk, mv, w_mem_o[0], g_ffn[0],
                             w_router[0])
    slots, offs = _expert_slots(aff)
    y = _experts(offs, slots, h_ext, w_exp_gate[0], w_exp_up[0], w_exp_down[0])
    return _combine(offs, x2, slots, y, g_final)
```

```python
import functools

import numpy as np
import jax
import jax.numpy as jnp
from jax import lax
from jax.experimental import pallas as pl
from jax.experimental.pallas import tpu as pltpu

F32 = jnp.float32
BF16 = jnp.bfloat16
I32 = jnp.int32

D_MODEL = 1024
BATCH = 4
SEQ = 4096
TOKENS = BATCH * SEQ
MEM_LEN = 256
RMS_EPS = 1e-6
F_GROUPS = 8
F_GROUP_DIM = 64
F_WIDTH = F_GROUPS * F_GROUP_DIM
MLA_HEADS = 8
QK_NOPE_DIM = 64
QK_ROPE_DIM = 32
V_HEAD_DIM = 64
Q_LORA_RANK = 384
KV_LORA_RANK = 256
ROPE_THETA = 10000.0
MEM_HEADS = 4
MEM_HEAD_DIM = D_MODEL // MEM_HEADS
N_EXPERTS = 16
CAPACITY = 2 * SEQ // N_EXPERTS

LANES = 128
HEAD_PAD = 128
IN_PAD = 1280
HALF = SEQ // 2
QUARTER = SEQ // 4
VMEM_LIMIT = 56 * 1024 * 1024

TM_IN = 512
TM_MIX = 512
TQ = 512
TOKEN_CHUNK = 512
OFFS_STRIDE = SEQ // TOKEN_CHUNK + 1
SLOT_WIN = 128
SLOT_ALIGN_SHIFT = 4
LAST_WIN = CAPACITY - SLOT_WIN
GATHER_EXPERTS = 8


def _rms(x, g):
    return x * lax.rsqrt(jnp.mean(x * x, axis=-1, keepdims=True) + RMS_EPS) * g


def _params(*sem):
    return pltpu.CompilerParams(dimension_semantics=sem, vmem_limit_bytes=VMEM_LIMIT)


def _rope_kernel(pos_ref, freq_ref, cos_ref, sin_ref):
    ang = pos_ref[...] * freq_ref[...]
    cos_ref[...] = jnp.cos(ang)
    sin_ref[...] = jnp.sin(ang)


def _rope_tables(positions):
    half = QK_ROPE_DIM // 2
    freqs = 1.0 / (ROPE_THETA ** (jnp.arange(0, QK_ROPE_DIM, 2, dtype=F32) / QK_ROPE_DIM))
    rows = TOKENS * half // LANES
    pos = jnp.repeat(positions.astype(F32).reshape(-1), half).reshape(rows, LANES)
    freq = jnp.tile(freqs, LANES // half).reshape(1, LANES)
    cos, sin = pl.pallas_call(
        _rope_kernel,
        out_shape=(jax.ShapeDtypeStruct((rows, LANES), F32),) * 2,
        name="rope_tables",
    )(pos, freq)
    return cos.reshape(TOKENS, half), sin.reshape(TOKENS, half)


def _rope_expanders():
    half = QK_ROPE_DIM // 2
    lane = np.arange(LANES)
    hit = (lane[None, :] % half) == np.arange(half)[:, None]
    is_sin = (lane // QK_ROPE_DIM) % 2 == 1
    return (jnp.asarray((hit & ~is_sin[None, :]).astype(np.float32)),
            jnp.asarray((hit & is_sin[None, :]).astype(np.float32)))


def _chan_kernel(cbd_ref, sbd_ref, w_ref, ab_ref):
    w = w_ref[...]
    ab_ref[:, :F_WIDTH] = jnp.dot(cbd_ref[...], w, precision=lax.Precision.HIGHEST,
                                  preferred_element_type=F32).astype(BF16)
    ab_ref[:, F_WIDTH:] = jnp.dot(sbd_ref[...], w, precision=lax.Precision.HIGHEST,
                                  preferred_element_type=F32).astype(BF16)


def _channel_mats(w_fourier):
    c = np.arange(F_GROUP_DIM)
    ang = 2.0 * np.pi * np.outer(c, c) / F_GROUP_DIM
    scale = F_GROUP_DIM ** -0.5
    eye = np.eye(F_GROUPS)
    cbd = np.kron(eye, np.cos(ang) * scale).astype(np.float32)
    sbd = np.kron(eye, np.sin(ang) * scale).astype(np.float32)
    wbd = (jnp.eye(F_GROUPS, dtype=F32)[:, None, :, None] * w_fourier[:, :, None, :]).reshape(F_WIDTH, F_WIDTH)
    return pl.pallas_call(
        _chan_kernel,
        out_shape=jax.ShapeDtypeStruct((F_WIDTH, 2 * F_WIDTH), BF16),
        name="channel_dft_fold",
    )(jnp.asarray(cbd), jnp.asarray(sbd), wbd)


def _inproj_kernel(x_ref, gmix_ref, win_ref, gq_ref, wq_ref, gkv_ref, wkv_ref, ab_ref, cos_ref, sin_ref,
                   ec_ref, es_ref, v12_ref, q_ref, k_ref, v_ref, vscr_ref):
    h = _rms(x_ref[...], gmix_ref[...]).astype(BF16)
    u = jnp.dot(h, win_ref[...], preferred_element_type=F32)

    v12 = jnp.dot(u[:, :F_WIDTH].astype(BF16), ab_ref[...], preferred_element_type=F32)
    half_rows = vscr_ref.shape[1] // 2
    for c in range(vscr_ref.shape[0]):
        cols = slice(LANES * c, LANES * (c + 1))
        vscr_ref[c] = v12[:, cols]
        v12_ref[0, :, cols] = vscr_ref[c, pl.ds(0, half_rows, stride=2), :].astype(BF16)
        v12_ref[1, :, cols] = vscr_ref[c, pl.ds(1, half_rows, stride=2), :].astype(BF16)

    tab = (jnp.dot(cos_ref[...], ec_ref[...], precision=lax.Precision.HIGHEST, preferred_element_type=F32)
           + jnp.dot(sin_ref[...], es_ref[...], precision=lax.Precision.HIGHEST, preferred_element_type=F32))
    lane = lax.broadcasted_iota(I32, tab.shape, 1)
    c1 = jnp.where(lane < 64, 1.0, jnp.where(lane < 96, tab, 0.0))
    c2 = jnp.where(lane >= 96, tab, 0.0)

    q0 = F_WIDTH
    qn = _rms(u[:, q0:q0 + Q_LORA_RANK], gq_ref[...]).astype(BF16)
    qa = jnp.dot(qn, wq_ref[...], preferred_element_type=F32)
    for hd in range(MLA_HEADS):
        blk = qa[:, HEAD_PAD * hd:HEAD_PAD * (hd + 1)]
        q_ref[:, HEAD_PAD * hd:HEAD_PAD * (hd + 1)] = (
            blk * c1 + pltpu.roll(blk * c2, 96, 1)).astype(BF16)

    kv0 = q0 + Q_LORA_RANK
    kvn = _rms(u[:, kv0:kv0 + KV_LORA_RANK], gkv_ref[...]).astype(BF16)
    kv = jnp.dot(kvn, wkv_ref[...], preferred_element_type=F32)
    kr0 = kv0 + KV_LORA_RANK
    t = u[:, kr0:kr0 + LANES] * jnp.where(lane < 64, tab, 0.0)
    kr = t + pltpu.roll(t, 96, 1)
    kr = jnp.where((lane >= 64) & (lane < 96), pltpu.roll(kr, 64, 1), 0.0)
    for hd in range(MLA_HEADS):
        k_ref[:, HEAD_PAD * hd:HEAD_PAD * (hd + 1)] = (
            kv[:, HEAD_PAD * hd:HEAD_PAD * (hd + 1)] + kr).astype(BF16)
    v_ref[...] = kv[:, MLA_HEADS * HEAD_PAD:].astype(BF16)


def _in_projection(x2d, g_mix, w_in, g_q, w_q_up, g_kv, w_kv_up, ab, cos, sin):
    w_kr = w_in[:, -QK_ROPE_DIM:]
    half = QK_ROPE_DIM // 2
    w_kr_rot = jnp.concatenate([-w_kr[:, half:], w_kr[:, :half]], axis=1)
    win = jnp.concatenate(
        [w_in, w_kr_rot, jnp.zeros((D_MODEL, IN_PAD - w_in.shape[1] - QK_ROPE_DIM), F32)], axis=1).astype(BF16)
    scale = (QK_NOPE_DIM + QK_ROPE_DIM) ** -0.5
    wq = w_q_up.reshape(Q_LORA_RANK, MLA_HEADS, QK_NOPE_DIM + QK_ROPE_DIM) * scale
    wq_rope = wq[:, :, QK_NOPE_DIM:]
    wq_rot = jnp.concatenate([-wq_rope[:, :, half:], wq_rope[:, :, :half]], axis=2)
    wq = jnp.concatenate([wq, wq_rot], axis=2).reshape(Q_LORA_RANK, MLA_HEADS * HEAD_PAD).astype(BF16)
    wkv = w_kv_up.reshape(KV_LORA_RANK, MLA_HEADS, QK_NOPE_DIM + V_HEAD_DIM)
    wk = jnp.concatenate([wkv[:, :, :QK_NOPE_DIM],
                          jnp.zeros((KV_LORA_RANK, MLA_HEADS, HEAD_PAD - QK_NOPE_DIM), F32)], axis=2)
    wkv = jnp.concatenate([wk.reshape(KV_LORA_RANK, MLA_HEADS * HEAD_PAD),
                           wkv[:, :, QK_NOPE_DIM:].reshape(KV_LORA_RANK, MLA_HEADS * V_HEAD_DIM)],
                          axis=1).astype(BF16)

    tm = TM_IN
    ec, es = _rope_expanders()
    full = lambda shape: pl.BlockSpec(shape, lambda i: (0,) * len(shape))
    tile = lambda w: pl.BlockSpec((tm, w), lambda i: (i, 0))
    per_half = HALF // tm
    per_seq = SEQ // tm
    v12_spec = pl.BlockSpec((None, 2, None, tm // 2, 2 * F_WIDTH),
                            lambda i: (i // per_seq, 0, (i % per_seq) // per_half, i % per_half, 0))
    return pl.pallas_call(
        _inproj_kernel,
        grid=(TOKENS // tm,),
        in_specs=[tile(D_MODEL), full((1, D_MODEL)), full(win.shape), full((1, Q_LORA_RANK)), full(wq.shape),
                  full((1, KV_LORA_RANK)), full(wkv.shape), full(ab.shape), tile(QK_ROPE_DIM // 2),
                  tile(QK_ROPE_DIM // 2), full(ec.shape), full(es.shape)],
        out_specs=[v12_spec, tile(MLA_HEADS * HEAD_PAD), tile(MLA_HEADS * HEAD_PAD),
                   tile(MLA_HEADS * V_HEAD_DIM)],
        out_shape=[jax.ShapeDtypeStruct((BATCH, 2, 2, QUARTER, 2 * F_WIDTH), BF16),
                   jax.ShapeDtypeStruct((TOKENS, MLA_HEADS * HEAD_PAD), BF16),
                   jax.ShapeDtypeStruct((TOKENS, MLA_HEADS * HEAD_PAD), BF16),
                   jax.ShapeDtypeStruct((TOKENS, MLA_HEADS * V_HEAD_DIM), BF16)],
        scratch_shapes=[pltpu.VMEM((2 * F_WIDTH // LANES, tm, LANES), F32)],
        compiler_params=_params("parallel"),
        name="in_projection",
    )(x2d, g_mix.reshape(1, -1), win, g_q.reshape(1, -1), wq, g_kv.reshape(1, -1), wkv, ab, cos, sin, ec, es)


def _seq_dft_kernel(v_ref, m_ref, y_ref):
    sign = jnp.where(pl.program_id(1) == 0, 1.0, -1.0)
    parts = []
    for q in range(2):
        ab = (v_ref[q, 0].astype(F32) + sign * v_ref[q, 1].astype(F32)).astype(BF16)
        parts.append(jnp.dot(m_ref[q, :, :QUARTER], ab[:, :F_WIDTH], preferred_element_type=F32)
                     + jnp.dot(m_ref[q, :, QUARTER:], ab[:, F_WIDTH:], preferred_element_type=F32))
    y_ref[0] = (parts[0] + parts[1]).astype(BF16)
    y_ref[1] = (parts[0] - parts[1]).astype(BF16)


def _seq_dft_mats():
    i = np.arange(QUARTER)
    out = np.zeros((2, 2, QUARTER, 2 * QUARTER), np.float32)
    for p in range(2):
        for q in range(2):
            prod = np.outer(2 * i + p, 2 * i + q) % SEQ
            ang = 2.0 * np.pi * prod / SEQ
            out[p, q, :, :QUARTER] = np.cos(ang) / np.sqrt(SEQ)
            out[p, q, :, QUARTER:] = -np.sin(ang) / np.sqrt(SEQ)
    return out


def _sequence_dft(v):
    mats = jnp.asarray(_seq_dft_mats()).astype(BF16)
    return pl.pallas_call(
        _seq_dft_kernel,
        grid=(BATCH, 2),
        in_specs=[pl.BlockSpec((None, 2, 2, QUARTER, 2 * F_WIDTH), lambda b, p: (b, 0, 0, 0, 0)),
                  pl.BlockSpec((None, 2, QUARTER, 2 * QUARTER), lambda b, p: (p, 0, 0, 0))],
        out_specs=pl.BlockSpec((None, None, 2, QUARTER, F_WIDTH), lambda b, p: (b, p, 0, 0, 0)),
        out_shape=jax.ShapeDtypeStruct((BATCH, 2, 2, QUARTER, F_WIDTH), BF16),
        compiler_params=_params("parallel", "arbitrary"),
        name="sequence_dft",
    )(v, mats)


def _mla_kernel(q_ref, k_ref, v_ref, o_ref):
    v = v_ref[...]
    outs = []
    for j in range(2):
        q = q_ref[:, HEAD_PAD * j:HEAD_PAD * (j + 1)]
        k = k_ref[:, HEAD_PAD * j:HEAD_PAD * (j + 1)]
        s = lax.dot_general(q, k, (((1,), (1,)), ((), ())), preferred_element_type=F32)
        p = jnp.exp(s - jnp.max(s, axis=1, keepdims=True))
        l = jnp.sum(p, axis=1, keepdims=True)
        outs.append(jnp.dot(p.astype(BF16), v, preferred_element_type=F32) / l)
    lane = lax.broadcasted_iota(I32, outs[0].shape, 1)
    o_ref[...] = jnp.where(lane < V_HEAD_DIM, outs[0], outs[1]).astype(BF16)


def _mla_attention(q, k, v):
    q = q.reshape(BATCH, SEQ, -1)
    k = k.reshape(BATCH, SEQ, -1)
    v = v.reshape(BATCH, SEQ, -1)
    out = pl.pallas_call(
        _mla_kernel,
        grid=(BATCH, MLA_HEADS // 2, SEQ // TQ),
        in_specs=[pl.BlockSpec((None, TQ, 2 * HEAD_PAD), lambda b, hp, i: (b, i, hp)),
                  pl.BlockSpec((None, SEQ, 2 * HEAD_PAD), lambda b, hp, i: (b, 0, hp)),
                  pl.BlockSpec((None, SEQ, 2 * V_HEAD_DIM), lambda b, hp, i: (b, 0, hp))],
        out_specs=pl.BlockSpec((None, TQ, 2 * V_HEAD_DIM), lambda b, hp, i: (b, i, hp)),
        out_shape=jax.ShapeDtypeStruct((BATCH, SEQ, MLA_HEADS * V_HEAD_DIM), BF16),
        compiler_params=_params("parallel", "parallel", "arbitrary"),
        name="mla_attention",
    )(q, k, v)
    return out.reshape(TOKENS, MLA_HEADS * V_HEAD_DIM)


def _memkv_kernel(mem_ref, g_ref, w_ref, k_ref, v_ref):
    mn = _rms(mem_ref[...], g_ref[...]).astype(BF16)
    kv = jnp.dot(mn, w_ref[...], preferred_element_type=F32)
    k_ref[...] = kv[:, :D_MODEL].astype(BF16)
    v_ref[...] = kv[:, D_MODEL:].astype(BF16)


def _memory_kv(mem, g_mem_kv, w_mem_kv):
    blk = pl.BlockSpec((None, MEM_LEN, D_MODEL), lambda b: (b, 0, 0))
    return pl.pallas_call(
        _memkv_kernel,
        grid=(BATCH,),
        in_specs=[blk, pl.BlockSpec((1, D_MODEL), lambda b: (0, 0)),
                  pl.BlockSpec((D_MODEL, 2 * D_MODEL), lambda b: (0, 0))],
        out_specs=[blk, blk],
        out_shape=[jax.ShapeDtypeStruct((BATCH, MEM_LEN, D_MODEL), BF16)] * 2,
        compiler_params=_params("parallel"),
        name="memory_kv",
    )(mem, g_mem_kv.reshape(1, -1), w_mem_kv.astype(BF16))


def _mix_kernel(x_ref, yf_ref, ya_ref, wo_ref, gq_ref, wmq_ref, mk_ref, mv_ref, wmo_ref, gf_ref, wrh_ref,
                wrl_ref, x2_ref, hext_ref, aff_ref, zscr_ref):
    half_rows = zscr_ref.shape[1] // 2
    wo_f = wo_ref[:F_WIDTH, :]
    z_even = jnp.dot(yf_ref[0], wo_f, preferred_element_type=F32)
    z_odd = jnp.dot(yf_ref[1], wo_f, preferred_element_type=F32)
    for c in range(zscr_ref.shape[0]):
        cols = slice(LANES * c, LANES * (c + 1))
        zscr_ref[c, pl.ds(0, half_rows, stride=2), :] = z_even[:, cols]
        zscr_ref[c, pl.ds(1, half_rows, stride=2), :] = z_odd[:, cols]
    z = jnp.concatenate([zscr_ref[c] for c in range(zscr_ref.shape[0])], axis=1)
    x1 = x_ref[...] + z + jnp.dot(ya_ref[...], wo_ref[F_WIDTH:, :], preferred_element_type=F32)

    hq = _rms(x1, gq_ref[...]).astype(BF16)
    qm = (jnp.dot(hq, wmq_ref[...], preferred_element_type=F32) * (MEM_HEAD_DIM ** -0.5)).astype(BF16)
    heads = []
    for hd in range(MEM_HEADS):
        sl = slice(MEM_HEAD_DIM * hd, MEM_HEAD_DIM * (hd + 1))
        s = lax.dot_general(qm[:, sl], mk_ref[:, sl], (((1,), (1,)), ((), ())), preferred_element_type=F32)
        p = jnp.exp(s - jnp.max(s, axis=1, keepdims=True))
        l = jnp.sum(p, axis=1, keepdims=True)
        heads.append((jnp.dot(p.astype(BF16), mv_ref[:, sl], preferred_element_type=F32) / l).astype(BF16))
    o = jnp.concatenate(heads, axis=1)
    x2 = x1 + jnp.dot(o, wmo_ref[...], preferred_element_type=F32)
    x2_ref[...] = x2

    h3 = _rms(x2, gf_ref[...])
    h3_hi = h3.astype(BF16)
    hext_ref[:, :D_MODEL] = h3_hi
    h3_lo = (h3 - h3_hi.astype(F32)).astype(BF16)
    logits = (jnp.dot(h3_hi, wrh_ref[...], preferred_element_type=F32)
              + jnp.dot(h3_lo, wrh_ref[...], preferred_element_type=F32)
              + jnp.dot(h3_hi, wrl_ref[...], preferred_element_type=F32))
    lane = lax.broadcasted_iota(I32, logits.shape, 1)
    logits = jnp.where(lane < N_EXPERTS, logits, -jnp.inf)
    e = jnp.exp(logits - jnp.max(logits, axis=1, keepdims=True))
    aff = e / jnp.sum(e, axis=1, keepdims=True)
    aff_ref[...] = aff
    hi = aff.astype(BF16)
    r1 = aff - hi.astype(F32)
    mid = r1.astype(BF16)
    lo = (r1 - mid.astype(F32)).astype(BF16)
    hext_ref[:, D_MODEL:] = jnp.where(
        lane < N_EXPERTS, hi,
        jnp.where(lane < 2 * N_EXPERTS, pltpu.roll(mid.astype(F32), N_EXPERTS, 1).astype(BF16),
                  pltpu.roll(lo.astype(F32), 2 * N_EXPERTS, 1).astype(BF16)))


def _mixing(x2d, y_f, y_a, w_out, g_mem_q, w_mem_q, mk, mv, w_mem_o, g_ffn, w_router):
    tm = TM_MIX
    wr = jnp.concatenate([w_router, jnp.zeros((D_MODEL, LANES - N_EXPERTS), F32)], axis=1)
    wr_hi = wr.astype(BF16)
    wr_lo = (wr - wr_hi.astype(F32)).astype(BF16)
    full = lambda shape: pl.BlockSpec(shape, lambda i: (0,) * len(shape))
    tile = lambda w: pl.BlockSpec((tm, w), lambda i: (i, 0))
    per_half = HALF // tm
    per_seq = SEQ // tm
    per_batch = pl.BlockSpec((None, MEM_LEN, D_MODEL), lambda i: (i // per_seq, 0, 0))
    yf_spec = pl.BlockSpec((None, 2, None, tm // 2, F_WIDTH),
                           lambda i: (i // per_seq, 0, (i % per_seq) // per_half, i % per_half, 0))
    return pl.pallas_call(
        _mix_kernel,
        grid=(TOKENS // tm,),
        in_specs=[tile(D_MODEL), yf_spec, tile(F_WIDTH), full((D_MODEL, D_MODEL)), full((1, D_MODEL)),
                  full((D_MODEL, D_MODEL)), per_batch, per_batch, full((D_MODEL, D_MODEL)), full((1, D_MODEL)),
                  full((D_MODEL, LANES)), full((D_MODEL, LANES))],
        out_specs=[tile(D_MODEL), tile(D_MODEL + LANES), tile(LANES)],
        out_shape=[jax.ShapeDtypeStruct((TOKENS, D_MODEL), F32),
                   jax.ShapeDtypeStruct((TOKENS, D_MODEL + LANES), BF16),
                   jax.ShapeDtypeStruct((TOKENS, LANES), F32)],
        scratch_shapes=[pltpu.VMEM((D_MODEL // LANES, tm, LANES), F32)],
        compiler_params=_params("parallel"),
        name="mix_memattn_router",
    )(x2d, y_f, y_a, w_out.astype(BF16), g_mem_q.reshape(1, -1), w_mem_q.astype(BF16), mk, mv,
      w_mem_o.astype(BF16), g_ffn.reshape(1, -1), wr_hi, wr_lo)


def _topk_kernel(aff_ref, slot_ref, offs_ref):
    aff = aff_ref[...]
    rows = aff.shape[0]

    thr = jnp.zeros((rows, 1), I32)
    for bit in range(30, -1, -1):
        cand = thr | (1 << bit)
        cnt = jnp.sum(jnp.where(aff >= pltpu.bitcast(cand, F32), 1.0, 0.0), axis=1, keepdims=True)
        thr = jnp.where(cnt >= CAPACITY, cand, thr)
    thr_f = pltpu.bitcast(thr, F32)

    chunk = 256
    r = lax.broadcasted_iota(I32, (chunk, chunk), 0)
    c = lax.broadcasted_iota(I32, (chunk, chunk), 1)
    tri = jnp.where(r < c, 1.0, 0.0).astype(BF16)

    def exclusive_count(mask):
        off = jnp.zeros((rows, 1), F32)
        outs = []
        for j in range(SEQ // chunk):
            mj = mask[:, chunk * j:chunk * (j + 1)]
            outs.append(jnp.dot(mj.astype(BF16), tri, preferred_element_type=F32) + off)
            off = off + jnp.sum(mj, axis=1, keepdims=True)
        return jnp.concatenate(outs, axis=1), off

    gt = aff > thr_f
    tie = jnp.where(aff == thr_f, 1.0, 0.0)
    n_gt = jnp.sum(jnp.where(gt, 1.0, 0.0), axis=1, keepdims=True)
    tie_rank, _ = exclusive_count(tie)
    sel = jnp.where(gt | ((tie > 0.0) & (tie_rank < CAPACITY - n_gt)), 1.0, 0.0)
    slot, _ = exclusive_count(sel)
    slot_ref[...] = jnp.where(sel > 0.0, slot.astype(I32), -1)
    tok = lax.broadcasted_iota(I32, (SEQ, LANES), 0)
    j = lax.broadcasted_iota(I32, (SEQ, LANES), 1)
    before = jnp.where(tok < j * TOKEN_CHUNK, 1.0, 0.0).astype(BF16)
    offs_ref[...] = jnp.dot(sel.astype(BF16), before, preferred_element_type=F32).astype(I32)


def _expert_slots(aff):
    aff_t = aff[:, :N_EXPERTS].reshape(BATCH, SEQ, N_EXPERTS).transpose(0, 2, 1).reshape(BATCH * N_EXPERTS, SEQ)
    slots, offs = pl.pallas_call(
        _topk_kernel,
        out_shape=[jax.ShapeDtypeStruct((BATCH * N_EXPERTS, SEQ), I32),
                   jax.ShapeDtypeStruct((BATCH * N_EXPERTS, LANES), I32)],
        compiler_params=pltpu.CompilerParams(vmem_limit_bytes=VMEM_LIMIT),
        name="expert_topk",
    )(aff_t)
    return slots, offs[:, :OFFS_STRIDE].reshape(-1)


def _window_start(first):
    start = jnp.minimum((first >> SLOT_ALIGN_SHIFT) << SLOT_ALIGN_SHIFT, LAST_WIN)
    return pl.multiple_of(start, 1 << SLOT_ALIGN_SHIFT)


def _gather_kernel(offs_ref, slot_ref, h_ref, x_ref):
    b = pl.program_id(0)
    e0 = pl.program_id(1) * GATHER_EXPERTS
    j = pl.program_id(2)

    @pl.when(j == 0)
    def _():
        x_ref[...] = jnp.zeros_like(x_ref)

    def bounds(e):
        base = (b * N_EXPERTS + e0 + e) * OFFS_STRIDE + j
        return offs_ref[base], offs_ref[base + 1]

    row = lax.broadcasted_iota(I32, (SLOT_WIN, TOKEN_CHUNK), 0)
    h_c = h_ref[...]
    starts = [_window_start(bounds(e)[0]) for e in range(GATHER_EXPERTS)]
    onehot = jnp.concatenate(
        [jnp.where(row + starts[e] == slot_ref[e:e + 1, :], 1.0, 0.0).astype(BF16)
         for e in range(GATHER_EXPERTS)], axis=0)
    picked = jnp.dot(onehot, h_c, preferred_element_type=F32)
    for e in range(GATHER_EXPERTS):
        rows = pl.ds(e * CAPACITY + starts[e], SLOT_WIN)
        x_ref[rows, :] = (x_ref[rows, :].astype(F32)
                          + picked[e * SLOT_WIN:(e + 1) * SLOT_WIN]).astype(BF16)

    for e in range(GATHER_EXPERTS):
        _, end = bounds(e)
        covered = starts[e] + SLOT_WIN
        slot_e = slot_ref[e:e + 1, :]

        def extra_window(i, carry):
            lo = covered + i * SLOT_WIN
            r0 = pl.multiple_of(jnp.minimum(lo, LAST_WIN), 1 << SLOT_ALIGN_SHIFT)
            hot = jnp.where((row + r0 == slot_e) & (slot_e >= lo), 1.0, 0.0).astype(BF16)
            rows = pl.ds(e * CAPACITY + r0, SLOT_WIN)
            x_ref[rows, :] = (x_ref[rows, :].astype(F32)
                              + jnp.dot(hot, h_c, preferred_element_type=F32)).astype(BF16)
            return carry

        lax.fori_loop(0, jnp.maximum(end - covered + SLOT_WIN - 1, 0) // SLOT_WIN, extra_window, 0)


def _gather(offs, slots, h_ext):
    slots = slots.reshape(BATCH, N_EXPERTS, SEQ)
    h_ext = h_ext.reshape(BATCH, SEQ, D_MODEL + LANES)
    return pl.pallas_call(
        _gather_kernel,
        grid_spec=pltpu.PrefetchScalarGridSpec(
            num_scalar_prefetch=1,
            grid=(BATCH, N_EXPERTS // GATHER_EXPERTS, SEQ // TOKEN_CHUNK),
            in_specs=[pl.BlockSpec((None, GATHER_EXPERTS, TOKEN_CHUNK), lambda b, g, j, offs: (b, g, j)),
                      pl.BlockSpec((None, TOKEN_CHUNK, D_MODEL + LANES), lambda b, g, j, offs: (b, j, 0))],
            out_specs=pl.BlockSpec((None, GATHER_EXPERTS * CAPACITY, D_MODEL + LANES),
                                   lambda b, g, j, offs: (b, g, 0))),
        out_shape=jax.ShapeDtypeStruct((BATCH, N_EXPERTS * CAPACITY, D_MODEL + LANES), BF16),
        compiler_params=_params("parallel", "parallel", "arbitrary"),
        name="expert_gather",
    )(offs, slots, h_ext)


def _expert_kernel(x_ref, wg_ref, wu_ref, wd_ref, y_ref, wg_s, wu_s, wd_s):
    @pl.when(pl.program_id(1) == 0)
    def _():
        wg_s[...] = wg_ref[...].astype(BF16)
        wu_s[...] = wu_ref[...].astype(BF16)
        wd_s[...] = wd_ref[...].astype(BF16)

    xin = x_ref[:, :D_MODEL]
    ext = x_ref[:, D_MODEL:].astype(F32)
    lane = lax.broadcasted_iota(I32, ext.shape, 1)
    e = pl.program_id(0)
    mine = (lane == e) | (lane == e + N_EXPERTS) | (lane == e + 2 * N_EXPERTS)
    gate = jnp.sum(jnp.where(mine, ext, 0.0), axis=1, keepdims=True)

    a = jnp.dot(xin, wg_s[...], preferred_element_type=F32)
    b = jnp.dot(xin, wu_s[...], preferred_element_type=F32)
    hid = (a / (1.0 + jnp.exp(-a)) * b).astype(BF16)
    y = jnp.dot(hid, wd_s[...], preferred_element_type=F32)
    y_ref[...] = (y * gate).astype(BF16)


def _experts(xin, w_gate, w_up, w_down):
    wspec = pl.BlockSpec((None, D_MODEL, D_MODEL), lambda e, b: (e, 0, 0))
    return pl.pallas_call(
        _expert_kernel,
        grid=(N_EXPERTS, BATCH),
        in_specs=[pl.BlockSpec((None, CAPACITY, D_MODEL + LANES), lambda e, b: (b, e, 0)),
                  wspec, wspec, wspec],
        out_specs=pl.BlockSpec((None, CAPACITY, D_MODEL), lambda e, b: (b, e, 0)),
        out_shape=jax.ShapeDtypeStruct((BATCH, N_EXPERTS * CAPACITY, D_MODEL), BF16),
        scratch_shapes=[pltpu.VMEM((D_MODEL, D_MODEL), BF16)] * 3,
        compiler_params=_params("arbitrary", "arbitrary"),
        name="expert_ffn",
    )(xin, w_gate, w_up, w_down)


def _combine_kernel(offs_ref, x2_ref, slot_ref, y_ref, g_ref, o_ref):
    b = pl.program_id(0)
    j = pl.program_id(1)
    lane = lax.broadcasted_iota(I32, (TOKEN_CHUNK, SLOT_WIN), 1)
    slot = slot_ref[...]

    def bounds(e):
        base = (b * N_EXPERTS + e) * OFFS_STRIDE + j
        return offs_ref[base], offs_ref[base + 1]

    starts = [_window_start(bounds(e)[0]) for e in range(N_EXPERTS)]
    onehot = jnp.concatenate(
        [jnp.where(lane + starts[e] == slot[:, e:e + 1], 1.0, 0.0).astype(BF16) for e in range(N_EXPERTS)],
        axis=1)
    rows = jnp.concatenate(
        [y_ref[pl.ds(e * CAPACITY + starts[e], SLOT_WIN), :] for e in range(N_EXPERTS)], axis=0)
    o_ref[...] = x2_ref[...] + jnp.dot(onehot, rows, preferred_element_type=F32)

    tail_lane = lax.broadcasted_iota(I32, (TOKEN_CHUNK, LAST_WIN), 1) + SLOT_WIN
    for e in range(N_EXPERTS):
        _, end = bounds(e)
        covered = starts[e] + SLOT_WIN

        @pl.when(end > covered)
        def _():
            sl = slot[:, e:e + 1]
            hot = jnp.where((tail_lane == sl) & (sl >= covered), 1.0, 0.0).astype(BF16)
            o_ref[...] += jnp.dot(hot, y_ref[e * CAPACITY + SLOT_WIN:(e + 1) * CAPACITY, :],
                                  preferred_element_type=F32)

    o_ref[...] = _rms(o_ref[...], g_ref[...])


def _combine(offs, x2, slots, y, g_final):
    slots_t = slots.reshape(BATCH, N_EXPERTS, SEQ).transpose(0, 2, 1)
    x2 = x2.reshape(BATCH, SEQ, D_MODEL)
    tm = TOKEN_CHUNK
    return pl.pallas_call(
        _combine_kernel,
        grid_spec=pltpu.PrefetchScalarGridSpec(
            num_scalar_prefetch=1,
            grid=(BATCH, SEQ // tm),
            in_specs=[pl.BlockSpec((None, tm, D_MODEL), lambda b, i, offs: (b, i, 0)),
                      pl.BlockSpec((None, tm, N_EXPERTS), lambda b, i, offs: (b, i, 0)),
                      pl.BlockSpec((None, N_EXPERTS * CAPACITY, D_MODEL), lambda b, i, offs: (b, 0, 0)),
                      pl.BlockSpec((1, D_MODEL), lambda b, i, offs: (0, 0))],
            out_specs=pl.BlockSpec((None, tm, D_MODEL), lambda b, i, offs: (b, i, 0))),
        out_shape=jax.ShapeDtypeStruct((BATCH, SEQ, D_MODEL), F32),
        compiler_params=_params("parallel", "arbitrary"),
        name="combine_final_norm",
    )(offs, x2, slots_t, y, g_final.reshape(1, -1))


def kernel(x, mem, positions, g_mix, w_in, g_q_lat, w_q_up, g_kv_lat, w_kv_up, w_fourier, w_out, g_mem_q,
           g_mem_kv, w_mem_q, w_mem_kv, w_mem_o, g_ffn, w_router, w_exp_gate, w_exp_up, w_exp_down, g_final):
    assert x.shape == (BATCH, SEQ, D_MODEL) and g_mix.shape[0] == 1
    x2d = x.reshape(TOKENS, D_MODEL)
    cos, sin = _rope_tables(positions)
    ab = _channel_mats(w_fourier[0])
    v12, q, k, v = _in_projection(x2d, g_mix[0], w_in[0], g_q_lat[0], w_q_up[0], g_kv_lat[0], w_kv_up[0], ab,
                                  cos, sin)
    y_f = _sequence_dft(v12)
    y_a = _mla_attention(q, k, v)
    mk, mv = _memory_kv(mem, g_mem_kv[0], w_mem_kv[0])
    x2, h_ext, aff = _mixing(x2d, y_f, y_a, w_out[0], g_mem_q[0], w_mem_q[0], mk, mv, w_mem_o[0], g_ffn[0],
                             w_router[0])
    slots, offs = _expert_slots(aff)
    xin = _gather(offs, slots, h_ext)
    y = _experts(xin, w_exp_gate[0], w_exp_up[0], w_exp_down[0])
    return _combine(offs, x2, slots, y, g_final)
```

```python
import functools

import numpy as np
import jax
import jax.numpy as jnp
from jax import lax
from jax.experimental import pallas as pl
from jax.experimental.pallas import tpu as pltpu

F32 = jnp.float32
BF16 = jnp.bfloat16
I32 = jnp.int32

D_MODEL = 1024
BATCH = 4
SEQ = 4096
TOKENS = BATCH * SEQ
MEM_LEN = 256
RMS_EPS = 1e-6
F_GROUPS = 8
F_GROUP_DIM = 64
F_WIDTH = F_GROUPS * F_GROUP_DIM
MLA_HEADS = 8
QK_NOPE_DIM = 64
QK_ROPE_DIM = 32
V_HEAD_DIM = 64
Q_LORA_RANK = 384
KV_LORA_RANK = 256
ROPE_THETA = 10000.0
MEM_HEADS = 4
MEM_HEAD_DIM = D_MODEL // MEM_HEADS
N_EXPERTS = 16
CAPACITY = 2 * SEQ // N_EXPERTS

NORM_SLACK = 1.01
SAFE_SHIFT = 30.0

LANES = 128
HEAD_PAD = 128
IN_PAD = 1280
HALF = SEQ // 2
QUARTER = SEQ // 4
VMEM_LIMIT = 56 * 1024 * 1024

TM_IN = 512
TM_MIX = 512
TQ = 512
TOKEN_CHUNK = 512
OFFS_STRIDE = SEQ // TOKEN_CHUNK + 1
SLOT_WIN = 128
SLOT_ALIGN_SHIFT = 4
LAST_WIN = CAPACITY - SLOT_WIN
GATHER_EXPERTS = 8


def _rms(x, g):
    return x * lax.rsqrt(jnp.mean(x * x, axis=-1, keepdims=True) + RMS_EPS) * g


def _params(*sem):
    return pltpu.CompilerParams(dimension_semantics=sem, vmem_limit_bytes=VMEM_LIMIT)


def _rope_kernel(pos_ref, freq_ref, cos_ref, sin_ref):
    ang = pos_ref[...] * freq_ref[...]
    cos_ref[...] = jnp.cos(ang)
    sin_ref[...] = jnp.sin(ang)


def _rope_tables(positions):
    half = QK_ROPE_DIM // 2
    freqs = 1.0 / (ROPE_THETA ** (jnp.arange(0, QK_ROPE_DIM, 2, dtype=F32) / QK_ROPE_DIM))
    rows = TOKENS * half // LANES
    pos = jnp.repeat(positions.astype(F32).reshape(-1), half).reshape(rows, LANES)
    freq = jnp.tile(freqs, LANES // half).reshape(1, LANES)
    cos, sin = pl.pallas_call(
        _rope_kernel,
        out_shape=(jax.ShapeDtypeStruct((rows, LANES), F32),) * 2,
        name="rope_tables",
    )(pos, freq)
    return cos.reshape(TOKENS, half), sin.reshape(TOKENS, half)


def _rope_expanders():
    half = QK_ROPE_DIM // 2
    lane = np.arange(LANES)
    hit = (lane[None, :] % half) == np.arange(half)[:, None]
    is_sin = (lane // QK_ROPE_DIM) % 2 == 1
    return (jnp.asarray((hit & ~is_sin[None, :]).astype(np.float32)),
            jnp.asarray((hit & is_sin[None, :]).astype(np.float32)))


def _chan_kernel(cbd_ref, sbd_ref, w_ref, ab_ref):
    w = w_ref[...]
    ab_ref[:, :F_WIDTH] = jnp.dot(cbd_ref[...], w, precision=lax.Precision.HIGHEST,
                                  preferred_element_type=F32).astype(BF16)
    ab_ref[:, F_WIDTH:] = jnp.dot(sbd_ref[...], w, precision=lax.Precision.HIGHEST,
                                  preferred_element_type=F32).astype(BF16)


def _channel_mats(w_fourier):
    c = np.arange(F_GROUP_DIM)
    ang = 2.0 * np.pi * np.outer(c, c) / F_GROUP_DIM
    scale = F_GROUP_DIM ** -0.5
    eye = np.eye(F_GROUPS)
    cbd = np.kron(eye, np.cos(ang) * scale).astype(np.float32)
    sbd = np.kron(eye, np.sin(ang) * scale).astype(np.float32)
    wbd = (jnp.eye(F_GROUPS, dtype=F32)[:, None, :, None] * w_fourier[:, :, None, :]).reshape(F_WIDTH, F_WIDTH)
    return pl.pallas_call(
        _chan_kernel,
        out_shape=jax.ShapeDtypeStruct((F_WIDTH, 2 * F_WIDTH), BF16),
        name="channel_dft_fold",
    )(jnp.asarray(cbd), jnp.asarray(sbd), wbd)


def _inproj_kernel(x_ref, gmix_ref, win_ref, gq_ref, wq_ref, gkv_ref, wkv_ref, ab_ref, cos_ref, sin_ref,
                   ec_ref, es_ref, hsum_ref, v12_ref, q_ref, k_ref, v_ref, ksq_ref, vscr_ref):
    h = _rms(x_ref[...], gmix_ref[...]).astype(BF16)
    u = jnp.dot(h, win_ref[...], preferred_element_type=F32)

    v12 = jnp.dot(u[:, :F_WIDTH].astype(BF16), ab_ref[...], preferred_element_type=F32)
    half_rows = vscr_ref.shape[1] // 2
    for c in range(vscr_ref.shape[0]):
        cols = slice(LANES * c, LANES * (c + 1))
        vscr_ref[c] = v12[:, cols]
        v12_ref[0, :, cols] = vscr_ref[c, pl.ds(0, half_rows, stride=2), :].astype(BF16)
        v12_ref[1, :, cols] = vscr_ref[c, pl.ds(1, half_rows, stride=2), :].astype(BF16)

    tab = (jnp.dot(cos_ref[...], ec_ref[...], precision=lax.Precision.HIGHEST, preferred_element_type=F32)
           + jnp.dot(sin_ref[...], es_ref[...], precision=lax.Precision.HIGHEST, preferred_element_type=F32))
    lane = lax.broadcasted_iota(I32, tab.shape, 1)
    c1 = jnp.where(lane < 64, 1.0, jnp.where(lane < 96, tab, 0.0))
    c2 = jnp.where(lane >= 96, tab, 0.0)

    q0 = F_WIDTH
    qn = _rms(u[:, q0:q0 + Q_LORA_RANK], gq_ref[...]).astype(BF16)
    qa = jnp.dot(qn, wq_ref[...], preferred_element_type=F32)
    for hd in range(MLA_HEADS):
        blk = qa[:, HEAD_PAD * hd:HEAD_PAD * (hd + 1)]
        q_ref[:, HEAD_PAD * hd:HEAD_PAD * (hd + 1)] = (
            blk * c1 + pltpu.roll(blk * c2, 96, 1)).astype(BF16)

    kv0 = q0 + Q_LORA_RANK
    kvn = _rms(u[:, kv0:kv0 + KV_LORA_RANK], gkv_ref[...]).astype(BF16)
    kv = jnp.dot(kvn, wkv_ref[...], preferred_element_type=F32)
    kr0 = kv0 + KV_LORA_RANK
    t = u[:, kr0:kr0 + LANES] * jnp.where(lane < 64, tab, 0.0)
    kr = t + pltpu.roll(t, 96, 1)
    kr = jnp.where((lane >= 64) & (lane < 96), pltpu.roll(kr, 64, 1), 0.0)
    k_blocks = [kv[:, HEAD_PAD * hd:HEAD_PAD * (hd + 1)] + kr for hd in range(MLA_HEADS)]
    for hd in range(MLA_HEADS):
        k_ref[:, HEAD_PAD * hd:HEAD_PAD * (hd + 1)] = k_blocks[hd].astype(BF16)
    v_ref[...] = kv[:, MLA_HEADS * HEAD_PAD:].astype(BF16)
    k_sq = jnp.concatenate([(kb * kb).astype(BF16) for kb in k_blocks], axis=1)
    ksq_ref[...] = jnp.max(jnp.dot(k_sq, hsum_ref[...], preferred_element_type=F32), axis=0, keepdims=True)


def _in_projection(x2d, g_mix, w_in, g_q, w_q_up, g_kv, w_kv_up, ab, cos, sin):
    w_kr = w_in[:, -QK_ROPE_DIM:]
    half = QK_ROPE_DIM // 2
    w_kr_rot = jnp.concatenate([-w_kr[:, half:], w_kr[:, :half]], axis=1)
    win = jnp.concatenate(
        [w_in, w_kr_rot, jnp.zeros((D_MODEL, IN_PAD - w_in.shape[1] - QK_ROPE_DIM), F32)], axis=1).astype(BF16)
    scale = (QK_NOPE_DIM + QK_ROPE_DIM) ** -0.5
    wq = w_q_up.reshape(Q_LORA_RANK, MLA_HEADS, QK_NOPE_DIM + QK_ROPE_DIM) * scale
    wq_rope = wq[:, :, QK_NOPE_DIM:]
    wq_rot = jnp.concatenate([-wq_rope[:, :, half:], wq_rope[:, :, :half]], axis=2)
    wq = jnp.concatenate([wq, wq_rot], axis=2).reshape(Q_LORA_RANK, MLA_HEADS * HEAD_PAD).astype(BF16)
    wkv = w_kv_up.reshape(KV_LORA_RANK, MLA_HEADS, QK_NOPE_DIM + V_HEAD_DIM)
    wk = jnp.concatenate([wkv[:, :, :QK_NOPE_DIM],
                          jnp.zeros((KV_LORA_RANK, MLA_HEADS, HEAD_PAD - QK_NOPE_DIM), F32)], axis=2)
    wkv = jnp.concatenate([wk.reshape(KV_LORA_RANK, MLA_HEADS * HEAD_PAD),
                           wkv[:, :, QK_NOPE_DIM:].reshape(KV_LORA_RANK, MLA_HEADS * V_HEAD_DIM)],
                          axis=1).astype(BF16)

    tm = TM_IN
    ec, es = _rope_expanders()
    hsum = jnp.asarray((np.arange(MLA_HEADS * HEAD_PAD)[:, None] // HEAD_PAD
                        == np.arange(LANES)[None, :]).astype(np.float32)).astype(BF16)
    full = lambda shape: pl.BlockSpec(shape, lambda i: (0,) * len(shape))
    tile = lambda w: pl.BlockSpec((tm, w), lambda i: (i, 0))
    per_half = HALF // tm
    per_seq = SEQ // tm
    v12_spec = pl.BlockSpec((None, 2, None, tm // 2, 2 * F_WIDTH),
                            lambda i: (i // per_seq, 0, (i % per_seq) // per_half, i % per_half, 0))
    return pl.pallas_call(
        _inproj_kernel,
        grid=(TOKENS // tm,),
        in_specs=[tile(D_MODEL), full((1, D_MODEL)), full(win.shape), full((1, Q_LORA_RANK)), full(wq.shape),
                  full((1, KV_LORA_RANK)), full(wkv.shape), full(ab.shape), tile(QK_ROPE_DIM // 2),
                  tile(QK_ROPE_DIM // 2), full(ec.shape), full(es.shape), full(hsum.shape)],
        out_specs=[v12_spec, tile(MLA_HEADS * HEAD_PAD), tile(MLA_HEADS * HEAD_PAD),
                   tile(MLA_HEADS * V_HEAD_DIM), pl.BlockSpec((None, 1, LANES), lambda i: (i, 0, 0))],
        out_shape=[jax.ShapeDtypeStruct((BATCH, 2, 2, QUARTER, 2 * F_WIDTH), BF16),
                   jax.ShapeDtypeStruct((TOKENS, MLA_HEADS * HEAD_PAD), BF16),
                   jax.ShapeDtypeStruct((TOKENS, MLA_HEADS * HEAD_PAD), BF16),
                   jax.ShapeDtypeStruct((TOKENS, MLA_HEADS * V_HEAD_DIM), BF16),
                   jax.ShapeDtypeStruct((TOKENS // tm, 1, LANES), F32)],
        scratch_shapes=[pltpu.VMEM((2 * F_WIDTH // LANES, tm, LANES), F32)],
        compiler_params=_params("parallel"),
        name="in_projection",
    )(x2d, g_mix.reshape(1, -1), win, g_q.reshape(1, -1), wq, g_kv.reshape(1, -1), wkv, ab, cos, sin, ec, es, hsum)


def _seq_dft_kernel(v_ref, m_ref, y_ref):
    sign = jnp.where(pl.program_id(1) == 0, 1.0, -1.0)
    parts = []
    for q in range(2):
        ab = (v_ref[q, 0].astype(F32) + sign * v_ref[q, 1].astype(F32)).astype(BF16)
        parts.append(jnp.dot(m_ref[q, :, :QUARTER], ab[:, :F_WIDTH], preferred_element_type=F32)
                     + jnp.dot(m_ref[q, :, QUARTER:], ab[:, F_WIDTH:], preferred_element_type=F32))
    y_ref[0] = (parts[0] + parts[1]).astype(BF16)
    y_ref[1] = (parts[0] - parts[1]).astype(BF16)


def _seq_dft_mats():
    i = np.arange(QUARTER)
    out = np.zeros((2, 2, QUARTER, 2 * QUARTER), np.float32)
    for p in range(2):
        for q in range(2):
            prod = np.outer(2 * i + p, 2 * i + q) % SEQ
            ang = 2.0 * np.pi * prod / SEQ
            out[p, q, :, :QUARTER] = np.cos(ang) / np.sqrt(SEQ)
            out[p, q, :, QUARTER:] = -np.sin(ang) / np.sqrt(SEQ)
    return out


def _sequence_dft(v):
    mats = jnp.asarray(_seq_dft_mats()).astype(BF16)
    return pl.pallas_call(
        _seq_dft_kernel,
        grid=(BATCH, 2),
        in_specs=[pl.BlockSpec((None, 2, 2, QUARTER, 2 * F_WIDTH), lambda b, p: (b, 0, 0, 0, 0)),
                  pl.BlockSpec((None, 2, QUARTER, 2 * QUARTER), lambda b, p: (p, 0, 0, 0))],
        out_specs=pl.BlockSpec((None, None, 2, QUARTER, F_WIDTH), lambda b, p: (b, p, 0, 0, 0)),
        out_shape=jax.ShapeDtypeStruct((BATCH, 2, 2, QUARTER, F_WIDTH), BF16),
        compiler_params=_params("parallel", "arbitrary"),
        name="sequence_dft",
    )(v, mats)


def _mla_kernel(q_ref, k_ref, v_ref, ksq_ref, o_ref):
    v = v_ref[...]
    hp = pl.program_id(1)
    ksq = jnp.max(ksq_ref[...], axis=0, keepdims=True)
    head_lane = lax.broadcasted_iota(I32, ksq.shape, 1)
    for j in range(2):
        cols = slice(HEAD_PAD * j, HEAD_PAD * (j + 1))
        out_cols = slice(V_HEAD_DIM * j, V_HEAD_DIM * (j + 1))
        q = q_ref[:, cols]
        k_norm = jnp.sqrt(jnp.max(jnp.where(head_lane == 2 * hp + j, ksq, 0.0), axis=1, keepdims=True))
        qf = q.astype(F32)
        bound = jnp.sqrt(jnp.sum(qf * qf, axis=1, keepdims=True)) * (k_norm * NORM_SLACK)
        safe = jnp.max(bound) <= SAFE_SHIFT

        def attend(row_shift):
            s = lax.dot_general(q, k_ref[:, cols], (((1,), (1,)), ((), ())), preferred_element_type=F32)
            p = jnp.exp(s - row_shift(s))
            l = jnp.sum(p, axis=1, keepdims=True)
            o = jnp.dot(p.astype(BF16), v, preferred_element_type=F32) / l
            o_ref[:, out_cols] = o[:, out_cols].astype(BF16)

        pl.when(safe)(lambda: attend(lambda s: bound))
        pl.when(jnp.logical_not(safe))(lambda: attend(lambda s: jnp.max(s, axis=1, keepdims=True)))


def _mla_attention(q, k, v, ksq):
    q = q.reshape(BATCH, SEQ, -1)
    k = k.reshape(BATCH, SEQ, -1)
    v = v.reshape(BATCH, SEQ, -1)
    ksq = ksq.reshape(BATCH, SEQ // TM_IN, LANES)
    out = pl.pallas_call(
        _mla_kernel,
        grid=(BATCH, MLA_HEADS // 2, SEQ // TQ),
        in_specs=[pl.BlockSpec((None, TQ, 2 * HEAD_PAD), lambda b, hp, i: (b, i, hp)),
                  pl.BlockSpec((None, SEQ, 2 * HEAD_PAD), lambda b, hp, i: (b, 0, hp)),
                  pl.BlockSpec((None, SEQ, 2 * V_HEAD_DIM), lambda b, hp, i: (b, 0, hp)),
                  pl.BlockSpec((None, SEQ // TM_IN, LANES), lambda b, hp, i: (b, 0, 0))],
        out_specs=pl.BlockSpec((None, TQ, 2 * V_HEAD_DIM), lambda b, hp, i: (b, i, hp)),
        out_shape=jax.ShapeDtypeStruct((BATCH, SEQ, MLA_HEADS * V_HEAD_DIM), BF16),
        compiler_params=_params("parallel", "parallel", "arbitrary"),
        name="mla_attention",
    )(q, k, v, ksq)
    return out.reshape(TOKENS, MLA_HEADS * V_HEAD_DIM)


def _memkv_kernel(mem_ref, g_ref, w_ref, k_ref, v_ref):
    mn = _rms(mem_ref[...], g_ref[...]).astype(BF16)
    kv = jnp.dot(mn, w_ref[...], preferred_element_type=F32)
    k_ref[...] = kv[:, :D_MODEL].astype(BF16)
    v_ref[...] = kv[:, D_MODEL:].astype(BF16)


def _memory_kv(mem, g_mem_kv, w_mem_kv):
    blk = pl.BlockSpec((None, MEM_LEN, D_MODEL), lambda b: (b, 0, 0))
    return pl.pallas_call(
        _memkv_kernel,
        grid=(BATCH,),
        in_specs=[blk, pl.BlockSpec((1, D_MODEL), lambda b: (0, 0)),
                  pl.BlockSpec((D_MODEL, 2 * D_MODEL), lambda b: (0, 0))],
        out_specs=[blk, blk],
        out_shape=[jax.ShapeDtypeStruct((BATCH, MEM_LEN, D_MODEL), BF16)] * 2,
        compiler_params=_params("parallel"),
        name="memory_kv",
    )(mem, g_mem_kv.reshape(1, -1), w_mem_kv.astype(BF16))


def _mix_kernel(x_ref, yf_ref, ya_ref, wo_ref, gq_ref, wmq_ref, mk_ref, mv_ref, wmo_ref, gf_ref, wrh_ref,
                wrl_ref, x2_ref, hext_ref, aff_ref, zscr_ref):
    half_rows = zscr_ref.shape[1] // 2
    wo_f = wo_ref[:F_WIDTH, :]
    z_even = jnp.dot(yf_ref[0], wo_f, preferred_element_type=F32)
    z_odd = jnp.dot(yf_ref[1], wo_f, preferred_element_type=F32)
    for c in range(zscr_ref.shape[0]):
        cols = slice(LANES * c, LANES * (c + 1))
        zscr_ref[c, pl.ds(0, half_rows, stride=2), :] = z_even[:, cols]
        zscr_ref[c, pl.ds(1, half_rows, stride=2), :] = z_odd[:, cols]
    z = jnp.concatenate([zscr_ref[c] for c in range(zscr_ref.shape[0])], axis=1)
    x1 = x_ref[...] + z + jnp.dot(ya_ref[...], wo_ref[F_WIDTH:, :], preferred_element_type=F32)

    hq = _rms(x1, gq_ref[...]).astype(BF16)
    qm = (jnp.dot(hq, wmq_ref[...], preferred_element_type=F32) * (MEM_HEAD_DIM ** -0.5)).astype(BF16)
    heads = []
    for hd in range(MEM_HEADS):
        sl = slice(MEM_HEAD_DIM * hd, MEM_HEAD_DIM * (hd + 1))
        s = lax.dot_general(qm[:, sl], mk_ref[:, sl], (((1,), (1,)), ((), ())), preferred_element_type=F32)
        p = jnp.exp(s - jnp.max(s, axis=1, keepdims=True))
        l = jnp.sum(p, axis=1, keepdims=True)
        heads.append((jnp.dot(p.astype(BF16), mv_ref[:, sl], preferred_element_type=F32) / l).astype(BF16))
    o = jnp.concatenate(heads, axis=1)
    x2 = x1 + jnp.dot(o, wmo_ref[...], preferred_element_type=F32)
    x2_ref[...] = x2

    h3 = _rms(x2, gf_ref[...])
    h3_hi = h3.astype(BF16)
    hext_ref[:, :D_MODEL] = h3_hi
    h3_lo = (h3 - h3_hi.astype(F32)).astype(BF16)
    logits = (jnp.dot(h3_hi, wrh_ref[...], preferred_element_type=F32)
              + jnp.dot(h3_lo, wrh_ref[...], preferred_element_type=F32)
              + jnp.dot(h3_hi, wrl_ref[...], preferred_element_type=F32))
    lane = lax.broadcasted_iota(I32, logits.shape, 1)
    logits = jnp.where(lane < N_EXPERTS, logits, -jnp.inf)
    e = jnp.exp(logits - jnp.max(logits, axis=1, keepdims=True))
    aff = e / jnp.sum(e, axis=1, keepdims=True)
    aff_ref[...] = aff
    hi = aff.astype(BF16)
    r1 = aff - hi.astype(F32)
    mid = r1.astype(BF16)
    lo = (r1 - mid.astype(F32)).astype(BF16)
    hext_ref[:, D_MODEL:] = jnp.where(
        lane < N_EXPERTS, hi,
        jnp.where(lane < 2 * N_EXPERTS, pltpu.roll(mid.astype(F32), N_EXPERTS, 1).astype(BF16),
                  pltpu.roll(lo.astype(F32), 2 * N_EXPERTS, 1).astype(BF16)))


def _mixing(x2d, y_f, y_a, w_out, g_mem_q, w_mem_q, mk, mv, w_mem_o, g_ffn, w_router):
    tm = TM_MIX
    wr = jnp.concatenate([w_router, jnp.zeros((D_MODEL, LANES - N_EXPERTS), F32)], axis=1)
    wr_hi = wr.astype(BF16)
    wr_lo = (wr - wr_hi.astype(F32)).astype(BF16)
    full = lambda shape: pl.BlockSpec(shape, lambda i: (0,) * len(shape))
    tile = lambda w: pl.BlockSpec((tm, w), lambda i: (i, 0))
    per_half = HALF // tm
    per_seq = SEQ // tm
    per_batch = pl.BlockSpec((None, MEM_LEN, D_MODEL), lambda i: (i // per_seq, 0, 0))
    yf_spec = pl.BlockSpec((None, 2, None, tm // 2, F_WIDTH),
                           lambda i: (i // per_seq, 0, (i % per_seq) // per_half, i % per_half, 0))
    return pl.pallas_call(
        _mix_kernel,
        grid=(TOKENS // tm,),
        in_specs=[tile(D_MODEL), yf_spec, tile(F_WIDTH), full((D_MODEL, D_MODEL)), full((1, D_MODEL)),
                  full((D_MODEL, D_MODEL)), per_batch, per_batch, full((D_MODEL, D_MODEL)), full((1, D_MODEL)),
                  full((D_MODEL, LANES)), full((D_MODEL, LANES))],
        out_specs=[tile(D_MODEL), tile(D_MODEL + LANES), tile(LANES)],
        out_shape=[jax.ShapeDtypeStruct((TOKENS, D_MODEL), F32),
                   jax.ShapeDtypeStruct((TOKENS, D_MODEL + LANES), BF16),
                   jax.ShapeDtypeStruct((TOKENS, LANES), F32)],
        scratch_shapes=[pltpu.VMEM((D_MODEL // LANES, tm, LANES), F32)],
        compiler_params=_params("parallel"),
        name="mix_memattn_router",
    )(x2d, y_f, y_a, w_out.astype(BF16), g_mem_q.reshape(1, -1), w_mem_q.astype(BF16), mk, mv,
      w_mem_o.astype(BF16), g_ffn.reshape(1, -1), wr_hi, wr_lo)


def _topk_kernel(aff_ref, slot_ref, offs_ref):
    aff = aff_ref[...]
    rows = aff.shape[0]

    thr = jnp.zeros((rows, 1), I32)
    for bit in range(30, -1, -1):
        cand = thr | (1 << bit)
        cnt = jnp.sum(jnp.where(aff >= pltpu.bitcast(cand, F32), 1.0, 0.0), axis=1, keepdims=True)
        thr = jnp.where(cnt >= CAPACITY, cand, thr)
    thr_f = pltpu.bitcast(thr, F32)

    chunk = 256
    r = lax.broadcasted_iota(I32, (chunk, chunk), 0)
    c = lax.broadcasted_iota(I32, (chunk, chunk), 1)
    tri = jnp.where(r < c, 1.0, 0.0).astype(BF16)

    def exclusive_count(mask):
        off = jnp.zeros((rows, 1), F32)
        outs = []
        for j in range(SEQ // chunk):
            mj = mask[:, chunk * j:chunk * (j + 1)]
            outs.append(jnp.dot(mj.astype(BF16), tri, preferred_element_type=F32) + off)
            off = off + jnp.sum(mj, axis=1, keepdims=True)
        return jnp.concatenate(outs, axis=1), off

    gt = aff > thr_f
    tie = jnp.where(aff == thr_f, 1.0, 0.0)
    n_gt = jnp.sum(jnp.where(gt, 1.0, 0.0), axis=1, keepdims=True)
    tie_rank, _ = exclusive_count(tie)
    sel = jnp.where(gt | ((tie > 0.0) & (tie_rank < CAPACITY - n_gt)), 1.0, 0.0)
    slot, _ = exclusive_count(sel)
    slot_ref[...] = jnp.where(sel > 0.0, slot.astype(I32), -1)
    tok = lax.broadcasted_iota(I32, (SEQ, LANES), 0)
    j = lax.broadcasted_iota(I32, (SEQ, LANES), 1)
    before = jnp.where(tok < j * TOKEN_CHUNK, 1.0, 0.0).astype(BF16)
    offs_ref[...] = jnp.dot(sel.astype(BF16), before, preferred_element_type=F32).astype(I32)


def _expert_slots(aff):
    aff_t = aff[:, :N_EXPERTS].reshape(BATCH, SEQ, N_EXPERTS).transpose(0, 2, 1).reshape(BATCH * N_EXPERTS, SEQ)
    slots, offs = pl.pallas_call(
        _topk_kernel,
        out_shape=[jax.ShapeDtypeStruct((BATCH * N_EXPERTS, SEQ), I32),
                   jax.ShapeDtypeStruct((BATCH * N_EXPERTS, LANES), I32)],
        compiler_params=pltpu.CompilerParams(vmem_limit_bytes=VMEM_LIMIT),
        name="expert_topk",
    )(aff_t)
    return slots, offs[:, :OFFS_STRIDE].reshape(-1)


def _window_start(first):
    start = jnp.minimum((first >> SLOT_ALIGN_SHIFT) << SLOT_ALIGN_SHIFT, LAST_WIN)
    return pl.multiple_of(start, 1 << SLOT_ALIGN_SHIFT)


def _gather_kernel(offs_ref, slot_ref, h_ref, x_ref):
    b = pl.program_id(0)
    e0 = pl.program_id(1) * GATHER_EXPERTS
    j = pl.program_id(2)

    @pl.when(j == 0)
    def _():
        x_ref[...] = jnp.zeros_like(x_ref)

    def bounds(e):
        base = (b * N_EXPERTS + e0 + e) * OFFS_STRIDE + j
        return offs_ref[base], offs_ref[base + 1]

    row = lax.broadcasted_iota(I32, (SLOT_WIN, TOKEN_CHUNK), 0)
    h_c = h_ref[...]
    starts = [_window_start(bounds(e)[0]) for e in range(GATHER_EXPERTS)]
    onehot = jnp.concatenate(
        [jnp.where(row + starts[e] == slot_ref[e:e + 1, :], 1.0, 0.0).astype(BF16)
         for e in range(GATHER_EXPERTS)], axis=0)
    picked = jnp.dot(onehot, h_c, preferred_element_type=F32)
    for e in range(GATHER_EXPERTS):
        rows = pl.ds(e * CAPACITY + starts[e], SLOT_WIN)
        x_ref[rows, :] = (x_ref[rows, :].astype(F32)
                          + picked[e * SLOT_WIN:(e + 1) * SLOT_WIN]).astype(BF16)

    for e in range(GATHER_EXPERTS):
        _, end = bounds(e)
        covered = starts[e] + SLOT_WIN
        slot_e = slot_ref[e:e + 1, :]

        def extra_window(i, carry):
            lo = covered + i * SLOT_WIN
            r0 = pl.multiple_of(jnp.minimum(lo, LAST_WIN), 1 << SLOT_ALIGN_SHIFT)
            hot = jnp.where((row + r0 == slot_e) & (slot_e >= lo), 1.0, 0.0).astype(BF16)
            rows = pl.ds(e * CAPACITY + r0, SLOT_WIN)
            x_ref[rows, :] = (x_ref[rows, :].astype(F32)
                              + jnp.dot(hot, h_c, preferred_element_type=F32)).astype(BF16)
            return carry

        lax.fori_loop(0, jnp.maximum(end - covered + SLOT_WIN - 1, 0) // SLOT_WIN, extra_window, 0)


def _gather(offs, slots, h_ext):
    slots = slots.reshape(BATCH, N_EXPERTS, SEQ)
    h_ext = h_ext.reshape(BATCH, SEQ, D_MODEL + LANES)
    return pl.pallas_call(
        _gather_kernel,
        grid_spec=pltpu.PrefetchScalarGridSpec(
            num_scalar_prefetch=1,
            grid=(BATCH, N_EXPERTS // GATHER_EXPERTS, SEQ // TOKEN_CHUNK),
            in_specs=[pl.BlockSpec((None, GATHER_EXPERTS, TOKEN_CHUNK), lambda b, g, j, offs: (b, g, j)),
                      pl.BlockSpec((None, TOKEN_CHUNK, D_MODEL + LANES), lambda b, g, j, offs: (b, j, 0))],
            out_specs=pl.BlockSpec((None, GATHER_EXPERTS * CAPACITY, D_MODEL + LANES),
                                   lambda b, g, j, offs: (b, g, 0))),
        out_shape=jax.ShapeDtypeStruct((BATCH, N_EXPERTS * CAPACITY, D_MODEL + LANES), BF16),
        compiler_params=_params("parallel", "parallel", "arbitrary"),
        name="expert_gather",
    )(offs, slots, h_ext)


def _expert_kernel(x_ref, wg_ref, wu_ref, wd_ref, y_ref, wg_s, wu_s, wd_s):
    @pl.when(pl.program_id(1) == 0)
    def _():
        wg_s[...] = wg_ref[...].astype(BF16)
        wu_s[...] = wu_ref[...].astype(BF16)
        wd_s[...] = wd_ref[...].astype(BF16)

    xin = x_ref[:, :D_MODEL]
    ext = x_ref[:, D_MODEL:].astype(F32)
    lane = lax.broadcasted_iota(I32, ext.shape, 1)
    e = pl.program_id(0)
    mine = (lane == e) | (lane == e + N_EXPERTS) | (lane == e + 2 * N_EXPERTS)
    gate = jnp.sum(jnp.where(mine, ext, 0.0), axis=1, keepdims=True)

    a = jnp.dot(xin, wg_s[...], preferred_element_type=F32)
    b = jnp.dot(xin, wu_s[...], preferred_element_type=F32)
    hid = (a / (1.0 + jnp.exp(-a)) * b).astype(BF16)
    y = jnp.dot(hid, wd_s[...], preferred_element_type=F32)
    y_ref[...] = (y * gate).astype(BF16)


def _experts(xin, w_gate, w_up, w_down):
    wspec = pl.BlockSpec((None, D_MODEL, D_MODEL), lambda e, b: (e, 0, 0))
    return pl.pallas_call(
        _expert_kernel,
        grid=(N_EXPERTS, BATCH),
        in_specs=[pl.BlockSpec((None, CAPACITY, D_MODEL + LANES), lambda e, b: (b, e, 0)),
                  wspec, wspec, wspec],
        out_specs=pl.BlockSpec((None, CAPACITY, D_MODEL), lambda e, b: (b, e, 0)),
        out_shape=jax.ShapeDtypeStruct((BATCH, N_EXPERTS * CAPACITY, D_MODEL), BF16),
        scratch_shapes=[pltpu.VMEM((D_MODEL, D_MODEL), BF16)] * 3,
        compiler_params=_params("arbitrary", "arbitrary"),
        name="expert_ffn",
    )(xin, w_gate, w_up, w_down)


def _combine_kernel(offs_ref, x2_ref, slot_ref, y_ref, g_ref, o_ref):
    b = pl.program_id(0)
    j = pl.program_id(1)
    lane = lax.broadcasted_iota(I32, (TOKEN_CHUNK, SLOT_WIN), 1)
    slot = slot_ref[...]

    def bounds(e):
        base = (b * N_EXPERTS + e) * OFFS_STRIDE + j
        return offs_ref[base], offs_ref[base + 1]

    starts = [_window_start(bounds(e)[0]) for e in range(N_EXPERTS)]
    onehot = jnp.concatenate(
        [jnp.where(lane + starts[e] == slot[:, e:e + 1], 1.0, 0.0).astype(BF16) for e in range(N_EXPERTS)],
        axis=1)
    rows = jnp.concatenate(
        [y_ref[pl.ds(e * CAPACITY + starts[e], SLOT_WIN), :] for e in range(N_EXPERTS)], axis=0)
    o_ref[...] = x2_ref[...] + jnp.dot(onehot, rows, preferred_element_type=F32)

    tail_lane = lax.broadcasted_iota(I32, (TOKEN_CHUNK, LAST_WIN), 1) + SLOT_WIN
    for e in range(N_EXPERTS):
        _, end = bounds(e)
        covered = starts[e] + SLOT_WIN

        @pl.when(end > covered)
        def _():
            sl = slot[:, e:e + 1]
            hot = jnp.where((tail_lane == sl) & (sl >= covered), 1.0, 0.0).astype(BF16)
            o_ref[...] += jnp.dot(hot, y_ref[e * CAPACITY + SLOT_WIN:(e + 1) * CAPACITY, :],
                                  preferred_element_type=F32)

    o_ref[...] = _rms(o_ref[...], g_ref[...])


def _combine(offs, x2, slots, y, g_final):
    slots_t = slots.reshape(BATCH, N_EXPERTS, SEQ).transpose(0, 2, 1)
    x2 = x2.reshape(BATCH, SEQ, D_MODEL)
    tm = TOKEN_CHUNK
    return pl.pallas_call(
        _combine_kernel,
        grid_spec=pltpu.PrefetchScalarGridSpec(
            num_scalar_prefetch=1,
            grid=(BATCH, SEQ // tm),
            in_specs=[pl.BlockSpec((None, tm, D_MODEL), lambda b, i, offs: (b, i, 0)),
                      pl.BlockSpec((None, tm, N_EXPERTS), lambda b, i, offs: (b, i, 0)),
                      pl.BlockSpec((None, N_EXPERTS * CAPACITY, D_MODEL), lambda b, i, offs: (b, 0, 0)),
                      pl.BlockSpec((1, D_MODEL), lambda b, i, offs: (0, 0))],
            out_specs=pl.BlockSpec((None, tm, D_MODEL), lambda b, i, offs: (b, i, 0))),
        out_shape=jax.ShapeDtypeStruct((BATCH, SEQ, D_MODEL), F32),
        compiler_params=_params("parallel", "arbitrary"),
        name="combine_final_norm",
    )(offs, x2, slots_t, y, g_final.reshape(1, -1))


def kernel(x, mem, positions, g_mix, w_in, g_q_lat, w_q_up, g_kv_lat, w_kv_up, w_fourier, w_out, g_mem_q,
           g_mem_kv, w_mem_q, w_mem_kv, w_mem_o, g_ffn, w_router, w_exp_gate, w_exp_up, w_exp_down, g_final):
    assert x.shape == (BATCH, SEQ, D_MODEL) and g_mix.shape[0] == 1
    x2d = x.reshape(TOKENS, D_MODEL)
    cos, sin = _rope_tables(positions)
    ab = _channel_mats(w_fourier[0])
    v12, q, k, v, ksq = _in_projection(x2d, g_mix[0], w_in[0], g_q_lat[0], w_q_up[0], g_kv_lat[0], w_kv_up[0],
                                       ab, cos, sin)
    y_f = _sequence_dft(v12)
    y_a = _mla_attention(q, k, v, ksq)
    mk, mv = _memory_kv(mem, g_mem_kv[0], w_mem_kv[0])
    x2, h_ext, aff = _mixing(x2d, y_f, y_a, w_out[0], g_mem_q[0], w_mem_q[0], mk, mv, w_mem_o[0], g_ffn[0],
                             w_router[0])
    slots, offs = _expert_slots(aff)
    xin = _gather(offs, slots, h_ext)
    y = _experts(xin, w_exp_gate[0], w_exp_up[0], w_exp_down[0])
    return _combine(offs, x2, slots, y, g_final)
```

```python
import functools

import numpy as np
import jax
import jax.numpy as jnp
from jax import lax
from jax.experimental import pallas as pl
from jax.experimental.pallas import tpu as pltpu

F32 = jnp.float32
BF16 = jnp.bfloat16
I32 = jnp.int32

D_MODEL = 1024
BATCH = 4
SEQ = 4096
TOKENS = BATCH * SEQ
MEM_LEN = 256
RMS_EPS = 1e-6
F_GROUPS = 8
F_GROUP_DIM = 64
F_WIDTH = F_GROUPS * F_GROUP_DIM
MLA_HEADS = 8
QK_NOPE_DIM = 64
QK_ROPE_DIM = 32
V_HEAD_DIM = 64
Q_LORA_RANK = 384
KV_LORA_RANK = 256
ROPE_THETA = 10000.0
MEM_HEADS = 4
MEM_HEAD_DIM = D_MODEL // MEM_HEADS
N_EXPERTS = 16
CAPACITY = 2 * SEQ // N_EXPERTS

NORM_SLACK = 1.01
SAFE_SHIFT = 30.0

LANES = 128
HEAD_PAD = 128
IN_PAD = 1280
HALF = SEQ // 2
QUARTER = SEQ // 4
VMEM_LIMIT = 56 * 1024 * 1024

TM_IN = 1024
IN_GROUPS = 1
TM_MIX = 1024
MIX_GROUPS = 2
TQ = 1024
TOKEN_CHUNK = 512
OFFS_STRIDE = SEQ // TOKEN_CHUNK + 1
GATHER_WIN = 96
SCATTER_WIN = 128
SLOT_ALIGN_SHIFT = 4
GATHER_EXPERTS = 8


def _rms(x, g):
    return x * lax.rsqrt(jnp.mean(x * x, axis=-1, keepdims=True) + RMS_EPS) * g


def _params(*sem):
    return pltpu.CompilerParams(dimension_semantics=sem, vmem_limit_bytes=VMEM_LIMIT)


def _rope_kernel(pos_ref, freq_ref, cos_ref, sin_ref):
    ang = pos_ref[...] * freq_ref[...]
    cos_ref[...] = jnp.cos(ang)
    sin_ref[...] = jnp.sin(ang)


def _rope_tables(positions):
    half = QK_ROPE_DIM // 2
    freqs = 1.0 / (ROPE_THETA ** (jnp.arange(0, QK_ROPE_DIM, 2, dtype=F32) / QK_ROPE_DIM))
    rows = TOKENS * half // LANES
    pos = jnp.repeat(positions.astype(F32).reshape(-1), half).reshape(rows, LANES)
    freq = jnp.tile(freqs, LANES // half).reshape(1, LANES)
    cos, sin = pl.pallas_call(
        _rope_kernel,
        out_shape=(jax.ShapeDtypeStruct((rows, LANES), F32),) * 2,
        name="rope_tables",
    )(pos, freq)
    return cos.reshape(TOKENS, half), sin.reshape(TOKENS, half)


def _rope_expanders():
    half = QK_ROPE_DIM // 2
    lane = np.arange(LANES)
    hit = (lane[None, :] % half) == np.arange(half)[:, None]
    is_sin = (lane // QK_ROPE_DIM) % 2 == 1
    return (jnp.asarray((hit & ~is_sin[None, :]).astype(np.float32)),
            jnp.asarray((hit & is_sin[None, :]).astype(np.float32)))


def _chan_kernel(cbd_ref, sbd_ref, w_ref, ab_ref):
    w = w_ref[...]
    ab_ref[:, :F_WIDTH] = jnp.dot(cbd_ref[...], w, precision=lax.Precision.HIGHEST,
                                  preferred_element_type=F32).astype(BF16)
    ab_ref[:, F_WIDTH:] = jnp.dot(sbd_ref[...], w, precision=lax.Precision.HIGHEST,
                                  preferred_element_type=F32).astype(BF16)


def _channel_mats(w_fourier):
    c = np.arange(F_GROUP_DIM)
    ang = 2.0 * np.pi * np.outer(c, c) / F_GROUP_DIM
    scale = F_GROUP_DIM ** -0.5
    eye = np.eye(F_GROUPS)
    cbd = np.kron(eye, np.cos(ang) * scale).astype(np.float32)
    sbd = np.kron(eye, np.sin(ang) * scale).astype(np.float32)
    wbd = (jnp.eye(F_GROUPS, dtype=F32)[:, None, :, None] * w_fourier[:, :, None, :]).reshape(F_WIDTH, F_WIDTH)
    return pl.pallas_call(
        _chan_kernel,
        out_shape=jax.ShapeDtypeStruct((F_WIDTH, 2 * F_WIDTH), BF16),
        name="channel_dft_fold",
    )(jnp.asarray(cbd), jnp.asarray(sbd), wbd)


def _inproj_kernel(x_ref, gmix_ref, win_ref, gq_ref, wq_ref, gkv_ref, wkv_ref, ab_ref, cos_ref, sin_ref,
                   ec_ref, es_ref, hsum_ref, v12_ref, q_ref, k_ref, v_ref, ksq_ref, vscr_ref):
    rows_per_group = x_ref.shape[0] // IN_GROUPS
    half_rows = rows_per_group // 2
    ksq_max = None
    for grp in range(IN_GROUPS):
        r0 = grp * rows_per_group
        rows = pl.ds(r0, rows_per_group)
        half = pl.ds(r0 // 2, half_rows)
        h = _rms(x_ref[rows, :], gmix_ref[...]).astype(BF16)
        u = jnp.dot(h, win_ref[...], preferred_element_type=F32)

        v12 = jnp.dot(u[:, :F_WIDTH].astype(BF16), ab_ref[...], preferred_element_type=F32)
        for c in range(vscr_ref.shape[0]):
            cols = slice(LANES * c, LANES * (c + 1))
            vscr_ref[c, rows, :] = v12[:, cols]
            v12_ref[0, half, cols] = vscr_ref[c, pl.ds(r0, half_rows, stride=2), :].astype(BF16)
            v12_ref[1, half, cols] = vscr_ref[c, pl.ds(r0 + 1, half_rows, stride=2), :].astype(BF16)

        tab = (jnp.dot(cos_ref[rows, :], ec_ref[...], precision=lax.Precision.HIGHEST,
                       preferred_element_type=F32)
               + jnp.dot(sin_ref[rows, :], es_ref[...], precision=lax.Precision.HIGHEST,
                         preferred_element_type=F32))
        lane = lax.broadcasted_iota(I32, tab.shape, 1)
        c1 = jnp.where(lane < 64, 1.0, jnp.where(lane < 96, tab, 0.0))
        c2 = jnp.where(lane >= 96, tab, 0.0)

        q0 = F_WIDTH
        qn = _rms(u[:, q0:q0 + Q_LORA_RANK], gq_ref[...]).astype(BF16)
        qa = jnp.dot(qn, wq_ref[...], preferred_element_type=F32)
        for hd in range(MLA_HEADS):
            blk = qa[:, HEAD_PAD * hd:HEAD_PAD * (hd + 1)]
            q_ref[rows, HEAD_PAD * hd:HEAD_PAD * (hd + 1)] = (
                blk * c1 + pltpu.roll(blk * c2, 96, 1)).astype(BF16)

        kv0 = q0 + Q_LORA_RANK
        kvn = _rms(u[:, kv0:kv0 + KV_LORA_RANK], gkv_ref[...]).astype(BF16)
        kv = jnp.dot(kvn, wkv_ref[...], preferred_element_type=F32)
        kr0 = kv0 + KV_LORA_RANK
        t = u[:, kr0:kr0 + LANES] * jnp.where(lane < 64, tab, 0.0)
        kr = t + pltpu.roll(t, 96, 1)
        kr = jnp.where((lane >= 64) & (lane < 96), pltpu.roll(kr, 64, 1), 0.0)
        k_blocks = [kv[:, HEAD_PAD * hd:HEAD_PAD * (hd + 1)] + kr for hd in range(MLA_HEADS)]
        for hd in range(MLA_HEADS):
            k_ref[rows, HEAD_PAD * hd:HEAD_PAD * (hd + 1)] = k_blocks[hd].astype(BF16)
        v_ref[rows, :] = kv[:, MLA_HEADS * HEAD_PAD:].astype(BF16)
        k_sq = jnp.concatenate([(kb * kb).astype(BF16) for kb in k_blocks], axis=1)
        grp_max = jnp.max(jnp.dot(k_sq, hsum_ref[...], preferred_element_type=F32), axis=0, keepdims=True)
        ksq_max = grp_max if ksq_max is None else jnp.maximum(ksq_max, grp_max)
    ksq_ref[...] = ksq_max


def _in_projection(x2d, g_mix, w_in, g_q, w_q_up, g_kv, w_kv_up, ab, cos, sin):
    w_kr = w_in[:, -QK_ROPE_DIM:]
    half = QK_ROPE_DIM // 2
    w_kr_rot = jnp.concatenate([-w_kr[:, half:], w_kr[:, :half]], axis=1)
    win = jnp.concatenate(
        [w_in, w_kr_rot, jnp.zeros((D_MODEL, IN_PAD - w_in.shape[1] - QK_ROPE_DIM), F32)], axis=1).astype(BF16)
    scale = (QK_NOPE_DIM + QK_ROPE_DIM) ** -0.5
    wq = w_q_up.reshape(Q_LORA_RANK, MLA_HEADS, QK_NOPE_DIM + QK_ROPE_DIM) * scale
    wq_rope = wq[:, :, QK_NOPE_DIM:]
    wq_rot = jnp.concatenate([-wq_rope[:, :, half:], wq_rope[:, :, :half]], axis=2)
    wq = jnp.concatenate([wq, wq_rot], axis=2).reshape(Q_LORA_RANK, MLA_HEADS * HEAD_PAD).astype(BF16)
    wkv = w_kv_up.reshape(KV_LORA_RANK, MLA_HEADS, QK_NOPE_DIM + V_HEAD_DIM)
    wk = jnp.concatenate([wkv[:, :, :QK_NOPE_DIM],
                          jnp.zeros((KV_LORA_RANK, MLA_HEADS, HEAD_PAD - QK_NOPE_DIM), F32)], axis=2)
    wkv = jnp.concatenate([wk.reshape(KV_LORA_RANK, MLA_HEADS * HEAD_PAD),
                           wkv[:, :, QK_NOPE_DIM:].reshape(KV_LORA_RANK, MLA_HEADS * V_HEAD_DIM)],
                          axis=1).astype(BF16)

    tm = TM_IN
    ec, es = _rope_expanders()
    hsum = jnp.asarray((np.arange(MLA_HEADS * HEAD_PAD)[:, None] // HEAD_PAD
                        == np.arange(LANES)[None, :]).astype(np.float32)).astype(BF16)
    full = lambda shape: pl.BlockSpec(shape, lambda i: (0,) * len(shape))
    tile = lambda w: pl.BlockSpec((tm, w), lambda i: (i, 0))
    per_half = HALF // tm
    per_seq = SEQ // tm
    v12_spec = pl.BlockSpec((None, 2, None, tm // 2, 2 * F_WIDTH),
                            lambda i: (i // per_seq, 0, (i % per_seq) // per_half, i % per_half, 0))
    return pl.pallas_call(
        _inproj_kernel,
        grid=(TOKENS // tm,),
        in_specs=[tile(D_MODEL), full((1, D_MODEL)), full(win.shape), full((1, Q_LORA_RANK)), full(wq.shape),
                  full((1, KV_LORA_RANK)), full(wkv.shape), full(ab.shape), tile(QK_ROPE_DIM // 2),
                  tile(QK_ROPE_DIM // 2), full(ec.shape), full(es.shape), full(hsum.shape)],
        out_specs=[v12_spec, tile(MLA_HEADS * HEAD_PAD), tile(MLA_HEADS * HEAD_PAD),
                   tile(MLA_HEADS * V_HEAD_DIM), pl.BlockSpec((None, 1, LANES), lambda i: (i, 0, 0))],
        out_shape=[jax.ShapeDtypeStruct((BATCH, 2, 2, QUARTER, 2 * F_WIDTH), BF16),
                   jax.ShapeDtypeStruct((TOKENS, MLA_HEADS * HEAD_PAD), BF16),
                   jax.ShapeDtypeStruct((TOKENS, MLA_HEADS * HEAD_PAD), BF16),
                   jax.ShapeDtypeStruct((TOKENS, MLA_HEADS * V_HEAD_DIM), BF16),
                   jax.ShapeDtypeStruct((TOKENS // tm, 1, LANES), F32)],
        scratch_shapes=[pltpu.VMEM((2 * F_WIDTH // LANES, tm, LANES), F32)],
        compiler_params=_params("parallel"),
        name="in_projection",
    )(x2d, g_mix.reshape(1, -1), win, g_q.reshape(1, -1), wq, g_kv.reshape(1, -1), wkv, ab, cos, sin, ec, es, hsum)


def _seq_dft_kernel(v_ref, m_ref, y_ref):
    sign = jnp.where(pl.program_id(1) == 0, 1.0, -1.0)
    parts = []
    for q in range(2):
        ab = (v_ref[q, 0].astype(F32) + sign * v_ref[q, 1].astype(F32)).astype(BF16)
        parts.append(jnp.dot(m_ref[q, :, :QUARTER], ab[:, :F_WIDTH], preferred_element_type=F32)
                     + jnp.dot(m_ref[q, :, QUARTER:], ab[:, F_WIDTH:], preferred_element_type=F32))
    y_ref[0] = (parts[0] + parts[1]).astype(BF16)
    y_ref[1] = (parts[0] - parts[1]).astype(BF16)


def _seq_dft_mats():
    i = np.arange(QUARTER)
    out = np.zeros((2, 2, QUARTER, 2 * QUARTER), np.float32)
    for p in range(2):
        for q in range(2):
            prod = np.outer(2 * i + p, 2 * i + q) % SEQ
            ang = 2.0 * np.pi * prod / SEQ
            out[p, q, :, :QUARTER] = np.cos(ang) / np.sqrt(SEQ)
            out[p, q, :, QUARTER:] = -np.sin(ang) / np.sqrt(SEQ)
    return out


def _sequence_dft(v):
    mats = jnp.asarray(_seq_dft_mats()).astype(BF16)
    return pl.pallas_call(
        _seq_dft_kernel,
        grid=(BATCH, 2),
        in_specs=[pl.BlockSpec((None, 2, 2, QUARTER, 2 * F_WIDTH), lambda b, p: (b, 0, 0, 0, 0)),
                  pl.BlockSpec((None, 2, QUARTER, 2 * QUARTER), lambda b, p: (p, 0, 0, 0))],
        out_specs=pl.BlockSpec((None, None, 2, QUARTER, F_WIDTH), lambda b, p: (b, p, 0, 0, 0)),
        out_shape=jax.ShapeDtypeStruct((BATCH, 2, 2, QUARTER, F_WIDTH), BF16),
        compiler_params=_params("parallel", "arbitrary"),
        name="sequence_dft",
    )(v, mats)


def _mla_kernel(q_ref, k_ref, v_ref, ksq_ref, o_ref):
    v = v_ref[...]
    hp = pl.program_id(1)
    ksq = jnp.max(ksq_ref[...], axis=0, keepdims=True)
    head_lane = lax.broadcasted_iota(I32, ksq.shape, 1)
    for j in range(2):
        cols = slice(HEAD_PAD * j, HEAD_PAD * (j + 1))
        out_cols = slice(V_HEAD_DIM * j, V_HEAD_DIM * (j + 1))
        q = q_ref[:, cols]
        k_norm = jnp.sqrt(jnp.max(jnp.where(head_lane == 2 * hp + j, ksq, 0.0), axis=1, keepdims=True))
        qf = q.astype(F32)
        bound = jnp.sqrt(jnp.sum(qf * qf, axis=1, keepdims=True)) * (k_norm * NORM_SLACK)
        safe = jnp.max(bound) <= SAFE_SHIFT

        def attend(row_shift):
            s = lax.dot_general(q, k_ref[:, cols], (((1,), (1,)), ((), ())), preferred_element_type=F32)
            p = jnp.exp(s - row_shift(s))
            l = jnp.sum(p, axis=1, keepdims=True)
            o = jnp.dot(p.astype(BF16), v, preferred_element_type=F32) / l
            o_ref[:, out_cols] = o[:, out_cols].astype(BF16)

        pl.when(safe)(lambda: attend(lambda s: bound))
        pl.when(jnp.logical_not(safe))(lambda: attend(lambda s: jnp.max(s, axis=1, keepdims=True)))


def _mla_attention(q, k, v, ksq):
    q = q.reshape(BATCH, SEQ, -1)
    k = k.reshape(BATCH, SEQ, -1)
    v = v.reshape(BATCH, SEQ, -1)
    ksq = ksq.reshape(BATCH, SEQ // TM_IN, LANES)
    out = pl.pallas_call(
        _mla_kernel,
        grid=(BATCH, MLA_HEADS // 2, SEQ // TQ),
        in_specs=[pl.BlockSpec((None, TQ, 2 * HEAD_PAD), lambda b, hp, i: (b, i, hp)),
                  pl.BlockSpec((None, SEQ, 2 * HEAD_PAD), lambda b, hp, i: (b, 0, hp)),
                  pl.BlockSpec((None, SEQ, 2 * V_HEAD_DIM), lambda b, hp, i: (b, 0, hp)),
                  pl.BlockSpec((None, SEQ // TM_IN, LANES), lambda b, hp, i: (b, 0, 0))],
        out_specs=pl.BlockSpec((None, TQ, 2 * V_HEAD_DIM), lambda b, hp, i: (b, i, hp)),
        out_shape=jax.ShapeDtypeStruct((BATCH, SEQ, MLA_HEADS * V_HEAD_DIM), BF16),
        compiler_params=_params("parallel", "parallel", "arbitrary"),
        name="mla_attention",
    )(q, k, v, ksq)
    return out.reshape(TOKENS, MLA_HEADS * V_HEAD_DIM)


def _memkv_kernel(mem_ref, g_ref, w_ref, k_ref, v_ref):
    mn = _rms(mem_ref[...], g_ref[...]).astype(BF16)
    kv = jnp.dot(mn, w_ref[...], preferred_element_type=F32)
    k_ref[...] = kv[:, :D_MODEL].astype(BF16)
    v_ref[...] = kv[:, D_MODEL:].astype(BF16)


def _memory_kv(mem, g_mem_kv, w_mem_kv):
    blk = pl.BlockSpec((None, MEM_LEN, D_MODEL), lambda b: (b, 0, 0))
    return pl.pallas_call(
        _memkv_kernel,
        grid=(BATCH,),
        in_specs=[blk, pl.BlockSpec((1, D_MODEL), lambda b: (0, 0)),
                  pl.BlockSpec((D_MODEL, 2 * D_MODEL), lambda b: (0, 0))],
        out_specs=[blk, blk],
        out_shape=[jax.ShapeDtypeStruct((BATCH, MEM_LEN, D_MODEL), BF16)] * 2,
        compiler_params=_params("parallel"),
        name="memory_kv",
    )(mem, g_mem_kv.reshape(1, -1), w_mem_kv.astype(BF16))


def _mix_kernel(x_ref, yf_ref, ya_ref, wo_ref, gq_ref, wmq_ref, mk_ref, mv_ref, wmo_ref, gf_ref, wrh_ref,
                wrl_ref, x2_ref, hext_ref, aff_ref, zscr_ref):
    tm = x_ref.shape[0]
    rows_per_group = tm // MIX_GROUPS
    for grp in range(MIX_GROUPS):
        rows = pl.ds(grp * rows_per_group, rows_per_group)
        half = pl.ds(grp * rows_per_group // 2, rows_per_group // 2)
        wo_f = wo_ref[:F_WIDTH, :]
        z_even = jnp.dot(yf_ref[0, half, :], wo_f, preferred_element_type=F32)
        z_odd = jnp.dot(yf_ref[1, half, :], wo_f, preferred_element_type=F32)
        for c in range(zscr_ref.shape[0]):
            cols = slice(LANES * c, LANES * (c + 1))
            zscr_ref[c, pl.ds(grp * rows_per_group, rows_per_group // 2, stride=2), :] = z_even[:, cols]
            zscr_ref[c, pl.ds(grp * rows_per_group + 1, rows_per_group // 2, stride=2), :] = z_odd[:, cols]
        z = jnp.concatenate([zscr_ref[c, rows, :] for c in range(zscr_ref.shape[0])], axis=1)
        x1 = x_ref[rows, :] + z + jnp.dot(ya_ref[rows, :], wo_ref[F_WIDTH:, :], preferred_element_type=F32)

        hq = _rms(x1, gq_ref[...]).astype(BF16)
        qm = (jnp.dot(hq, wmq_ref[...], preferred_element_type=F32) * (MEM_HEAD_DIM ** -0.5)).astype(BF16)
        heads = []
        for hd in range(MEM_HEADS):
            sl = slice(MEM_HEAD_DIM * hd, MEM_HEAD_DIM * (hd + 1))
            s = lax.dot_general(qm[:, sl], mk_ref[:, sl], (((1,), (1,)), ((), ())),
                                preferred_element_type=F32)
            p = jnp.exp(s - jnp.max(s, axis=1, keepdims=True))
            l = jnp.sum(p, axis=1, keepdims=True)
            heads.append((jnp.dot(p.astype(BF16), mv_ref[:, sl], preferred_element_type=F32) / l).astype(BF16))
        o = jnp.concatenate(heads, axis=1)
        x2 = x1 + jnp.dot(o, wmo_ref[...], preferred_element_type=F32)
        x2_ref[rows, :] = x2

        h3 = _rms(x2, gf_ref[...])
        h3_hi = h3.astype(BF16)
        hext_ref[rows, :D_MODEL] = h3_hi
        h3_lo = (h3 - h3_hi.astype(F32)).astype(BF16)
        logits = (jnp.dot(h3_hi, wrh_ref[...], preferred_element_type=F32)
                  + jnp.dot(h3_lo, wrh_ref[...], preferred_element_type=F32)
                  + jnp.dot(h3_hi, wrl_ref[...], preferred_element_type=F32))
        lane = lax.broadcasted_iota(I32, logits.shape, 1)
        logits = jnp.where(lane < N_EXPERTS, logits, -jnp.inf)
        e = jnp.exp(logits - jnp.max(logits, axis=1, keepdims=True))
        aff = e / jnp.sum(e, axis=1, keepdims=True)
        aff_ref[rows, :] = aff
        hi = aff.astype(BF16)
        r1 = aff - hi.astype(F32)
        mid = r1.astype(BF16)
        lo = (r1 - mid.astype(F32)).astype(BF16)
        hext_ref[rows, D_MODEL:] = jnp.where(
            lane < N_EXPERTS, hi,
            jnp.where(lane < 2 * N_EXPERTS, pltpu.roll(mid.astype(F32), N_EXPERTS, 1).astype(BF16),
                      pltpu.roll(lo.astype(F32), 2 * N_EXPERTS, 1).astype(BF16)))


def _mixing(x2d, y_f, y_a, w_out, g_mem_q, w_mem_q, mk, mv, w_mem_o, g_ffn, w_router):
    tm = TM_MIX
    wr = jnp.concatenate([w_router, jnp.zeros((D_MODEL, LANES - N_EXPERTS), F32)], axis=1)
    wr_hi = wr.astype(BF16)
    wr_lo = (wr - wr_hi.astype(F32)).astype(BF16)
    full = lambda shape: pl.BlockSpec(shape, lambda i: (0,) * len(shape))
    tile = lambda w: pl.BlockSpec((tm, w), lambda i: (i, 0))
    per_half = HALF // tm
    per_seq = SEQ // tm
    per_batch = pl.BlockSpec((None, MEM_LEN, D_MODEL), lambda i: (i // per_seq, 0, 0))
    yf_spec = pl.BlockSpec((None, 2, None, tm // 2, F_WIDTH),
                           lambda i: (i // per_seq, 0, (i % per_seq) // per_half, i % per_half, 0))
    return pl.pallas_call(
        _mix_kernel,
        grid=(TOKENS // tm,),
        in_specs=[tile(D_MODEL), yf_spec, tile(F_WIDTH), full((D_MODEL, D_MODEL)), full((1, D_MODEL)),
                  full((D_MODEL, D_MODEL)), per_batch, per_batch, full((D_MODEL, D_MODEL)), full((1, D_MODEL)),
                  full((D_MODEL, LANES)), full((D_MODEL, LANES))],
        out_specs=[tile(D_MODEL), tile(D_MODEL + LANES), tile(LANES)],
        out_shape=[jax.ShapeDtypeStruct((TOKENS, D_MODEL), F32),
                   jax.ShapeDtypeStruct((TOKENS, D_MODEL + LANES), BF16),
                   jax.ShapeDtypeStruct((TOKENS, LANES), F32)],
        scratch_shapes=[pltpu.VMEM((D_MODEL // LANES, tm, LANES), F32)],
        compiler_params=_params("parallel"),
        name="mix_memattn_router",
    )(x2d, y_f, y_a, w_out.astype(BF16), g_mem_q.reshape(1, -1), w_mem_q.astype(BF16), mk, mv,
      w_mem_o.astype(BF16), g_ffn.reshape(1, -1), wr_hi, wr_lo)


def _topk_kernel(aff_ref, slot_ref, offs_ref):
    aff = aff_ref[...]
    rows = aff.shape[0]

    thr = jnp.zeros((rows, 1), I32)
    for bit in range(30, -1, -1):
        cand = thr | (1 << bit)
        cnt = jnp.sum(jnp.where(aff >= pltpu.bitcast(cand, F32), 1.0, 0.0), axis=1, keepdims=True)
        thr = jnp.where(cnt >= CAPACITY, cand, thr)
    thr_f = pltpu.bitcast(thr, F32)

    chunk = 256
    r = lax.broadcasted_iota(I32, (chunk, chunk), 0)
    c = lax.broadcasted_iota(I32, (chunk, chunk), 1)
    tri = jnp.where(r < c, 1.0, 0.0).astype(BF16)

    def exclusive_count(mask):
        off = jnp.zeros((rows, 1), F32)
        outs = []
        for j in range(SEQ // chunk):
            mj = mask[:, chunk * j:chunk * (j + 1)]
            outs.append(jnp.dot(mj.astype(BF16), tri, preferred_element_type=F32) + off)
            off = off + jnp.sum(mj, axis=1, keepdims=True)
        return jnp.concatenate(outs, axis=1), off

    gt = aff > thr_f
    tie = jnp.where(aff == thr_f, 1.0, 0.0)
    n_gt = jnp.sum(jnp.where(gt, 1.0, 0.0), axis=1, keepdims=True)
    tie_rank, _ = exclusive_count(tie)
    sel = jnp.where(gt | ((tie > 0.0) & (tie_rank < CAPACITY - n_gt)), 1.0, 0.0)
    slot, _ = exclusive_count(sel)
    slot_ref[...] = jnp.where(sel > 0.0, slot.astype(I32), -1)
    tok = lax.broadcasted_iota(I32, (SEQ, LANES), 0)
    j = lax.broadcasted_iota(I32, (SEQ, LANES), 1)
    before = jnp.where(tok < j * TOKEN_CHUNK, 1.0, 0.0).astype(BF16)
    offs_ref[...] = jnp.dot(sel.astype(BF16), before, preferred_element_type=F32).astype(I32)


def _expert_slots(aff):
    aff_t = aff[:, :N_EXPERTS].reshape(BATCH, SEQ, N_EXPERTS).transpose(0, 2, 1).reshape(BATCH * N_EXPERTS, SEQ)
    slots, offs = pl.pallas_call(
        _topk_kernel,
        out_shape=[jax.ShapeDtypeStruct((BATCH * N_EXPERTS, SEQ), I32),
                   jax.ShapeDtypeStruct((BATCH * N_EXPERTS, LANES), I32)],
        compiler_params=pltpu.CompilerParams(vmem_limit_bytes=VMEM_LIMIT),
        name="expert_topk",
    )(aff_t)
    return slots, offs[:, :OFFS_STRIDE].reshape(-1)


def _window_start(first, win):
    start = jnp.minimum((first >> SLOT_ALIGN_SHIFT) << SLOT_ALIGN_SHIFT, CAPACITY - win)
    return pl.multiple_of(start, 1 << SLOT_ALIGN_SHIFT)


def _gather_kernel(offs_ref, slot_ref, h_ref, x_ref):
    win = GATHER_WIN
    last = CAPACITY - win
    b = pl.program_id(0)
    e0 = pl.program_id(1) * GATHER_EXPERTS
    j = pl.program_id(2)

    @pl.when(j == 0)
    def _():
        x_ref[...] = jnp.zeros_like(x_ref)

    def bounds(e):
        base = (b * N_EXPERTS + e0 + e) * OFFS_STRIDE + j
        return offs_ref[base], offs_ref[base + 1]

    row = lax.broadcasted_iota(I32, (win, TOKEN_CHUNK), 0)
    h_c = h_ref[...]
    starts = [_window_start(bounds(e)[0], win) for e in range(GATHER_EXPERTS)]
    onehot = jnp.concatenate(
        [jnp.where(row + starts[e] == slot_ref[e:e + 1, :], 1.0, 0.0).astype(BF16)
         for e in range(GATHER_EXPERTS)], axis=0)
    picked = jnp.dot(onehot, h_c, preferred_element_type=F32)
    for e in range(GATHER_EXPERTS):
        rows = pl.ds(e * CAPACITY + starts[e], win)
        x_ref[rows, :] = (x_ref[rows, :].astype(F32) + picked[e * win:(e + 1) * win]).astype(BF16)

    for e in range(GATHER_EXPERTS):
        _, end = bounds(e)
        covered = starts[e] + win
        slot_e = slot_ref[e:e + 1, :]

        def extra_window(i, carry):
            lo = covered + i * win
            r0 = pl.multiple_of(jnp.minimum(lo, last), 1 << SLOT_ALIGN_SHIFT)
            hot = jnp.where((row + r0 == slot_e) & (slot_e >= lo), 1.0, 0.0).astype(BF16)
            rows = pl.ds(e * CAPACITY + r0, win)
            x_ref[rows, :] = (x_ref[rows, :].astype(F32)
                              + jnp.dot(hot, h_c, preferred_element_type=F32)).astype(BF16)
            return carry

        lax.fori_loop(0, jnp.maximum(end - covered + win - 1, 0) // win, extra_window, 0)


def _gather(offs, slots, h_ext):
    slots = slots.reshape(BATCH, N_EXPERTS, SEQ)
    h_ext = h_ext.reshape(BATCH, SEQ, D_MODEL + LANES)
    return pl.pallas_call(
        _gather_kernel,
        grid_spec=pltpu.PrefetchScalarGridSpec(
            num_scalar_prefetch=1,
            grid=(BATCH, N_EXPERTS // GATHER_EXPERTS, SEQ // TOKEN_CHUNK),
            in_specs=[pl.BlockSpec((None, GATHER_EXPERTS, TOKEN_CHUNK), lambda b, g, j, offs: (b, g, j)),
                      pl.BlockSpec((None, TOKEN_CHUNK, D_MODEL + LANES), lambda b, g, j, offs: (b, j, 0))],
            out_specs=pl.BlockSpec((None, GATHER_EXPERTS * CAPACITY, D_MODEL + LANES),
                                   lambda b, g, j, offs: (b, g, 0))),
        out_shape=jax.ShapeDtypeStruct((BATCH, N_EXPERTS * CAPACITY, D_MODEL + LANES), BF16),
        compiler_params=_params("parallel", "parallel", "arbitrary"),
        name="expert_gather",
    )(offs, slots, h_ext)


def _expert_kernel(x_ref, wg_ref, wu_ref, wd_ref, y_ref, wg_s, wu_s, wd_s):
    @pl.when(pl.program_id(1) == 0)
    def _():
        wg_s[...] = wg_ref[...].astype(BF16)
        wu_s[...] = wu_ref[...].astype(BF16)
        wd_s[...] = wd_ref[...].astype(BF16)

    xin = x_ref[:, :D_MODEL]
    ext = x_ref[:, D_MODEL:].astype(F32)
    lane = lax.broadcasted_iota(I32, ext.shape, 1)
    e = pl.program_id(0)
    mine = (lane == e) | (lane == e + N_EXPERTS) | (lane == e + 2 * N_EXPERTS)
    gate = jnp.sum(jnp.where(mine, ext, 0.0), axis=1, keepdims=True)

    a = jnp.dot(xin, wg_s[...], preferred_element_type=F32)
    b = jnp.dot(xin, wu_s[...], preferred_element_type=F32)
    hid = (a / (1.0 + jnp.exp(-a)) * b).astype(BF16)
    y = jnp.dot(hid, wd_s[...], preferred_element_type=F32)
    y_ref[...] = (y * gate).astype(BF16)


def _experts(xin, w_gate, w_up, w_down):
    wspec = pl.BlockSpec((None, D_MODEL, D_MODEL), lambda e, b: (e, 0, 0))
    return pl.pallas_call(
        _expert_kernel,
        grid=(N_EXPERTS, BATCH),
        in_specs=[pl.BlockSpec((None, CAPACITY, D_MODEL + LANES), lambda e, b: (b, e, 0)),
                  wspec, wspec, wspec],
        out_specs=pl.BlockSpec((None, CAPACITY, D_MODEL), lambda e, b: (b, e, 0)),
        out_shape=jax.ShapeDtypeStruct((BATCH, N_EXPERTS * CAPACITY, D_MODEL), BF16),
        scratch_shapes=[pltpu.VMEM((D_MODEL, D_MODEL), BF16)] * 3,
        compiler_params=_params("arbitrary", "arbitrary"),
        name="expert_ffn",
    )(xin, w_gate, w_up, w_down)


def _combine_kernel(offs_ref, x2_ref, slot_ref, y_ref, g_ref, o_ref):
    b = pl.program_id(0)
    j = pl.program_id(1)
    win = SCATTER_WIN
    lane = lax.broadcasted_iota(I32, (TOKEN_CHUNK, win), 1)
    slot = slot_ref[...]

    def bounds(e):
        base = (b * N_EXPERTS + e) * OFFS_STRIDE + j
        return offs_ref[base], offs_ref[base + 1]

    starts = [_window_start(bounds(e)[0], win) for e in range(N_EXPERTS)]
    onehot = jnp.concatenate(
        [jnp.where(lane + starts[e] == slot[:, e:e + 1], 1.0, 0.0).astype(BF16) for e in range(N_EXPERTS)],
        axis=1)
    rows = jnp.concatenate(
        [y_ref[pl.ds(e * CAPACITY + starts[e], win), :] for e in range(N_EXPERTS)], axis=0)
    o_ref[...] = x2_ref[...] + jnp.dot(onehot, rows, preferred_element_type=F32)

    tail_lane = lax.broadcasted_iota(I32, (TOKEN_CHUNK, CAPACITY - win), 1) + win
    for e in range(N_EXPERTS):
        _, end = bounds(e)
        covered = starts[e] + win

        @pl.when(end > covered)
        def _():
            sl = slot[:, e:e + 1]
            hot = jnp.where((tail_lane == sl) & (sl >= covered), 1.0, 0.0).astype(BF16)
            o_ref[...] += jnp.dot(hot, y_ref[e * CAPACITY + win:(e + 1) * CAPACITY, :],
                                  preferred_element_type=F32)

    o_ref[...] = _rms(o_ref[...], g_ref[...])


def _combine(offs, x2, slots, y, g_final):
    slots_t = slots.reshape(BATCH, N_EXPERTS, SEQ).transpose(0, 2, 1)
    x2 = x2.reshape(BATCH, SEQ, D_MODEL)
    tm = TOKEN_CHUNK
    return pl.pallas_call(
        _combine_kernel,
        grid_spec=pltpu.PrefetchScalarGridSpec(
            num_scalar_prefetch=1,
            grid=(BATCH, SEQ // tm),
            in_specs=[pl.BlockSpec((None, tm, D_MODEL), lambda b, i, offs: (b, i, 0)),
                      pl.BlockSpec((None, tm, N_EXPERTS), lambda b, i, offs: (b, i, 0)),
                      pl.BlockSpec((None, N_EXPERTS * CAPACITY, D_MODEL), lambda b, i, offs: (b, 0, 0)),
                      pl.BlockSpec((1, D_MODEL), lambda b, i, offs: (0, 0))],
            out_specs=pl.BlockSpec((None, tm, D_MODEL), lambda b, i, offs: (b, i, 0))),
        out_shape=jax.ShapeDtypeStruct((BATCH, SEQ, D_MODEL), F32),
        compiler_params=_params("parallel", "arbitrary"),
        name="combine_final_norm",
    )(offs, x2, slots_t, y, g_final.reshape(1, -1))


def kernel(x, mem, positions, g_mix, w_in, g_q_lat, w_q_up, g_kv_lat, w_kv_up, w_fourier, w_out, g_mem_q,
           g_mem_kv, w_mem_q, w_mem_kv, w_mem_o, g_ffn, w_router, w_exp_gate, w_exp_up, w_exp_down, g_final):
    assert x.shape == (BATCH, SEQ, D_MODEL) and g_mix.shape[0] == 1
    x2d = x.reshape(TOKENS, D_MODEL)
    cos, sin = _rope_tables(positions)
    ab = _channel_mats(w_fourier[0])
    v12, q, k, v, ksq = _in_projection(x2d, g_mix[0], w_in[0], g_q_lat[0], w_q_up[0], g_kv_lat[0], w_kv_up[0],
                                       ab, cos, sin)
    y_f = _sequence_dft(v12)
    y_a = _mla_attention(q, k, v, ksq)
    mk, mv = _memory_kv(mem, g_mem_kv[0], w_mem_kv[0])
    x2, h_ext, aff = _mixing(x2d, y_f, y_a, w_out[0], g_mem_q[0], w_mem_q[0], mk, mv, w_mem_o[0], g_ffn[0],
                             w_router[0])
    slots, offs = _expert_slots(aff)
    xin = _gather(offs, slots, h_ext)
    y = _experts(xin, w_exp_gate[0], w_exp_up[0], w_exp_down[0])
    return _combine(offs, x2, slots, y, g_final)
```

```python
import functools

import numpy as np
import jax
import jax.numpy as jnp
from jax import lax
from jax.experimental import pallas as pl
from jax.experimental.pallas import tpu as pltpu

F32 = jnp.float32
BF16 = jnp.bfloat16
I32 = jnp.int32

D_MODEL = 1024
BATCH = 4
SEQ = 4096
TOKENS = BATCH * SEQ
MEM_LEN = 256
RMS_EPS = 1e-6
F_GROUPS = 8
F_GROUP_DIM = 64
F_WIDTH = F_GROUPS * F_GROUP_DIM
MLA_HEADS = 8
QK_NOPE_DIM = 64
QK_ROPE_DIM = 32
V_HEAD_DIM = 64
Q_LORA_RANK = 384
KV_LORA_RANK = 256
ROPE_THETA = 10000.0
MEM_HEADS = 4
MEM_HEAD_DIM = D_MODEL // MEM_HEADS
N_EXPERTS = 16
CAPACITY = 2 * SEQ // N_EXPERTS

NORM_SLACK = 1.01
SAFE_SHIFT = 30.0

LANES = 128
HEAD_PAD = 128
IN_PAD = 1280
HALF = SEQ // 2
QUARTER = SEQ // 4
VMEM_LIMIT = 56 * 1024 * 1024

TM_IN = 1024
IN_GROUPS = 1
TM_MIX = 1024
MIX_GROUPS = 2
TQ = 1024
TOKEN_CHUNK = 512
OFFS_STRIDE = SEQ // TOKEN_CHUNK + 1
GATHER_WIN = 96
SCATTER_WIN = 128
SLOT_ALIGN_SHIFT = 4
GATHER_EXPERTS = 8
FFN_SEQS = 2


def _rms(x, g):
    return x * lax.rsqrt(jnp.mean(x * x, axis=-1, keepdims=True) + RMS_EPS) * g


def _params(*sem):
    return pltpu.CompilerParams(dimension_semantics=sem, vmem_limit_bytes=VMEM_LIMIT)


def _rope_kernel(pos_ref, freq_ref, cos_ref, sin_ref):
    ang = pos_ref[...] * freq_ref[...]
    cos_ref[...] = jnp.cos(ang)
    sin_ref[...] = jnp.sin(ang)


def _rope_tables(positions):
    half = QK_ROPE_DIM // 2
    freqs = 1.0 / (ROPE_THETA ** (jnp.arange(0, QK_ROPE_DIM, 2, dtype=F32) / QK_ROPE_DIM))
    rows = TOKENS * half // LANES
    pos = jnp.repeat(positions.astype(F32).reshape(-1), half).reshape(rows, LANES)
    freq = jnp.tile(freqs, LANES // half).reshape(1, LANES)
    cos, sin = pl.pallas_call(
        _rope_kernel,
        out_shape=(jax.ShapeDtypeStruct((rows, LANES), F32),) * 2,
        name="rope_tables",
    )(pos, freq)
    return cos.reshape(TOKENS, half), sin.reshape(TOKENS, half)


def _rope_expanders():
    half = QK_ROPE_DIM // 2
    lane = np.arange(LANES)
    hit = (lane[None, :] % half) == np.arange(half)[:, None]
    is_sin = (lane // QK_ROPE_DIM) % 2 == 1
    return (jnp.asarray((hit & ~is_sin[None, :]).astype(np.float32)),
            jnp.asarray((hit & is_sin[None, :]).astype(np.float32)))


def _chan_kernel(cbd_ref, sbd_ref, w_ref, ab_ref):
    w = w_ref[...]
    ab_ref[:, :F_WIDTH] = jnp.dot(cbd_ref[...], w, precision=lax.Precision.HIGHEST,
                                  preferred_element_type=F32).astype(BF16)
    ab_ref[:, F_WIDTH:] = jnp.dot(sbd_ref[...], w, precision=lax.Precision.HIGHEST,
                                  preferred_element_type=F32).astype(BF16)


def _channel_mats(w_fourier):
    c = np.arange(F_GROUP_DIM)
    ang = 2.0 * np.pi * np.outer(c, c) / F_GROUP_DIM
    scale = F_GROUP_DIM ** -0.5
    eye = np.eye(F_GROUPS)
    cbd = np.kron(eye, np.cos(ang) * scale).astype(np.float32)
    sbd = np.kron(eye, np.sin(ang) * scale).astype(np.float32)
    wbd = (jnp.eye(F_GROUPS, dtype=F32)[:, None, :, None] * w_fourier[:, :, None, :]).reshape(F_WIDTH, F_WIDTH)
    return pl.pallas_call(
        _chan_kernel,
        out_shape=jax.ShapeDtypeStruct((F_WIDTH, 2 * F_WIDTH), BF16),
        name="channel_dft_fold",
    )(jnp.asarray(cbd), jnp.asarray(sbd), wbd)


def _inproj_kernel(x_ref, gmix_ref, win_ref, gq_ref, wq_ref, gkv_ref, wkv_ref, ab_ref, cos_ref, sin_ref,
                   ec_ref, es_ref, hsum_ref, v12_ref, q_ref, k_ref, v_ref, ksq_ref, vscr_ref):
    rows_per_group = x_ref.shape[0] // IN_GROUPS
    half_rows = rows_per_group // 2
    ksq_max = None
    for grp in range(IN_GROUPS):
        r0 = grp * rows_per_group
        rows = pl.ds(r0, rows_per_group)
        half = pl.ds(r0 // 2, half_rows)
        h = _rms(x_ref[rows, :], gmix_ref[...]).astype(BF16)
        u = jnp.dot(h, win_ref[...], preferred_element_type=F32)

        v12 = jnp.dot(u[:, :F_WIDTH].astype(BF16), ab_ref[...], preferred_element_type=F32)
        for c in range(vscr_ref.shape[0]):
            cols = slice(LANES * c, LANES * (c + 1))
            vscr_ref[c, rows, :] = v12[:, cols]
            v12_ref[0, half, cols] = vscr_ref[c, pl.ds(r0, half_rows, stride=2), :].astype(BF16)
            v12_ref[1, half, cols] = vscr_ref[c, pl.ds(r0 + 1, half_rows, stride=2), :].astype(BF16)

        tab = (jnp.dot(cos_ref[rows, :], ec_ref[...], precision=lax.Precision.HIGHEST,
                       preferred_element_type=F32)
               + jnp.dot(sin_ref[rows, :], es_ref[...], precision=lax.Precision.HIGHEST,
                         preferred_element_type=F32))
        lane = lax.broadcasted_iota(I32, tab.shape, 1)
        c1 = jnp.where(lane < 64, 1.0, jnp.where(lane < 96, tab, 0.0))
        c2 = jnp.where(lane >= 96, tab, 0.0)

        q0 = F_WIDTH
        qn = _rms(u[:, q0:q0 + Q_LORA_RANK], gq_ref[...]).astype(BF16)
        qa = jnp.dot(qn, wq_ref[...], preferred_element_type=F32)
        for hd in range(MLA_HEADS):
            blk = qa[:, HEAD_PAD * hd:HEAD_PAD * (hd + 1)]
            q_ref[rows, HEAD_PAD * hd:HEAD_PAD * (hd + 1)] = (
                blk * c1 + pltpu.roll(blk * c2, 96, 1)).astype(BF16)

        kv0 = q0 + Q_LORA_RANK
        kvn = _rms(u[:, kv0:kv0 + KV_LORA_RANK], gkv_ref[...]).astype(BF16)
        kv = jnp.dot(kvn, wkv_ref[...], preferred_element_type=F32)
        kr0 = kv0 + KV_LORA_RANK
        t = u[:, kr0:kr0 + LANES] * jnp.where(lane < 64, tab, 0.0)
        kr = t + pltpu.roll(t, 96, 1)
        kr = jnp.where((lane >= 64) & (lane < 96), pltpu.roll(kr, 64, 1), 0.0)
        k_blocks = [kv[:, HEAD_PAD * hd:HEAD_PAD * (hd + 1)] + kr for hd in range(MLA_HEADS)]
        for hd in range(MLA_HEADS):
            k_ref[rows, HEAD_PAD * hd:HEAD_PAD * (hd + 1)] = k_blocks[hd].astype(BF16)
        v_ref[rows, :] = kv[:, MLA_HEADS * HEAD_PAD:].astype(BF16)
        k_sq = jnp.concatenate([(kb * kb).astype(BF16) for kb in k_blocks], axis=1)
        grp_max = jnp.max(jnp.dot(k_sq, hsum_ref[...], preferred_element_type=F32), axis=0, keepdims=True)
        ksq_max = grp_max if ksq_max is None else jnp.maximum(ksq_max, grp_max)
    ksq_ref[...] = ksq_max


def _in_projection(x2d, g_mix, w_in, g_q, w_q_up, g_kv, w_kv_up, ab, cos, sin):
    w_kr = w_in[:, -QK_ROPE_DIM:]
    half = QK_ROPE_DIM // 2
    w_kr_rot = jnp.concatenate([-w_kr[:, half:], w_kr[:, :half]], axis=1)
    win = jnp.concatenate(
        [w_in, w_kr_rot, jnp.zeros((D_MODEL, IN_PAD - w_in.shape[1] - QK_ROPE_DIM), F32)], axis=1).astype(BF16)
    scale = (QK_NOPE_DIM + QK_ROPE_DIM) ** -0.5
    wq = w_q_up.reshape(Q_LORA_RANK, MLA_HEADS, QK_NOPE_DIM + QK_ROPE_DIM) * scale
    wq_rope = wq[:, :, QK_NOPE_DIM:]
    wq_rot = jnp.concatenate([-wq_rope[:, :, half:], wq_rope[:, :, :half]], axis=2)
    wq = jnp.concatenate([wq, wq_rot], axis=2).reshape(Q_LORA_RANK, MLA_HEADS * HEAD_PAD).astype(BF16)
    wkv = w_kv_up.reshape(KV_LORA_RANK, MLA_HEADS, QK_NOPE_DIM + V_HEAD_DIM)
    wk = jnp.concatenate([wkv[:, :, :QK_NOPE_DIM],
                          jnp.zeros((KV_LORA_RANK, MLA_HEADS, HEAD_PAD - QK_NOPE_DIM), F32)], axis=2)
    wkv = jnp.concatenate([wk.reshape(KV_LORA_RANK, MLA_HEADS * HEAD_PAD),
                           wkv[:, :, QK_NOPE_DIM:].reshape(KV_LORA_RANK, MLA_HEADS * V_HEAD_DIM)],
                          axis=1).astype(BF16)

    tm = TM_IN
    ec, es = _rope_expanders()
    hsum = jnp.asarray((np.arange(MLA_HEADS * HEAD_PAD)[:, None] // HEAD_PAD
                        == np.arange(LANES)[None, :]).astype(np.float32)).astype(BF16)
    full = lambda shape: pl.BlockSpec(shape, lambda i: (0,) * len(shape))
    tile = lambda w: pl.BlockSpec((tm, w), lambda i: (i, 0))
    per_half = HALF // tm
    per_seq = SEQ // tm
    v12_spec = pl.BlockSpec((None, 2, None, tm // 2, 2 * F_WIDTH),
                            lambda i: (i // per_seq, 0, (i % per_seq) // per_half, i % per_half, 0))
    return pl.pallas_call(
        _inproj_kernel,
        grid=(TOKENS // tm,),
        in_specs=[tile(D_MODEL), full((1, D_MODEL)), full(win.shape), full((1, Q_LORA_RANK)), full(wq.shape),
                  full((1, KV_LORA_RANK)), full(wkv.shape), full(ab.shape), tile(QK_ROPE_DIM // 2),
                  tile(QK_ROPE_DIM // 2), full(ec.shape), full(es.shape), full(hsum.shape)],
        out_specs=[v12_spec, tile(MLA_HEADS * HEAD_PAD), tile(MLA_HEADS * HEAD_PAD),
                   tile(MLA_HEADS * V_HEAD_DIM), pl.BlockSpec((None, 1, LANES), lambda i: (i, 0, 0))],
        out_shape=[jax.ShapeDtypeStruct((BATCH, 2, 2, QUARTER, 2 * F_WIDTH), BF16),
                   jax.ShapeDtypeStruct((TOKENS, MLA_HEADS * HEAD_PAD), BF16),
                   jax.ShapeDtypeStruct((TOKENS, MLA_HEADS * HEAD_PAD), BF16),
                   jax.ShapeDtypeStruct((TOKENS, MLA_HEADS * V_HEAD_DIM), BF16),
                   jax.ShapeDtypeStruct((TOKENS // tm, 1, LANES), F32)],
        scratch_shapes=[pltpu.VMEM((2 * F_WIDTH // LANES, tm, LANES), F32)],
        compiler_params=_params("parallel"),
        name="in_projection",
    )(x2d, g_mix.reshape(1, -1), win, g_q.reshape(1, -1), wq, g_kv.reshape(1, -1), wkv, ab, cos, sin, ec, es, hsum)


def _seq_dft_kernel(v_ref, m_ref, y_ref):
    sign = jnp.where(pl.program_id(1) == 0, 1.0, -1.0)
    parts = []
    for q in range(2):
        ab = (v_ref[q, 0].astype(F32) + sign * v_ref[q, 1].astype(F32)).astype(BF16)
        parts.append(jnp.dot(m_ref[q, :, :QUARTER], ab[:, :F_WIDTH], preferred_element_type=F32)
                     + jnp.dot(m_ref[q, :, QUARTER:], ab[:, F_WIDTH:], preferred_element_type=F32))
    y_ref[0] = (parts[0] + parts[1]).astype(BF16)
    y_ref[1] = (parts[0] - parts[1]).astype(BF16)


def _seq_dft_mats():
    i = np.arange(QUARTER)
    out = np.zeros((2, 2, QUARTER, 2 * QUARTER), np.float32)
    for p in range(2):
        for q in range(2):
            prod = np.outer(2 * i + p, 2 * i + q) % SEQ
            ang = 2.0 * np.pi * prod / SEQ
            out[p, q, :, :QUARTER] = np.cos(ang) / np.sqrt(SEQ)
            out[p, q, :, QUARTER:] = -np.sin(ang) / np.sqrt(SEQ)
    return out


def _sequence_dft(v):
    mats = jnp.asarray(_seq_dft_mats()).astype(BF16)
    return pl.pallas_call(
        _seq_dft_kernel,
        grid=(BATCH, 2),
        in_specs=[pl.BlockSpec((None, 2, 2, QUARTER, 2 * F_WIDTH), lambda b, p: (b, 0, 0, 0, 0)),
                  pl.BlockSpec((None, 2, QUARTER, 2 * QUARTER), lambda b, p: (p, 0, 0, 0))],
        out_specs=pl.BlockSpec((None, None, 2, QUARTER, F_WIDTH), lambda b, p: (b, p, 0, 0, 0)),
        out_shape=jax.ShapeDtypeStruct((BATCH, 2, 2, QUARTER, F_WIDTH), BF16),
        compiler_params=_params("parallel", "arbitrary"),
        name="sequence_dft",
    )(v, mats)


def _mla_kernel(q_ref, k_ref, v_ref, ksq_ref, o_ref):
    v = v_ref[...]
    hp = pl.program_id(1)
    ksq = jnp.max(ksq_ref[...], axis=0, keepdims=True)
    head_lane = lax.broadcasted_iota(I32, ksq.shape, 1)
    for j in range(2):
        cols = slice(HEAD_PAD * j, HEAD_PAD * (j + 1))
        out_cols = slice(V_HEAD_DIM * j, V_HEAD_DIM * (j + 1))
        q = q_ref[:, cols]
        k_norm = jnp.sqrt(jnp.max(jnp.where(head_lane == 2 * hp + j, ksq, 0.0), axis=1, keepdims=True))
        qf = q.astype(F32)
        bound = jnp.sqrt(jnp.sum(qf * qf, axis=1, keepdims=True)) * (k_norm * NORM_SLACK)
        safe = jnp.max(bound) <= SAFE_SHIFT

        def attend(row_shift):
            s = lax.dot_general(q, k_ref[:, cols], (((1,), (1,)), ((), ())), preferred_element_type=F32)
            p = jnp.exp(s - row_shift(s))
            l = jnp.sum(p, axis=1, keepdims=True)
            o = jnp.dot(p.astype(BF16), v, preferred_element_type=F32) / l
            o_ref[:, out_cols] = o[:, out_cols].astype(BF16)

        pl.when(safe)(lambda: attend(lambda s: bound))
        pl.when(jnp.logical_not(safe))(lambda: attend(lambda s: jnp.max(s, axis=1, keepdims=True)))


def _mla_attention(q, k, v, ksq):
    q = q.reshape(BATCH, SEQ, -1)
    k = k.reshape(BATCH, SEQ, -1)
    v = v.reshape(BATCH, SEQ, -1)
    ksq = ksq.reshape(BATCH, SEQ // TM_IN, LANES)
    out = pl.pallas_call(
        _mla_kernel,
        grid=(BATCH, MLA_HEADS // 2, SEQ // TQ),
        in_specs=[pl.BlockSpec((None, TQ, 2 * HEAD_PAD), lambda b, hp, i: (b, i, hp)),
                  pl.BlockSpec((None, SEQ, 2 * HEAD_PAD), lambda b, hp, i: (b, 0, hp)),
                  pl.BlockSpec((None, SEQ, 2 * V_HEAD_DIM), lambda b, hp, i: (b, 0, hp)),
                  pl.BlockSpec((None, SEQ // TM_IN, LANES), lambda b, hp, i: (b, 0, 0))],
        out_specs=pl.BlockSpec((None, TQ, 2 * V_HEAD_DIM), lambda b, hp, i: (b, i, hp)),
        out_shape=jax.ShapeDtypeStruct((BATCH, SEQ, MLA_HEADS * V_HEAD_DIM), BF16),
        compiler_params=_params("parallel", "parallel", "arbitrary"),
        name="mla_attention",
    )(q, k, v, ksq)
    return out.reshape(TOKENS, MLA_HEADS * V_HEAD_DIM)


def _memkv_kernel(mem_ref, g_ref, w_ref, k_ref, v_ref):
    mn = _rms(mem_ref[...], g_ref[...]).astype(BF16)
    kv = jnp.dot(mn, w_ref[...], preferred_element_type=F32)
    k_ref[...] = kv[:, :D_MODEL].astype(BF16)
    v_ref[...] = kv[:, D_MODEL:].astype(BF16)


def _memory_kv(mem, g_mem_kv, w_mem_kv):
    blk = pl.BlockSpec((None, MEM_LEN, D_MODEL), lambda b: (b, 0, 0))
    return pl.pallas_call(
        _memkv_kernel,
        grid=(BATCH,),
        in_specs=[blk, pl.BlockSpec((1, D_MODEL), lambda b: (0, 0)),
                  pl.BlockSpec((D_MODEL, 2 * D_MODEL), lambda b: (0, 0))],
        out_specs=[blk, blk],
        out_shape=[jax.ShapeDtypeStruct((BATCH, MEM_LEN, D_MODEL), BF16)] * 2,
        compiler_params=_params("parallel"),
        name="memory_kv",
    )(mem, g_mem_kv.reshape(1, -1), w_mem_kv.astype(BF16))


def _mix_kernel(x_ref, yf_ref, ya_ref, wo_ref, gq_ref, wmq_ref, mk_ref, mv_ref, wmo_ref, gf_ref, wr_ref,
                x2_ref, hext_ref, aff_ref, zscr_ref):
    tm = x_ref.shape[0]
    rows_per_group = tm // MIX_GROUPS
    for grp in range(MIX_GROUPS):
        rows = pl.ds(grp * rows_per_group, rows_per_group)
        half = pl.ds(grp * rows_per_group // 2, rows_per_group // 2)
        wo_f = wo_ref[:F_WIDTH, :]
        z_even = jnp.dot(yf_ref[0, half, :], wo_f, preferred_element_type=F32)
        z_odd = jnp.dot(yf_ref[1, half, :], wo_f, preferred_element_type=F32)
        for c in range(zscr_ref.shape[0]):
            cols = slice(LANES * c, LANES * (c + 1))
            zscr_ref[c, pl.ds(grp * rows_per_group, rows_per_group // 2, stride=2), :] = z_even[:, cols]
            zscr_ref[c, pl.ds(grp * rows_per_group + 1, rows_per_group // 2, stride=2), :] = z_odd[:, cols]
        z = jnp.concatenate([zscr_ref[c, rows, :] for c in range(zscr_ref.shape[0])], axis=1)
        x1 = x_ref[rows, :] + z + jnp.dot(ya_ref[rows, :], wo_ref[F_WIDTH:, :], preferred_element_type=F32)

        hq = _rms(x1, gq_ref[...]).astype(BF16)
        qm = (jnp.dot(hq, wmq_ref[...], preferred_element_type=F32) * (MEM_HEAD_DIM ** -0.5)).astype(BF16)
        heads = []
        for hd in range(MEM_HEADS):
            sl = slice(MEM_HEAD_DIM * hd, MEM_HEAD_DIM * (hd + 1))
            s = lax.dot_general(qm[:, sl], mk_ref[:, sl], (((1,), (1,)), ((), ())),
                                preferred_element_type=F32)
            p = jnp.exp(s - jnp.max(s, axis=1, keepdims=True))
            l = jnp.sum(p, axis=1, keepdims=True)
            heads.append((jnp.dot(p.astype(BF16), mv_ref[:, sl], preferred_element_type=F32) / l).astype(BF16))
        o = jnp.concatenate(heads, axis=1)
        x2 = x1 + jnp.dot(o, wmo_ref[...], preferred_element_type=F32)
        x2_ref[rows, :] = x2

        h3 = _rms(x2, gf_ref[...])
        h3_hi = h3.astype(BF16)
        hext_ref[rows, :D_MODEL] = h3_hi
        h3_lo = (h3 - h3_hi.astype(F32)).astype(BF16)
        hi_terms = jnp.dot(h3_hi, wr_ref[...], preferred_element_type=F32)
        logits = (hi_terms[:, :LANES] + hi_terms[:, LANES:]
                  + jnp.dot(h3_lo, wr_ref[:, :LANES], preferred_element_type=F32))
        lane = lax.broadcasted_iota(I32, logits.shape, 1)
        logits = jnp.where(lane < N_EXPERTS, logits, -jnp.inf)
        e = jnp.exp(logits - jnp.max(logits, axis=1, keepdims=True))
        aff = e / jnp.sum(e, axis=1, keepdims=True)
        aff_ref[rows, :] = aff
        hi = aff.astype(BF16)
        r1 = aff - hi.astype(F32)
        mid = r1.astype(BF16)
        lo = (r1 - mid.astype(F32)).astype(BF16)
        hext_ref[rows, D_MODEL:] = jnp.where(
            lane < N_EXPERTS, hi,
            jnp.where(lane < 2 * N_EXPERTS, pltpu.roll(mid.astype(F32), N_EXPERTS, 1).astype(BF16),
                      pltpu.roll(lo.astype(F32), 2 * N_EXPERTS, 1).astype(BF16)))


def _mixing(x2d, y_f, y_a, w_out, g_mem_q, w_mem_q, mk, mv, w_mem_o, g_ffn, w_router):
    tm = TM_MIX
    wr = jnp.concatenate([w_router, jnp.zeros((D_MODEL, LANES - N_EXPERTS), F32)], axis=1)
    wr_hi = wr.astype(BF16)
    wr_cat = jnp.concatenate([wr_hi, (wr - wr_hi.astype(F32)).astype(BF16)], axis=1)
    full = lambda shape: pl.BlockSpec(shape, lambda i: (0,) * len(shape))
    tile = lambda w: pl.BlockSpec((tm, w), lambda i: (i, 0))
    per_half = HALF // tm
    per_seq = SEQ // tm
    per_batch = pl.BlockSpec((None, MEM_LEN, D_MODEL), lambda i: (i // per_seq, 0, 0))
    yf_spec = pl.BlockSpec((None, 2, None, tm // 2, F_WIDTH),
                           lambda i: (i // per_seq, 0, (i % per_seq) // per_half, i % per_half, 0))
    return pl.pallas_call(
        _mix_kernel,
        grid=(TOKENS // tm,),
        in_specs=[tile(D_MODEL), yf_spec, tile(F_WIDTH), full((D_MODEL, D_MODEL)), full((1, D_MODEL)),
                  full((D_MODEL, D_MODEL)), per_batch, per_batch, full((D_MODEL, D_MODEL)), full((1, D_MODEL)),
                  full((D_MODEL, 2 * LANES))],
        out_specs=[tile(D_MODEL), tile(D_MODEL + LANES), tile(LANES)],
        out_shape=[jax.ShapeDtypeStruct((TOKENS, D_MODEL), F32),
                   jax.ShapeDtypeStruct((TOKENS, D_MODEL + LANES), BF16),
                   jax.ShapeDtypeStruct((TOKENS, LANES), F32)],
        scratch_shapes=[pltpu.VMEM((D_MODEL // LANES, tm, LANES), F32)],
        compiler_params=_params("parallel"),
        name="mix_memattn_router",
    )(x2d, y_f, y_a, w_out.astype(BF16), g_mem_q.reshape(1, -1), w_mem_q.astype(BF16), mk, mv,
      w_mem_o.astype(BF16), g_ffn.reshape(1, -1), wr_cat)


def _topk_kernel(aff_ref, slot_ref, offs_ref):
    aff = aff_ref[...]
    rows = aff.shape[0]

    thr = jnp.zeros((rows, 1), I32)
    for bit in range(30, -1, -1):
        cand = thr | (1 << bit)
        cnt = jnp.sum(jnp.where(aff >= pltpu.bitcast(cand, F32), 1.0, 0.0), axis=1, keepdims=True)
        thr = jnp.where(cnt >= CAPACITY, cand, thr)
    thr_f = pltpu.bitcast(thr, F32)

    chunk = 256
    r = lax.broadcasted_iota(I32, (chunk, chunk), 0)
    c = lax.broadcasted_iota(I32, (chunk, chunk), 1)
    tri = jnp.where(r < c, 1.0, 0.0).astype(BF16)

    def exclusive_count(mask):
        off = jnp.zeros((rows, 1), F32)
        outs = []
        for j in range(SEQ // chunk):
            mj = mask[:, chunk * j:chunk * (j + 1)]
            outs.append(jnp.dot(mj.astype(BF16), tri, preferred_element_type=F32) + off)
            off = off + jnp.sum(mj, axis=1, keepdims=True)
        return jnp.concatenate(outs, axis=1), off

    gt = aff > thr_f
    tie = jnp.where(aff == thr_f, 1.0, 0.0)
    n_gt = jnp.sum(jnp.where(gt, 1.0, 0.0), axis=1, keepdims=True)
    tie_rank, _ = exclusive_count(tie)
    sel = jnp.where(gt | ((tie > 0.0) & (tie_rank < CAPACITY - n_gt)), 1.0, 0.0)
    slot, _ = exclusive_count(sel)
    slot_ref[...] = jnp.where(sel > 0.0, slot.astype(I32), -1)
    tok = lax.broadcasted_iota(I32, (SEQ, LANES), 0)
    j = lax.broadcasted_iota(I32, (SEQ, LANES), 1)
    before = jnp.where(tok < j * TOKEN_CHUNK, 1.0, 0.0).astype(BF16)
    offs_ref[...] = jnp.dot(sel.astype(BF16), before, preferred_element_type=F32).astype(I32)


def _expert_slots(aff):
    aff_t = aff[:, :N_EXPERTS].reshape(BATCH, SEQ, N_EXPERTS).transpose(0, 2, 1).reshape(BATCH * N_EXPERTS, SEQ)
    slots, offs = pl.pallas_call(
        _topk_kernel,
        out_shape=[jax.ShapeDtypeStruct((BATCH * N_EXPERTS, SEQ), I32),
                   jax.ShapeDtypeStruct((BATCH * N_EXPERTS, LANES), I32)],
        compiler_params=pltpu.CompilerParams(vmem_limit_bytes=VMEM_LIMIT),
        name="expert_topk",
    )(aff_t)
    return slots, offs[:, :OFFS_STRIDE].reshape(-1)


def _window_start(first, win):
    start = jnp.minimum((first >> SLOT_ALIGN_SHIFT) << SLOT_ALIGN_SHIFT, CAPACITY - win)
    return pl.multiple_of(start, 1 << SLOT_ALIGN_SHIFT)


def _gather_kernel(offs_ref, slot_ref, h_ref, x_ref):
    win = GATHER_WIN
    last = CAPACITY - win
    b = pl.program_id(0)
    e0 = pl.program_id(1) * GATHER_EXPERTS
    j = pl.program_id(2)

    @pl.when(j == 0)
    def _():
        x_ref[...] = jnp.zeros_like(x_ref)

    def bounds(e):
        base = (b * N_EXPERTS + e0 + e) * OFFS_STRIDE + j
        return offs_ref[base], offs_ref[base + 1]

    row = lax.broadcasted_iota(I32, (win, TOKEN_CHUNK), 0)
    h_c = h_ref[...]
    starts = [_window_start(bounds(e)[0], win) for e in range(GATHER_EXPERTS)]
    onehot = jnp.concatenate(
        [jnp.where(row + starts[e] == slot_ref[e:e + 1, :], 1.0, 0.0).astype(BF16)
         for e in range(GATHER_EXPERTS)], axis=0)
    picked = jnp.dot(onehot, h_c, preferred_element_type=F32)
    for e in range(GATHER_EXPERTS):
        rows = pl.ds(e * CAPACITY + starts[e], win)
        x_ref[rows, :] = (x_ref[rows, :].astype(F32) + picked[e * win:(e + 1) * win]).astype(BF16)

    for e in range(GATHER_EXPERTS):
        _, end = bounds(e)
        covered = starts[e] + win
        slot_e = slot_ref[e:e + 1, :]

        def extra_window(i, carry):
            lo = covered + i * win
            r0 = pl.multiple_of(jnp.minimum(lo, last), 1 << SLOT_ALIGN_SHIFT)
            hot = jnp.where((row + r0 == slot_e) & (slot_e >= lo), 1.0, 0.0).astype(BF16)
            rows = pl.ds(e * CAPACITY + r0, win)
            x_ref[rows, :] = (x_ref[rows, :].astype(F32)
                              + jnp.dot(hot, h_c, preferred_element_type=F32)).astype(BF16)
            return carry

        lax.fori_loop(0, jnp.maximum(end - covered + win - 1, 0) // win, extra_window, 0)


def _gather(offs, slots, h_ext):
    slots = slots.reshape(BATCH, N_EXPERTS, SEQ)
    h_ext = h_ext.reshape(BATCH, SEQ, D_MODEL + LANES)
    return pl.pallas_call(
        _gather_kernel,
        grid_spec=pltpu.PrefetchScalarGridSpec(
            num_scalar_prefetch=1,
            grid=(BATCH, N_EXPERTS // GATHER_EXPERTS, SEQ // TOKEN_CHUNK),
            in_specs=[pl.BlockSpec((None, GATHER_EXPERTS, TOKEN_CHUNK), lambda b, g, j, offs: (b, g, j)),
                      pl.BlockSpec((None, TOKEN_CHUNK, D_MODEL + LANES), lambda b, g, j, offs: (b, j, 0))],
            out_specs=pl.BlockSpec((None, GATHER_EXPERTS * CAPACITY, D_MODEL + LANES),
                                   lambda b, g, j, offs: (b, g, 0))),
        out_shape=jax.ShapeDtypeStruct((BATCH, N_EXPERTS * CAPACITY, D_MODEL + LANES), BF16),
        compiler_params=_params("parallel", "parallel", "arbitrary"),
        name="expert_gather",
    )(offs, slots, h_ext)


def _expert_kernel(x_ref, wg_ref, wu_ref, wd_ref, y_ref, wg_s, wu_s, wd_s):
    g = pl.program_id(0)
    part = pl.program_id(1)
    slab = wg_ref.shape[0]

    @pl.when(g < N_EXPERTS)
    def _():
        rows = pl.ds(pl.multiple_of(part * slab, slab), slab)
        wg_s[g % 2, rows, :] = wg_ref[...].astype(BF16)
        wu_s[g % 2, rows, :] = wu_ref[...].astype(BF16)
        wd_s[g % 2, rows, :] = wd_ref[...].astype(BF16)

    @pl.when(g == 0)
    def _():
        y_ref[...] = jnp.zeros_like(y_ref)

    @pl.when(g > 0)
    def _():
        e = g - 1
        cur = e % 2
        n_seq = x_ref.shape[0]
        x = jnp.concatenate([x_ref[s] for s in range(n_seq)], axis=0)
        xin = x[:, :D_MODEL]
        ext = x[:, D_MODEL:].astype(F32)
        lane = lax.broadcasted_iota(I32, ext.shape, 1)
        mine = (lane == e) | (lane == e + N_EXPERTS) | (lane == e + 2 * N_EXPERTS)
        gate = jnp.sum(jnp.where(mine, ext, 0.0), axis=1, keepdims=True)

        a = jnp.dot(xin, wg_s[cur], preferred_element_type=F32)
        b = jnp.dot(xin, wu_s[cur], preferred_element_type=F32)
        hid = (a / (1.0 + jnp.exp(-a)) * b).astype(BF16)
        y = (jnp.dot(hid, wd_s[cur], preferred_element_type=F32) * gate).astype(BF16)
        for s in range(n_seq):
            y_ref[s] = y[s * CAPACITY:(s + 1) * CAPACITY]


def _experts(xin, w_gate, w_up, w_down):
    parts = BATCH // FFN_SEQS
    slab = D_MODEL // parts
    xin = xin.reshape(parts, FFN_SEQS, N_EXPERTS * CAPACITY, D_MODEL + LANES)
    prev = lambda g: jnp.maximum(g - 1, 0)
    wspec = pl.BlockSpec((None, slab, D_MODEL), lambda g, p: (jnp.minimum(g, N_EXPERTS - 1), p, 0))
    y = pl.pallas_call(
        _expert_kernel,
        grid=(N_EXPERTS + 1, parts),
        in_specs=[pl.BlockSpec((None, FFN_SEQS, CAPACITY, D_MODEL + LANES), lambda g, p: (p, 0, prev(g), 0)),
                  wspec, wspec, wspec],
        out_specs=pl.BlockSpec((None, FFN_SEQS, CAPACITY, D_MODEL),
                               lambda g, p: (p, 0, jnp.where(g == 0, N_EXPERTS, g - 1), 0)),
        out_shape=jax.ShapeDtypeStruct((parts, FFN_SEQS, (N_EXPERTS + 1) * CAPACITY, D_MODEL), BF16),
        scratch_shapes=[pltpu.VMEM((2, D_MODEL, D_MODEL), BF16)] * 3,
        compiler_params=_params("arbitrary", "arbitrary"),
        name="expert_ffn",
    )(xin, w_gate, w_up, w_down)
    return y.reshape(BATCH, (N_EXPERTS + 1) * CAPACITY, D_MODEL)


def _combine_kernel(offs_ref, x2_ref, slot_ref, y_ref, g_ref, o_ref):
    b = pl.program_id(0)
    j = pl.program_id(1)
    win = SCATTER_WIN
    lane = lax.broadcasted_iota(I32, (TOKEN_CHUNK, win), 1)
    slot = slot_ref[...]

    def bounds(e):
        base = (b * N_EXPERTS + e) * OFFS_STRIDE + j
        return offs_ref[base], offs_ref[base + 1]

    starts = [_window_start(bounds(e)[0], win) for e in range(N_EXPERTS)]
    onehot = jnp.concatenate(
        [jnp.where(lane + starts[e] == slot[:, e:e + 1], 1.0, 0.0).astype(BF16) for e in range(N_EXPERTS)],
        axis=1)
    rows = jnp.concatenate(
        [y_ref[pl.ds(e * CAPACITY + starts[e], win), :] for e in range(N_EXPERTS)], axis=0)
    o_ref[...] = x2_ref[...] + jnp.dot(onehot, rows, preferred_element_type=F32)

    tail_lane = lax.broadcasted_iota(I32, (TOKEN_CHUNK, CAPACITY - win), 1) + win
    for e in range(N_EXPERTS):
        _, end = bounds(e)
        covered = starts[e] + win

        @pl.when(end > covered)
        def _():
            sl = slot[:, e:e + 1]
            hot = jnp.where((tail_lane == sl) & (sl >= covered), 1.0, 0.0).astype(BF16)
            o_ref[...] += jnp.dot(hot, y_ref[e * CAPACITY + win:(e + 1) * CAPACITY, :],
                                  preferred_element_type=F32)

    o_ref[...] = _rms(o_ref[...], g_ref[...])


def _combine(offs, x2, slots, y, g_final):
    slots_t = slots.reshape(BATCH, N_EXPERTS, SEQ).transpose(0, 2, 1)
    x2 = x2.reshape(BATCH, SEQ, D_MODEL)
    tm = TOKEN_CHUNK
    return pl.pallas_call(
        _combine_kernel,
        grid_spec=pltpu.PrefetchScalarGridSpec(
            num_scalar_prefetch=1,
            grid=(BATCH, SEQ // tm),
            in_specs=[pl.BlockSpec((None, tm, D_MODEL), lambda b, i, offs: (b, i, 0)),
                      pl.BlockSpec((None, tm, N_EXPERTS), lambda b, i, offs: (b, i, 0)),
                      pl.BlockSpec((None, N_EXPERTS * CAPACITY, D_MODEL), lambda b, i, offs: (b, 0, 0)),
                      pl.BlockSpec((1, D_MODEL), lambda b, i, offs: (0, 0))],
            out_specs=pl.BlockSpec((None, tm, D_MODEL), lambda b, i, offs: (b, i, 0))),
        out_shape=jax.ShapeDtypeStruct((BATCH, SEQ, D_MODEL), F32),
        compiler_params=_params("parallel", "arbitrary"),
        name="combine_final_norm",
    )(offs, x2, slots_t, y, g_final.reshape(1, -1))


def kernel(x, mem, positions, g_mix, w_in, g_q_lat, w_q_up, g_kv_lat, w_kv_up, w_fourier, w_out, g_mem_q,
           g_mem_kv, w_mem_q, w_mem_kv, w_mem_o, g_ffn, w_router, w_exp_gate, w_exp_up, w_exp_down, g_final):
    assert x.shape == (BATCH, SEQ, D_MODEL) and g_mix.shape[0] == 1
    x2d = x.reshape(TOKENS, D_MODEL)
    cos, sin = _rope_tables(positions)
    ab = _channel_mats(w_fourier[0])
    v12, q, k, v, ksq = _in_projection(x2d, g_mix[0], w_in[0], g_q_lat[0], w_q_up[0], g_kv_lat[0], w_kv_up[0],
                                       ab, cos, sin)
    y_f = _sequence_dft(v12)
    y_a = _mla_attention(q, k, v, ksq)
    mk, mv = _memory_kv(mem, g_mem_kv[0], w_mem_kv[0])
    x2, h_ext, aff = _mixing(x2d, y_f, y_a, w_out[0], g_mem_q[0], w_mem_q[0], mk, mv, w_mem_o[0], g_ffn[0],
                             w_router[0])
    slots, offs = _expert_slots(aff)
    xin = _gather(offs, slots, h_ext)
    y = _experts(xin, w_exp_gate[0], w_exp_up[0], w_exp_down[0])
    return _combine(offs, x2, slots, y, g_final)
```

```python
import functools

import numpy as np
import jax
import jax.numpy as jnp
from jax import lax
from jax.experimental import pallas as pl
from jax.experimental.pallas import tpu as pltpu

F32 = jnp.float32
BF16 = jnp.bfloat16
I32 = jnp.int32

D_MODEL = 1024
BATCH = 4
SEQ = 4096
TOKENS = BATCH * SEQ
MEM_LEN = 256
RMS_EPS = 1e-6
F_GROUPS = 8
F_GROUP_DIM = 64
F_WIDTH = F_GROUPS * F_GROUP_DIM
MLA_HEADS = 8
QK_NOPE_DIM = 64
QK_ROPE_DIM = 32
V_HEAD_DIM = 64
Q_LORA_RANK = 384
KV_LORA_RANK = 256
ROPE_THETA = 10000.0
MEM_HEADS = 4
MEM_HEAD_DIM = D_MODEL // MEM_HEADS
N_EXPERTS = 16
CAPACITY = 2 * SEQ // N_EXPERTS

NORM_SLACK = 1.01
SAFE_SHIFT = 30.0

LANES = 128
HEAD_PAD = 128
IN_PAD = 1280
HALF = SEQ // 2
QUARTER = SEQ // 4
VMEM_LIMIT = 56 * 1024 * 1024

TM_IN = 1024
IN_GROUPS = 1
TM_MIX = 1024
MIX_GROUPS = 2
TQ = 1024
TOKEN_CHUNK = 512
OFFS_STRIDE = SEQ // TOKEN_CHUNK + 1
GATHER_WIN = 96
SCATTER_WIN = 128
SLOT_ALIGN_SHIFT = 4
GATHER_EXPERTS = 16
FFN_SEQS = 2


def _rms(x, g):
    return x * lax.rsqrt(jnp.mean(x * x, axis=-1, keepdims=True) + RMS_EPS) * g


def _params(*sem):
    return pltpu.CompilerParams(dimension_semantics=sem, vmem_limit_bytes=VMEM_LIMIT)


def _rope_kernel(pos_ref, freq_ref, cos_ref, sin_ref):
    ang = pos_ref[...] * freq_ref[...]
    cos_ref[...] = jnp.cos(ang)
    sin_ref[...] = jnp.sin(ang)


def _rope_tables(positions):
    half = QK_ROPE_DIM // 2
    freqs = 1.0 / (ROPE_THETA ** (jnp.arange(0, QK_ROPE_DIM, 2, dtype=F32) / QK_ROPE_DIM))
    rows = TOKENS * half // LANES
    pos = jnp.repeat(positions.astype(F32).reshape(-1), half).reshape(rows, LANES)
    freq = jnp.tile(freqs, LANES // half).reshape(1, LANES)
    cos, sin = pl.pallas_call(
        _rope_kernel,
        out_shape=(jax.ShapeDtypeStruct((rows, LANES), F32),) * 2,
        name="rope_tables",
    )(pos, freq)
    return cos.reshape(TOKENS, half), sin.reshape(TOKENS, half)


def _rope_expanders():
    half = QK_ROPE_DIM // 2
    lane = np.arange(LANES)
    hit = (lane[None, :] % half) == np.arange(half)[:, None]
    is_sin = (lane // QK_ROPE_DIM) % 2 == 1
    return (jnp.asarray((hit & ~is_sin[None, :]).astype(np.float32)),
            jnp.asarray((hit & is_sin[None, :]).astype(np.float32)))


def _chan_kernel(cbd_ref, sbd_ref, w_ref, ab_ref):
    w = w_ref[...]
    ab_ref[:, :F_WIDTH] = jnp.dot(cbd_ref[...], w, precision=lax.Precision.HIGHEST,
                                  preferred_element_type=F32).astype(BF16)
    ab_ref[:, F_WIDTH:] = jnp.dot(sbd_ref[...], w, precision=lax.Precision.HIGHEST,
                                  preferred_element_type=F32).astype(BF16)


def _channel_mats(w_fourier):
    c = np.arange(F_GROUP_DIM)
    ang = 2.0 * np.pi * np.outer(c, c) / F_GROUP_DIM
    scale = F_GROUP_DIM ** -0.5
    eye = np.eye(F_GROUPS)
    cbd = np.kron(eye, np.cos(ang) * scale).astype(np.float32)
    sbd = np.kron(eye, np.sin(ang) * scale).astype(np.float32)
    wbd = (jnp.eye(F_GROUPS, dtype=F32)[:, None, :, None] * w_fourier[:, :, None, :]).reshape(F_WIDTH, F_WIDTH)
    return pl.pallas_call(
        _chan_kernel,
        out_shape=jax.ShapeDtypeStruct((F_WIDTH, 2 * F_WIDTH), BF16),
        name="channel_dft_fold",
    )(jnp.asarray(cbd), jnp.asarray(sbd), wbd)


def _inproj_kernel(x_ref, gmix_ref, win_ref, gq_ref, wq_ref, gkv_ref, wkv_ref, ab_ref, cos_ref, sin_ref,
                   ec_ref, es_ref, hsum_ref, v12_ref, q_ref, k_ref, v_ref, ksq_ref, vscr_ref):
    rows_per_group = x_ref.shape[0] // IN_GROUPS
    half_rows = rows_per_group // 2
    ksq_max = None
    for grp in range(IN_GROUPS):
        r0 = grp * rows_per_group
        rows = pl.ds(r0, rows_per_group)
        half = pl.ds(r0 // 2, half_rows)
        h = _rms(x_ref[rows, :], gmix_ref[...]).astype(BF16)
        u = jnp.dot(h, win_ref[...], preferred_element_type=F32)

        n_pairs = ab_ref.shape[0]
        for gp in range(n_pairs):
            pair = jnp.dot(u[:, LANES * gp:LANES * (gp + 1)].astype(BF16), ab_ref[gp],
                           preferred_element_type=F32)
            for part in range(2):
                c = part * n_pairs + gp
                cols = slice(LANES * c, LANES * (c + 1))
                vscr_ref[c, rows, :] = pair[:, LANES * part:LANES * (part + 1)]
                v12_ref[0, half, cols] = vscr_ref[c, pl.ds(r0, half_rows, stride=2), :].astype(BF16)
                v12_ref[1, half, cols] = vscr_ref[c, pl.ds(r0 + 1, half_rows, stride=2), :].astype(BF16)

        tab = (jnp.dot(cos_ref[rows, :], ec_ref[...], precision=lax.Precision.HIGHEST,
                       preferred_element_type=F32)
               + jnp.dot(sin_ref[rows, :], es_ref[...], precision=lax.Precision.HIGHEST,
                         preferred_element_type=F32))
        lane = lax.broadcasted_iota(I32, tab.shape, 1)
        rope0, rope1 = QK_NOPE_DIM, QK_NOPE_DIM + QK_ROPE_DIM
        back = LANES - QK_ROPE_DIM
        c1 = jnp.where(lane < rope0, 1.0, jnp.where(lane < rope1, tab, 0.0))
        c2 = jnp.where(lane >= rope1, tab, 0.0)

        q0 = F_WIDTH
        qn = _rms(u[:, q0:q0 + Q_LORA_RANK], gq_ref[...]).astype(BF16)
        qa = jnp.dot(qn, wq_ref[...], preferred_element_type=F32)
        q_blocks = []
        for hd in range(MLA_HEADS):
            blk = qa[:, HEAD_PAD * hd:HEAD_PAD * (hd + 1)]
            q_blocks.append(blk * c1 + pltpu.roll(blk * c2, back, 1))
            q_ref[rows, HEAD_PAD * hd:HEAD_PAD * (hd + 1)] = q_blocks[hd].astype(BF16)

        kv0 = q0 + Q_LORA_RANK
        kvn = _rms(u[:, kv0:kv0 + KV_LORA_RANK], gkv_ref[...]).astype(BF16)
        kv = jnp.dot(kvn, wkv_ref[...], preferred_element_type=F32)
        kr0 = kv0 + KV_LORA_RANK
        t = u[:, kr0:kr0 + LANES] * jnp.where(lane < 2 * QK_ROPE_DIM, tab, 0.0)
        kr = t + pltpu.roll(t, back, 1)
        kr = jnp.where((lane >= rope0) & (lane < rope1), pltpu.roll(kr, rope0, 1), 0.0)
        k_blocks = [kv[:, HEAD_PAD * hd:HEAD_PAD * (hd + 1)] + kr for hd in range(MLA_HEADS)]
        for hd in range(MLA_HEADS):
            k_ref[rows, HEAD_PAD * hd:HEAD_PAD * (hd + 1)] = k_blocks[hd].astype(BF16)
        v_ref[rows, :] = kv[:, MLA_HEADS * HEAD_PAD:].astype(BF16)
        sq = jnp.concatenate(
            [jnp.concatenate([(blk * blk).astype(BF16) for blk in blocks], axis=1)
             for blocks in (k_blocks, q_blocks)], axis=0)
        sq = jnp.dot(sq, hsum_ref[...], preferred_element_type=F32)
        grp_max = jnp.concatenate([jnp.max(sq[:rows_per_group], axis=0, keepdims=True),
                                   jnp.max(sq[rows_per_group:], axis=0, keepdims=True)], axis=0)
        ksq_max = grp_max if ksq_max is None else jnp.maximum(ksq_max, grp_max)
    ksq_ref[...] = ksq_max


def _in_projection(x2d, g_mix, w_in, g_q, w_q_up, g_kv, w_kv_up, ab, cos, sin):
    w_kr = w_in[:, -QK_ROPE_DIM:]
    half = QK_ROPE_DIM // 2
    w_kr_rot = jnp.concatenate([-w_kr[:, half:], w_kr[:, :half]], axis=1)
    win = jnp.concatenate(
        [w_in, w_kr_rot, jnp.zeros((D_MODEL, IN_PAD - w_in.shape[1] - QK_ROPE_DIM), F32)], axis=1).astype(BF16)
    scale = (QK_NOPE_DIM + QK_ROPE_DIM) ** -0.5
    wq = w_q_up.reshape(Q_LORA_RANK, MLA_HEADS, QK_NOPE_DIM + QK_ROPE_DIM) * scale
    wq_rope = wq[:, :, QK_NOPE_DIM:]
    wq_rot = jnp.concatenate([-wq_rope[:, :, half:], wq_rope[:, :, :half]], axis=2)
    wq = jnp.concatenate([wq, wq_rot], axis=2).reshape(Q_LORA_RANK, MLA_HEADS * HEAD_PAD).astype(BF16)
    wkv = w_kv_up.reshape(KV_LORA_RANK, MLA_HEADS, QK_NOPE_DIM + V_HEAD_DIM)
    wk = jnp.concatenate([wkv[:, :, :QK_NOPE_DIM],
                          jnp.zeros((KV_LORA_RANK, MLA_HEADS, HEAD_PAD - QK_NOPE_DIM), F32)], axis=2)
    wkv = jnp.concatenate([wk.reshape(KV_LORA_RANK, MLA_HEADS * HEAD_PAD),
                           wkv[:, :, QK_NOPE_DIM:].reshape(KV_LORA_RANK, MLA_HEADS * V_HEAD_DIM)],
                          axis=1).astype(BF16)

    ab = jnp.stack([jnp.concatenate([ab[LANES * g:LANES * (g + 1), LANES * g:LANES * (g + 1)],
                                     ab[LANES * g:LANES * (g + 1), F_WIDTH + LANES * g:F_WIDTH + LANES * (g + 1)]],
                                    axis=1) for g in range(F_WIDTH // LANES)])
    tm = TM_IN
    ec, es = _rope_expanders()
    hsum = jnp.asarray((np.arange(MLA_HEADS * HEAD_PAD)[:, None] // HEAD_PAD
                        == np.arange(LANES)[None, :]).astype(np.float32)).astype(BF16)
    full = lambda shape: pl.BlockSpec(shape, lambda i: (0,) * len(shape))
    tile = lambda w: pl.BlockSpec((tm, w), lambda i: (i, 0))
    per_half = HALF // tm
    per_seq = SEQ // tm
    v12_spec = pl.BlockSpec((None, 2, None, tm // 2, 2 * F_WIDTH),
                            lambda i: (i // per_seq, 0, (i % per_seq) // per_half, i % per_half, 0))
    return pl.pallas_call(
        _inproj_kernel,
        grid=(TOKENS // tm,),
        in_specs=[tile(D_MODEL), full((1, D_MODEL)), full(win.shape), full((1, Q_LORA_RANK)), full(wq.shape),
                  full((1, KV_LORA_RANK)), full(wkv.shape), full(ab.shape), tile(QK_ROPE_DIM // 2),
                  tile(QK_ROPE_DIM // 2), full(ec.shape), full(es.shape), full(hsum.shape)],
        out_specs=[v12_spec, tile(MLA_HEADS * HEAD_PAD), tile(MLA_HEADS * HEAD_PAD),
                   tile(MLA_HEADS * V_HEAD_DIM), pl.BlockSpec((None, 2, LANES), lambda i: (i, 0, 0))],
        out_shape=[jax.ShapeDtypeStruct((BATCH, 2, 2, QUARTER, 2 * F_WIDTH), BF16),
                   jax.ShapeDtypeStruct((TOKENS, MLA_HEADS * HEAD_PAD), BF16),
                   jax.ShapeDtypeStruct((TOKENS, MLA_HEADS * HEAD_PAD), BF16),
                   jax.ShapeDtypeStruct((TOKENS, MLA_HEADS * V_HEAD_DIM), BF16),
                   jax.ShapeDtypeStruct((TOKENS // tm, 2, LANES), F32)],
        scratch_shapes=[pltpu.VMEM((2 * F_WIDTH // LANES, tm, LANES), F32)],
        compiler_params=_params("parallel"),
        name="in_projection",
    )(x2d, g_mix.reshape(1, -1), win, g_q.reshape(1, -1), wq, g_kv.reshape(1, -1), wkv, ab, cos, sin, ec, es, hsum)


def _seq_dft_kernel(v_ref, m_ref, y_ref):
    sign = jnp.where(pl.program_id(1) == 0, 1.0, -1.0)
    parts = []
    for q in range(2):
        ab = (v_ref[q, 0].astype(F32) + sign * v_ref[q, 1].astype(F32)).astype(BF16)
        parts.append(jnp.dot(m_ref[q, :, :QUARTER], ab[:, :F_WIDTH], preferred_element_type=F32)
                     + jnp.dot(m_ref[q, :, QUARTER:], ab[:, F_WIDTH:], preferred_element_type=F32))
    y_ref[0] = (parts[0] + parts[1]).astype(BF16)
    y_ref[1] = (parts[0] - parts[1]).astype(BF16)


def _seq_dft_mats():
    i = np.arange(QUARTER)
    out = np.zeros((2, 2, QUARTER, 2 * QUARTER), np.float32)
    for p in range(2):
        for q in range(2):
            prod = np.outer(2 * i + p, 2 * i + q) % SEQ
            ang = 2.0 * np.pi * prod / SEQ
            out[p, q, :, :QUARTER] = np.cos(ang) / np.sqrt(SEQ)
            out[p, q, :, QUARTER:] = -np.sin(ang) / np.sqrt(SEQ)
    return out


def _sequence_dft(v):
    mats = jnp.asarray(_seq_dft_mats()).astype(BF16)
    return pl.pallas_call(
        _seq_dft_kernel,
        grid=(BATCH, 2),
        in_specs=[pl.BlockSpec((None, 2, 2, QUARTER, 2 * F_WIDTH), lambda b, p: (b, 0, 0, 0, 0)),
                  pl.BlockSpec((None, 2, QUARTER, 2 * QUARTER), lambda b, p: (p, 0, 0, 0))],
        out_specs=pl.BlockSpec((None, None, 2, QUARTER, F_WIDTH), lambda b, p: (b, p, 0, 0, 0)),
        out_shape=jax.ShapeDtypeStruct((BATCH, 2, 2, QUARTER, F_WIDTH), BF16),
        compiler_params=_params("parallel", "arbitrary"),
        name="sequence_dft",
    )(v, mats)


def _mla_kernel(q_ref, k_ref, v_ref, sq_ref, o_ref):
    v = v_ref[...]
    hp = pl.program_id(1)
    sq = jnp.max(sq_ref[...], axis=0)
    bounds = jnp.sqrt(sq[0:1] * sq[1:2]) * NORM_SLACK
    head_lane = lax.broadcasted_iota(I32, bounds.shape, 1)
    for j in range(2):
        cols = slice(HEAD_PAD * j, HEAD_PAD * (j + 1))
        out_cols = slice(V_HEAD_DIM * j, V_HEAD_DIM * (j + 1))
        bound = jnp.max(jnp.where(head_lane == 2 * hp + j, bounds, 0.0), axis=1, keepdims=True)
        safe = jnp.max(bound) <= SAFE_SHIFT

        def attend(row_shift):
            s = lax.dot_general(q_ref[:, cols], k_ref[:, cols], (((1,), (1,)), ((), ())),
                                preferred_element_type=F32)
            p = jnp.exp(s - row_shift(s))
            l = jnp.sum(p, axis=1, keepdims=True)
            o = jnp.dot(p.astype(BF16), v, preferred_element_type=F32) / l
            o_ref[:, out_cols] = o[:, out_cols].astype(BF16)

        pl.when(safe)(lambda: attend(lambda s: bound))
        pl.when(jnp.logical_not(safe))(lambda: attend(lambda s: jnp.max(s, axis=1, keepdims=True)))


def _mla_attention(q, k, v, ksq):
    q = q.reshape(BATCH, SEQ, -1)
    k = k.reshape(BATCH, SEQ, -1)
    v = v.reshape(BATCH, SEQ, -1)
    ksq = ksq.reshape(BATCH, SEQ // TM_IN, 2, LANES)
    out = pl.pallas_call(
        _mla_kernel,
        grid=(BATCH, MLA_HEADS // 2, SEQ // TQ),
        in_specs=[pl.BlockSpec((None, TQ, 2 * HEAD_PAD), lambda b, hp, i: (b, i, hp)),
                  pl.BlockSpec((None, SEQ, 2 * HEAD_PAD), lambda b, hp, i: (b, 0, hp)),
                  pl.BlockSpec((None, SEQ, 2 * V_HEAD_DIM), lambda b, hp, i: (b, 0, hp)),
                  pl.BlockSpec((None, SEQ // TM_IN, 2, LANES), lambda b, hp, i: (b, 0, 0, 0))],
        out_specs=pl.BlockSpec((None, TQ, 2 * V_HEAD_DIM), lambda b, hp, i: (b, i, hp)),
        out_shape=jax.ShapeDtypeStruct((BATCH, SEQ, MLA_HEADS * V_HEAD_DIM), BF16),
        compiler_params=_params("parallel", "parallel", "arbitrary"),
        name="mla_attention",
    )(q, k, v, ksq)
    return out.reshape(TOKENS, MLA_HEADS * V_HEAD_DIM)


def _memkv_kernel(mem_ref, g_ref, w_ref, k_ref, v_ref):
    mn = _rms(mem_ref[...], g_ref[...]).astype(BF16)
    kv = jnp.dot(mn, w_ref[...], preferred_element_type=F32)
    k_ref[...] = kv[:, :D_MODEL].astype(BF16)
    v_ref[...] = kv[:, D_MODEL:].astype(BF16)


def _memory_kv(mem, g_mem_kv, w_mem_kv):
    blk = pl.BlockSpec((None, MEM_LEN, D_MODEL), lambda b: (b, 0, 0))
    return pl.pallas_call(
        _memkv_kernel,
        grid=(BATCH,),
        in_specs=[blk, pl.BlockSpec((1, D_MODEL), lambda b: (0, 0)),
                  pl.BlockSpec((D_MODEL, 2 * D_MODEL), lambda b: (0, 0))],
        out_specs=[blk, blk],
        out_shape=[jax.ShapeDtypeStruct((BATCH, MEM_LEN, D_MODEL), BF16)] * 2,
        compiler_params=_params("parallel"),
        name="memory_kv",
    )(mem, g_mem_kv.reshape(1, -1), w_mem_kv.astype(BF16))


def _mix_kernel(x_ref, yf_ref, ya_ref, wo_ref, gq_ref, wmq_ref, mk_ref, mv_ref, wmo_ref, gf_ref, wr_ref,
                x2_ref, hext_ref, aff_ref, zscr_ref):
    tm = x_ref.shape[0]
    rows_per_group = tm // MIX_GROUPS
    for grp in range(MIX_GROUPS):
        rows = pl.ds(grp * rows_per_group, rows_per_group)
        half = pl.ds(grp * rows_per_group // 2, rows_per_group // 2)
        wo_f = wo_ref[:F_WIDTH, :]
        z_even = jnp.dot(yf_ref[0, half, :], wo_f, preferred_element_type=F32)
        z_odd = jnp.dot(yf_ref[1, half, :], wo_f, preferred_element_type=F32)
        for c in range(zscr_ref.shape[0]):
            cols = slice(LANES * c, LANES * (c + 1))
            zscr_ref[c, pl.ds(grp * rows_per_group, rows_per_group // 2, stride=2), :] = z_even[:, cols]
            zscr_ref[c, pl.ds(grp * rows_per_group + 1, rows_per_group // 2, stride=2), :] = z_odd[:, cols]
        z = jnp.concatenate([zscr_ref[c, rows, :] for c in range(zscr_ref.shape[0])], axis=1)
        x1 = x_ref[rows, :] + z + jnp.dot(ya_ref[rows, :], wo_ref[F_WIDTH:, :], preferred_element_type=F32)

        hq = _rms(x1, gq_ref[...]).astype(BF16)
        qm = (jnp.dot(hq, wmq_ref[...], preferred_element_type=F32) * (MEM_HEAD_DIM ** -0.5)).astype(BF16)
        heads = []
        for hd in range(MEM_HEADS):
            sl = slice(MEM_HEAD_DIM * hd, MEM_HEAD_DIM * (hd + 1))
            s = lax.dot_general(qm[:, sl], mk_ref[:, sl], (((1,), (1,)), ((), ())),
                                preferred_element_type=F32)
            p = jnp.exp(s - jnp.max(s, axis=1, keepdims=True))
            l = jnp.sum(p, axis=1, keepdims=True)
            heads.append((jnp.dot(p.astype(BF16), mv_ref[:, sl], preferred_element_type=F32) / l).astype(BF16))
        o = jnp.concatenate(heads, axis=1)
        x2 = x1 + jnp.dot(o, wmo_ref[...], preferred_element_type=F32)
        x2_ref[rows, :] = x2

        h3 = _rms(x2, gf_ref[...])
        h3_hi = h3.astype(BF16)
        hext_ref[rows, :D_MODEL] = h3_hi
        h3_lo = (h3 - h3_hi.astype(F32)).astype(BF16)
        hi_terms = jnp.dot(h3_hi, wr_ref[...], preferred_element_type=F32)
        logits = (hi_terms[:, :LANES] + hi_terms[:, LANES:]
                  + jnp.dot(h3_lo, wr_ref[:, :LANES], preferred_element_type=F32))
        lane = lax.broadcasted_iota(I32, logits.shape, 1)
        logits = jnp.where(lane < N_EXPERTS, logits, -jnp.inf)
        e = jnp.exp(logits - jnp.max(logits, axis=1, keepdims=True))
        aff = e / jnp.sum(e, axis=1, keepdims=True)
        aff_ref[rows, :] = aff
        hi = aff.astype(BF16)
        r1 = aff - hi.astype(F32)
        mid = r1.astype(BF16)
        lo = (r1 - mid.astype(F32)).astype(BF16)
        hext_ref[rows, D_MODEL:] = jnp.where(
            lane < N_EXPERTS, hi,
            jnp.where(lane < 2 * N_EXPERTS, pltpu.roll(mid.astype(F32), N_EXPERTS, 1).astype(BF16),
                      pltpu.roll(lo.astype(F32), 2 * N_EXPERTS, 1).astype(BF16)))


def _mixing(x2d, y_f, y_a, w_out, g_mem_q, w_mem_q, mk, mv, w_mem_o, g_ffn, w_router):
    tm = TM_MIX
    wr = jnp.concatenate([w_router, jnp.zeros((D_MODEL, LANES - N_EXPERTS), F32)], axis=1)
    wr_hi = wr.astype(BF16)
    wr_cat = jnp.concatenate([wr_hi, (wr - wr_hi.astype(F32)).astype(BF16)], axis=1)
    full = lambda shape: pl.BlockSpec(shape, lambda i: (0,) * len(shape))
    tile = lambda w: pl.BlockSpec((tm, w), lambda i: (i, 0))
    per_half = HALF // tm
    per_seq = SEQ // tm
    per_batch = pl.BlockSpec((None, MEM_LEN, D_MODEL), lambda i: (i // per_seq, 0, 0))
    yf_spec = pl.BlockSpec((None, 2, None, tm // 2, F_WIDTH),
                           lambda i: (i // per_seq, 0, (i % per_seq) // per_half, i % per_half, 0))
    return pl.pallas_call(
        _mix_kernel,
        grid=(TOKENS // tm,),
        in_specs=[tile(D_MODEL), yf_spec, tile(F_WIDTH), full((D_MODEL, D_MODEL)), full((1, D_MODEL)),
                  full((D_MODEL, D_MODEL)), per_batch, per_batch, full((D_MODEL, D_MODEL)), full((1, D_MODEL)),
                  full((D_MODEL, 2 * LANES))],
        out_specs=[tile(D_MODEL), tile(D_MODEL + LANES), tile(LANES)],
        out_shape=[jax.ShapeDtypeStruct((TOKENS, D_MODEL), F32),
                   jax.ShapeDtypeStruct((TOKENS, D_MODEL + LANES), BF16),
                   jax.ShapeDtypeStruct((TOKENS, LANES), F32)],
        scratch_shapes=[pltpu.VMEM((D_MODEL // LANES, tm, LANES), F32)],
        compiler_params=_params("parallel"),
        name="mix_memattn_router",
    )(x2d, y_f, y_a, w_out.astype(BF16), g_mem_q.reshape(1, -1), w_mem_q.astype(BF16), mk, mv,
      w_mem_o.astype(BF16), g_ffn.reshape(1, -1), wr_cat)


def _topk_kernel(aff_ref, slot_ref, offs_ref):
    aff = aff_ref[...]
    rows = aff.shape[0]

    thr = jnp.zeros((rows, 1), I32)
    for bit in range(30, -1, -1):
        cand = thr | (1 << bit)
        cnt = jnp.sum(jnp.where(aff >= pltpu.bitcast(cand, F32), 1.0, 0.0), axis=1, keepdims=True)
        thr = jnp.where(cnt >= CAPACITY, cand, thr)
    thr_f = pltpu.bitcast(thr, F32)

    chunk = 256
    r = lax.broadcasted_iota(I32, (chunk, chunk), 0)
    c = lax.broadcasted_iota(I32, (chunk, chunk), 1)
    tri = jnp.where(r < c, 1.0, 0.0).astype(BF16)

    def exclusive_count(mask):
        off = jnp.zeros((rows, 1), F32)
        outs = []
        for j in range(SEQ // chunk):
            mj = mask[:, chunk * j:chunk * (j + 1)]
            outs.append(jnp.dot(mj.astype(BF16), tri, preferred_element_type=F32) + off)
            off = off + jnp.sum(mj, axis=1, keepdims=True)
        return jnp.concatenate(outs, axis=1), off

    gt = aff > thr_f
    tie = jnp.where(aff == thr_f, 1.0, 0.0)
    n_gt = jnp.sum(jnp.where(gt, 1.0, 0.0), axis=1, keepdims=True)
    tie_rank, _ = exclusive_count(tie)
    sel = jnp.where(gt | ((tie > 0.0) & (tie_rank < CAPACITY - n_gt)), 1.0, 0.0)
    slot, _ = exclusive_count(sel)
    slot_ref[...] = jnp.where(sel > 0.0, slot.astype(I32), -1)
    tok = lax.broadcasted_iota(I32, (SEQ, LANES), 0)
    j = lax.broadcasted_iota(I32, (SEQ, LANES), 1)
    before = jnp.where(tok < j * TOKEN_CHUNK, 1.0, 0.0).astype(BF16)
    offs_ref[...] = jnp.dot(sel.astype(BF16), before, preferred_element_type=F32).astype(I32)


def _expert_slots(aff):
    aff_t = aff[:, :N_EXPERTS].reshape(BATCH, SEQ, N_EXPERTS).transpose(0, 2, 1).reshape(BATCH * N_EXPERTS, SEQ)
    slots, offs = pl.pallas_call(
        _topk_kernel,
        out_shape=[jax.ShapeDtypeStruct((BATCH * N_EXPERTS, SEQ), I32),
                   jax.ShapeDtypeStruct((BATCH * N_EXPERTS, LANES), I32)],
        compiler_params=pltpu.CompilerParams(vmem_limit_bytes=VMEM_LIMIT),
        name="expert_topk",
    )(aff_t)
    return slots, offs[:, :OFFS_STRIDE].reshape(-1)


def _window_start(first, win):
    start = jnp.minimum((first >> SLOT_ALIGN_SHIFT) << SLOT_ALIGN_SHIFT, CAPACITY - win)
    return pl.multiple_of(start, 1 << SLOT_ALIGN_SHIFT)


def _gather_kernel(offs_ref, slot_ref, h_ref, x_ref):
    win = GATHER_WIN
    last = CAPACITY - win
    b = pl.program_id(0)
    e0 = pl.program_id(1) * GATHER_EXPERTS
    j = pl.program_id(2)

    @pl.when(j == 0)
    def _():
        x_ref[...] = jnp.zeros_like(x_ref)

    def bounds(e):
        base = (b * N_EXPERTS + e0 + e) * OFFS_STRIDE + j
        return offs_ref[base], offs_ref[base + 1]

    row = lax.broadcasted_iota(I32, (win, TOKEN_CHUNK), 0)
    h_c = h_ref[...]
    starts = [_window_start(bounds(e)[0], win) for e in range(GATHER_EXPERTS)]
    onehot = jnp.concatenate(
        [jnp.where(row + starts[e] == slot_ref[e:e + 1, :], 1.0, 0.0).astype(BF16)
         for e in range(GATHER_EXPERTS)], axis=0)
    picked = jnp.dot(onehot, h_c, preferred_element_type=F32)
    for e in range(GATHER_EXPERTS):
        rows = pl.ds(e * CAPACITY + starts[e], win)
        x_ref[rows, :] = (x_ref[rows, :].astype(F32) + picked[e * win:(e + 1) * win]).astype(BF16)

    for e in range(GATHER_EXPERTS):
        _, end = bounds(e)
        covered = starts[e] + win
        slot_e = slot_ref[e:e + 1, :]

        def extra_window(i, carry):
            lo = covered + i * win
            r0 = pl.multiple_of(jnp.minimum(lo, last), 1 << SLOT_ALIGN_SHIFT)
            hot = jnp.where((row + r0 == slot_e) & (slot_e >= lo), 1.0, 0.0).astype(BF16)
            rows = pl.ds(e * CAPACITY + r0, win)
            x_ref[rows, :] = (x_ref[rows, :].astype(F32)
                              + jnp.dot(hot, h_c, preferred_element_type=F32)).astype(BF16)
            return carry

        lax.fori_loop(0, jnp.maximum(end - covered + win - 1, 0) // win, extra_window, 0)


def _gather(offs, slots, h_ext):
    slots = slots.reshape(BATCH, N_EXPERTS, SEQ)
    h_ext = h_ext.reshape(BATCH, SEQ, D_MODEL + LANES)
    return pl.pallas_call(
        _gather_kernel,
        grid_spec=pltpu.PrefetchScalarGridSpec(
            num_scalar_prefetch=1,
            grid=(BATCH, N_EXPERTS // GATHER_EXPERTS, SEQ // TOKEN_CHUNK),
            in_specs=[pl.BlockSpec((None, GATHER_EXPERTS, TOKEN_CHUNK), lambda b, g, j, offs: (b, g, j)),
                      pl.BlockSpec((None, TOKEN_CHUNK, D_MODEL + LANES), lambda b, g, j, offs: (b, j, 0))],
            out_specs=pl.BlockSpec((None, GATHER_EXPERTS * CAPACITY, D_MODEL + LANES),
                                   lambda b, g, j, offs: (b, g, 0))),
        out_shape=jax.ShapeDtypeStruct((BATCH, N_EXPERTS * CAPACITY, D_MODEL + LANES), BF16),
        compiler_params=_params("parallel", "parallel", "arbitrary"),
        name="expert_gather",
    )(offs, slots, h_ext)


def _expert_kernel(x_ref, wg_ref, wu_ref, wd_ref, y_ref, wg_s, wu_s, wd_s):
    g = pl.program_id(0)
    part = pl.program_id(1)
    slab = wg_ref.shape[0]

    @pl.when(g < N_EXPERTS)
    def _():
        rows = pl.ds(pl.multiple_of(part * slab, slab), slab)
        wg_s[g % 2, rows, :] = wg_ref[...].astype(BF16)
        wu_s[g % 2, rows, :] = wu_ref[...].astype(BF16)
        wd_s[g % 2, rows, :] = wd_ref[...].astype(BF16)

    @pl.when(g == 0)
    def _():
        y_ref[...] = jnp.zeros_like(y_ref)

    @pl.when(g > 0)
    def _():
        e = g - 1
        cur = e % 2
        n_seq = x_ref.shape[0]
        x = jnp.concatenate([x_ref[s] for s in range(n_seq)], axis=0)
        xin = x[:, :D_MODEL]
        ext = x[:, D_MODEL:].astype(F32)
        lane = lax.broadcasted_iota(I32, ext.shape, 1)
        mine = (lane == e) | (lane == e + N_EXPERTS) | (lane == e + 2 * N_EXPERTS)
        gate = jnp.sum(jnp.where(mine, ext, 0.0), axis=1, keepdims=True)

        a = jnp.dot(xin, wg_s[cur], preferred_element_type=F32)
        b = jnp.dot(xin, wu_s[cur], preferred_element_type=F32)
        hid = (a / (1.0 + jnp.exp(-a)) * b).astype(BF16)
        y = (jnp.dot(hid, wd_s[cur], preferred_element_type=F32) * gate).astype(BF16)
        for s in range(n_seq):
            y_ref[s] = y[s * CAPACITY:(s + 1) * CAPACITY]


def _experts(xin, w_gate, w_up, w_down):
    parts = BATCH // FFN_SEQS
    slab = D_MODEL // parts
    xin = xin.reshape(parts, FFN_SEQS, N_EXPERTS * CAPACITY, D_MODEL + LANES)
    prev = lambda g: jnp.maximum(g - 1, 0)
    wspec = pl.BlockSpec((None, slab, D_MODEL), lambda g, p: (jnp.minimum(g, N_EXPERTS - 1), p, 0))
    y = pl.pallas_call(
        _expert_kernel,
        grid=(N_EXPERTS + 1, parts),
        in_specs=[pl.BlockSpec((None, FFN_SEQS, CAPACITY, D_MODEL + LANES), lambda g, p: (p, 0, prev(g), 0)),
                  wspec, wspec, wspec],
        out_specs=pl.BlockSpec((None, FFN_SEQS, CAPACITY, D_MODEL),
                               lambda g, p: (p, 0, jnp.where(g == 0, N_EXPERTS, g - 1), 0)),
        out_shape=jax.ShapeDtypeStruct((parts, FFN_SEQS, (N_EXPERTS + 1) * CAPACITY, D_MODEL), BF16),
        scratch_shapes=[pltpu.VMEM((2, D_MODEL, D_MODEL), BF16)] * 3,
        compiler_params=_params("arbitrary", "arbitrary"),
        name="expert_ffn",
    )(xin, w_gate, w_up, w_down)
    return y.reshape(BATCH, (N_EXPERTS + 1) * CAPACITY, D_MODEL)


def _combine_kernel(offs_ref, x2_ref, slot_ref, y_ref, g_ref, o_ref):
    b = pl.program_id(0)
    j = pl.program_id(1)
    win = SCATTER_WIN
    lane = lax.broadcasted_iota(I32, (TOKEN_CHUNK, win), 1)
    slot = slot_ref[...]

    def bounds(e):
        base = (b * N_EXPERTS + e) * OFFS_STRIDE + j
        return offs_ref[base], offs_ref[base + 1]

    starts = [_window_start(bounds(e)[0], win) for e in range(N_EXPERTS)]
    onehot = jnp.concatenate(
        [jnp.where(lane + starts[e] == slot[:, e:e + 1], 1.0, 0.0).astype(BF16) for e in range(N_EXPERTS)],
        axis=1)
    rows = jnp.concatenate(
        [y_ref[pl.ds(e * CAPACITY + starts[e], win), :] for e in range(N_EXPERTS)], axis=0)
    o_ref[...] = x2_ref[...] + jnp.dot(onehot, rows, preferred_element_type=F32)

    tail_lane = lax.broadcasted_iota(I32, (TOKEN_CHUNK, CAPACITY - win), 1) + win
    for e in range(N_EXPERTS):
        _, end = bounds(e)
        covered = starts[e] + win

        @pl.when(end > covered)
        def _():
            sl = slot[:, e:e + 1]
            hot = jnp.where((tail_lane == sl) & (sl >= covered), 1.0, 0.0).astype(BF16)
            o_ref[...] += jnp.dot(hot, y_ref[e * CAPACITY + win:(e + 1) * CAPACITY, :],
                                  preferred_element_type=F32)

    o_ref[...] = _rms(o_ref[...], g_ref[...])


def _combine(offs, x2, slots, y, g_final):
    slots_t = slots.reshape(BATCH, N_EXPERTS, SEQ).transpose(0, 2, 1)
    x2 = x2.reshape(BATCH, SEQ, D_MODEL)
    tm = TOKEN_CHUNK
    return pl.pallas_call(
        _combine_kernel,
        grid_spec=pltpu.PrefetchScalarGridSpec(
            num_scalar_prefetch=1,
            grid=(BATCH, SEQ // tm),
            in_specs=[pl.BlockSpec((None, tm, D_MODEL), lambda b, i, offs: (b, i, 0)),
                      pl.BlockSpec((None, tm, N_EXPERTS), lambda b, i, offs: (b, i, 0)),
                      pl.BlockSpec((None, N_EXPERTS * CAPACITY, D_MODEL), lambda b, i, offs: (b, 0, 0)),
                      pl.BlockSpec((1, D_MODEL), lambda b, i, offs: (0, 0))],
            out_specs=pl.BlockSpec((None, tm, D_MODEL), lambda b, i, offs: (b, i, 0))),
        out_shape=jax.ShapeDtypeStruct((BATCH, SEQ, D_MODEL), F32),
        compiler_params=_params("parallel", "arbitrary"),
        name="combine_final_norm",
    )(offs, x2, slots_t, y, g_final.reshape(1, -1))


def kernel(x, mem, positions, g_mix, w_in, g_q_lat, w_q_up, g_kv_lat, w_kv_up, w_fourier, w_out, g_mem_q,
           g_mem_kv, w_mem_q, w_mem_kv, w_mem_o, g_ffn, w_router, w_exp_gate, w_exp_up, w_exp_down, g_final):
    assert x.shape == (BATCH, SEQ, D_MODEL) and g_mix.shape[0] == 1
    x2d = x.reshape(TOKENS, D_MODEL)
    cos, sin = _rope_tables(positions)
    ab = _channel_mats(w_fourier[0])
    v12, q, k, v, ksq = _in_projection(x2d, g_mix[0], w_in[0], g_q_lat[0], w_q_up[0], g_kv_lat[0], w_kv_up[0],
                                       ab, cos, sin)
    y_f = _sequence_dft(v12)
    y_a = _mla_attention(q, k, v, ksq)
    mk, mv = _memory_kv(mem, g_mem_kv[0], w_mem_kv[0])
    x2, h_ext, aff = _mixing(x2d, y_f, y_a, w_out[0], g_mem_q[0], w_mem_q[0], mk, mv, w_mem_o[0], g_ffn[0],
                             w_router[0])
    slots, offs = _expert_slots(aff)
    xin = _gather(offs, slots, h_ext)
    y = _experts(xin, w_exp_gate[0], w_exp_up[0], w_exp_down[0])
    return _combine(offs, x2, slots, y, g_final)
```

```python
import functools

import numpy as np
import jax
import jax.numpy as jnp
from jax import lax
from jax.experimental import pallas as pl
from jax.experimental.pallas import tpu as pltpu

F32 = jnp.float32
BF16 = jnp.bfloat16
I32 = jnp.int32

D_MODEL = 1024
BATCH = 4
SEQ = 4096
TOKENS = BATCH * SEQ
MEM_LEN = 256
RMS_EPS = 1e-6
F_GROUPS = 8
F_GROUP_DIM = 64
F_WIDTH = F_GROUPS * F_GROUP_DIM
MLA_HEADS = 8
QK_NOPE_DIM = 64
QK_ROPE_DIM = 32
V_HEAD_DIM = 64
Q_LORA_RANK = 384
KV_LORA_RANK = 256
ROPE_THETA = 10000.0
MEM_HEADS = 4
MEM_HEAD_DIM = D_MODEL // MEM_HEADS
N_EXPERTS = 16
CAPACITY = 2 * SEQ // N_EXPERTS

NORM_SLACK = 1.01
SAFE_SHIFT = 30.0

LANES = 128
HEAD_PAD = 128
IN_PAD = 1280
HALF = SEQ // 2
QUARTER = SEQ // 4
VMEM_LIMIT = 56 * 1024 * 1024

TM_IN = 1024
IN_GROUPS = 1
TM_MIX = 1024
MIX_GROUPS = 2
TQ = 1024
TOKEN_CHUNK = 512
OFFS_STRIDE = SEQ // TOKEN_CHUNK + 1
GATHER_WIN = 96
SCATTER_WIN = 128
SLOT_ALIGN_SHIFT = 4
GATHER_EXPERTS = 16
FFN_SEQS = 2


def _rms(x, g):
    return x * lax.rsqrt(jnp.mean(x * x, axis=-1, keepdims=True) + RMS_EPS) * g


def _params(*sem):
    return pltpu.CompilerParams(dimension_semantics=sem, vmem_limit_bytes=VMEM_LIMIT)


def _rope_kernel(pos_ref, freq_ref, cos_ref, sin_ref):
    ang = pos_ref[...] * freq_ref[...]
    cos_ref[...] = jnp.cos(ang)
    sin_ref[...] = jnp.sin(ang)


def _rope_tables(positions):
    half = QK_ROPE_DIM // 2
    freqs = 1.0 / (ROPE_THETA ** (jnp.arange(0, QK_ROPE_DIM, 2, dtype=F32) / QK_ROPE_DIM))
    rows = TOKENS * half // LANES
    pos = jnp.repeat(positions.astype(F32).reshape(-1), half).reshape(rows, LANES)
    freq = jnp.tile(freqs, LANES // half).reshape(1, LANES)
    cos, sin = pl.pallas_call(
        _rope_kernel,
        out_shape=(jax.ShapeDtypeStruct((rows, LANES), F32),) * 2,
        name="rope_tables",
    )(pos, freq)
    return cos.reshape(TOKENS, half), sin.reshape(TOKENS, half)


def _rope_expanders():
    half = QK_ROPE_DIM // 2
    lane = np.arange(LANES)
    hit = (lane[None, :] % half) == np.arange(half)[:, None]
    is_sin = (lane // QK_ROPE_DIM) % 2 == 1
    return (jnp.asarray((hit & ~is_sin[None, :]).astype(np.float32)),
            jnp.asarray((hit & is_sin[None, :]).astype(np.float32)))


def _chan_kernel(cbd_ref, sbd_ref, w_ref, ab_ref):
    w = w_ref[...]
    ab_ref[:, :F_WIDTH] = jnp.dot(cbd_ref[...], w, precision=lax.Precision.HIGHEST,
                                  preferred_element_type=F32).astype(BF16)
    ab_ref[:, F_WIDTH:] = jnp.dot(sbd_ref[...], w, precision=lax.Precision.HIGHEST,
                                  preferred_element_type=F32).astype(BF16)


def _channel_mats(w_fourier):
    c = np.arange(F_GROUP_DIM)
    ang = 2.0 * np.pi * np.outer(c, c) / F_GROUP_DIM
    scale = F_GROUP_DIM ** -0.5
    eye = np.eye(F_GROUPS)
    cbd = np.kron(eye, np.cos(ang) * scale).astype(np.float32)
    sbd = np.kron(eye, np.sin(ang) * scale).astype(np.float32)
    wbd = (jnp.eye(F_GROUPS, dtype=F32)[:, None, :, None] * w_fourier[:, :, None, :]).reshape(F_WIDTH, F_WIDTH)
    return pl.pallas_call(
        _chan_kernel,
        out_shape=jax.ShapeDtypeStruct((F_WIDTH, 2 * F_WIDTH), BF16),
        name="channel_dft_fold",
    )(jnp.asarray(cbd), jnp.asarray(sbd), wbd)


def _inproj_kernel(x_ref, gmix_ref, win_ref, gq_ref, wq_ref, gkv_ref, wkv_ref, ab_ref, cos_ref, sin_ref,
                   ec_ref, es_ref, hsum_ref, v12_ref, q_ref, k_ref, v_ref, ksq_ref, vscr_ref):
    rows_per_group = x_ref.shape[0] // IN_GROUPS
    half_rows = rows_per_group // 2
    ksq_max = None
    for grp in range(IN_GROUPS):
        r0 = grp * rows_per_group
        rows = pl.ds(r0, rows_per_group)
        half = pl.ds(r0 // 2, half_rows)
        h = _rms(x_ref[rows, :], gmix_ref[...]).astype(BF16)
        u = jnp.dot(h, win_ref[...], preferred_element_type=F32)

        n_pairs = ab_ref.shape[0]
        for gp in range(n_pairs):
            pair = jnp.dot(u[:, LANES * gp:LANES * (gp + 1)].astype(BF16), ab_ref[gp],
                           preferred_element_type=F32)
            for part in range(2):
                c = part * n_pairs + gp
                cols = slice(LANES * c, LANES * (c + 1))
                vscr_ref[c, rows, :] = pair[:, LANES * part:LANES * (part + 1)]
                v12_ref[0, half, cols] = vscr_ref[c, pl.ds(r0, half_rows, stride=2), :].astype(BF16)
                v12_ref[1, half, cols] = vscr_ref[c, pl.ds(r0 + 1, half_rows, stride=2), :].astype(BF16)

        tab = (jnp.dot(cos_ref[rows, :], ec_ref[...], precision=lax.Precision.HIGHEST,
                       preferred_element_type=F32)
               + jnp.dot(sin_ref[rows, :], es_ref[...], precision=lax.Precision.HIGHEST,
                         preferred_element_type=F32))
        lane = lax.broadcasted_iota(I32, tab.shape, 1)
        rope0, rope1 = QK_NOPE_DIM, QK_NOPE_DIM + QK_ROPE_DIM
        back = LANES - QK_ROPE_DIM
        c1 = jnp.where(lane < rope0, 1.0, jnp.where(lane < rope1, tab, 0.0))
        c2 = jnp.where(lane >= rope1, tab, 0.0)

        q0 = F_WIDTH
        qn = _rms(u[:, q0:q0 + Q_LORA_RANK], gq_ref[...]).astype(BF16)
        qa = jnp.dot(qn, wq_ref[...], preferred_element_type=F32)
        q_blocks = []
        for hd in range(MLA_HEADS):
            blk = qa[:, HEAD_PAD * hd:HEAD_PAD * (hd + 1)]
            q_blocks.append(blk * c1 + pltpu.roll(blk * c2, back, 1))
            q_ref[rows, HEAD_PAD * hd:HEAD_PAD * (hd + 1)] = q_blocks[hd].astype(BF16)

        kv0 = q0 + Q_LORA_RANK
        kvn = _rms(u[:, kv0:kv0 + KV_LORA_RANK], gkv_ref[...]).astype(BF16)
        kv = jnp.dot(kvn, wkv_ref[...], preferred_element_type=F32)
        kr0 = kv0 + KV_LORA_RANK
        t = u[:, kr0:kr0 + LANES] * jnp.where(lane < 2 * QK_ROPE_DIM, tab, 0.0)
        kr = t + pltpu.roll(t, back, 1)
        kr = jnp.where((lane >= rope0) & (lane < rope1), pltpu.roll(kr, rope0, 1), 0.0)
        k_blocks = [kv[:, HEAD_PAD * hd:HEAD_PAD * (hd + 1)] + kr for hd in range(MLA_HEADS)]
        for hd in range(MLA_HEADS):
            k_ref[rows, HEAD_PAD * hd:HEAD_PAD * (hd + 1)] = k_blocks[hd].astype(BF16)
        v_ref[rows, :] = kv[:, MLA_HEADS * HEAD_PAD:].astype(BF16)
        sq = jnp.concatenate(
            [jnp.concatenate([(blk * blk).astype(BF16) for blk in blocks], axis=1)
             for blocks in (k_blocks, q_blocks)], axis=0)
        sq = jnp.dot(sq, hsum_ref[...], preferred_element_type=F32)
        grp_max = jnp.concatenate([jnp.max(sq[:rows_per_group], axis=0, keepdims=True),
                                   jnp.max(sq[rows_per_group:], axis=0, keepdims=True)], axis=0)
        ksq_max = grp_max if ksq_max is None else jnp.maximum(ksq_max, grp_max)
    ksq_ref[...] = ksq_max


def _in_projection(x2d, g_mix, w_in, g_q, w_q_up, g_kv, w_kv_up, ab, cos, sin):
    w_kr = w_in[:, -QK_ROPE_DIM:]
    half = QK_ROPE_DIM // 2
    w_kr_rot = jnp.concatenate([-w_kr[:, half:], w_kr[:, :half]], axis=1)
    win = jnp.concatenate(
        [w_in, w_kr_rot, jnp.zeros((D_MODEL, IN_PAD - w_in.shape[1] - QK_ROPE_DIM), F32)], axis=1).astype(BF16)
    scale = (QK_NOPE_DIM + QK_ROPE_DIM) ** -0.5
    wq = w_q_up.reshape(Q_LORA_RANK, MLA_HEADS, QK_NOPE_DIM + QK_ROPE_DIM) * scale
    wq_rope = wq[:, :, QK_NOPE_DIM:]
    wq_rot = jnp.concatenate([-wq_rope[:, :, half:], wq_rope[:, :, :half]], axis=2)
    wq = jnp.concatenate([wq, wq_rot], axis=2).reshape(Q_LORA_RANK, MLA_HEADS * HEAD_PAD).astype(BF16)
    wkv = w_kv_up.reshape(KV_LORA_RANK, MLA_HEADS, QK_NOPE_DIM + V_HEAD_DIM)
    wk = jnp.concatenate([wkv[:, :, :QK_NOPE_DIM],
                          jnp.zeros((KV_LORA_RANK, MLA_HEADS, HEAD_PAD - QK_NOPE_DIM), F32)], axis=2)
    wkv = jnp.concatenate([wk.reshape(KV_LORA_RANK, MLA_HEADS * HEAD_PAD),
                           wkv[:, :, QK_NOPE_DIM:].reshape(KV_LORA_RANK, MLA_HEADS * V_HEAD_DIM)],
                          axis=1).astype(BF16)

    ab = jnp.stack([jnp.concatenate([ab[LANES * g:LANES * (g + 1), LANES * g:LANES * (g + 1)],
                                     ab[LANES * g:LANES * (g + 1), F_WIDTH + LANES * g:F_WIDTH + LANES * (g + 1)]],
                                    axis=1) for g in range(F_WIDTH // LANES)])
    tm = TM_IN
    ec, es = _rope_expanders()
    hsum = jnp.asarray((np.arange(MLA_HEADS * HEAD_PAD)[:, None] // HEAD_PAD
                        == np.arange(LANES)[None, :]).astype(np.float32)).astype(BF16)
    full = lambda shape: pl.BlockSpec(shape, lambda i: (0,) * len(shape))
    tile = lambda w: pl.BlockSpec((tm, w), lambda i: (i, 0))
    per_half = HALF // tm
    per_seq = SEQ // tm
    v12_spec = pl.BlockSpec((None, 2, None, tm // 2, 2 * F_WIDTH),
                            lambda i: (i // per_seq, 0, (i % per_seq) // per_half, i % per_half, 0))
    return pl.pallas_call(
        _inproj_kernel,
        grid=(TOKENS // tm,),
        in_specs=[tile(D_MODEL), full((1, D_MODEL)), full(win.shape), full((1, Q_LORA_RANK)), full(wq.shape),
                  full((1, KV_LORA_RANK)), full(wkv.shape), full(ab.shape), tile(QK_ROPE_DIM // 2),
                  tile(QK_ROPE_DIM // 2), full(ec.shape), full(es.shape), full(hsum.shape)],
        out_specs=[v12_spec, tile(MLA_HEADS * HEAD_PAD), tile(MLA_HEADS * HEAD_PAD),
                   tile(MLA_HEADS * V_HEAD_DIM), pl.BlockSpec((None, 2, LANES), lambda i: (i, 0, 0))],
        out_shape=[jax.ShapeDtypeStruct((BATCH, 2, 2, QUARTER, 2 * F_WIDTH), BF16),
                   jax.ShapeDtypeStruct((TOKENS, MLA_HEADS * HEAD_PAD), BF16),
                   jax.ShapeDtypeStruct((TOKENS, MLA_HEADS * HEAD_PAD), BF16),
                   jax.ShapeDtypeStruct((TOKENS, MLA_HEADS * V_HEAD_DIM), BF16),
                   jax.ShapeDtypeStruct((TOKENS // tm, 2, LANES), F32)],
        scratch_shapes=[pltpu.VMEM((2 * F_WIDTH // LANES, tm, LANES), F32)],
        compiler_params=_params("parallel"),
        name="in_projection",
    )(x2d, g_mix.reshape(1, -1), win, g_q.reshape(1, -1), wq, g_kv.reshape(1, -1), wkv, ab, cos, sin, ec, es, hsum)


def _seq_dft_kernel(v_ref, m_ref, y_ref):
    sign = jnp.where(pl.program_id(1) == 0, 1.0, -1.0)
    parts = []
    for q in range(2):
        ab = (v_ref[q, 0].astype(F32) + sign * v_ref[q, 1].astype(F32)).astype(BF16)
        parts.append(jnp.dot(m_ref[q, :, :QUARTER], ab[:, :F_WIDTH], preferred_element_type=F32)
                     + jnp.dot(m_ref[q, :, QUARTER:], ab[:, F_WIDTH:], preferred_element_type=F32))
    y_ref[0] = (parts[0] + parts[1]).astype(BF16)
    y_ref[1] = (parts[0] - parts[1]).astype(BF16)


def _seq_dft_mats():
    i = np.arange(QUARTER)
    out = np.zeros((2, 2, QUARTER, 2 * QUARTER), np.float32)
    for p in range(2):
        for q in range(2):
            prod = np.outer(2 * i + p, 2 * i + q) % SEQ
            ang = 2.0 * np.pi * prod / SEQ
            out[p, q, :, :QUARTER] = np.cos(ang) / np.sqrt(SEQ)
            out[p, q, :, QUARTER:] = -np.sin(ang) / np.sqrt(SEQ)
    return out


def _sequence_dft(v):
    mats = jnp.asarray(_seq_dft_mats()).astype(BF16)
    return pl.pallas_call(
        _seq_dft_kernel,
        grid=(BATCH, 2),
        in_specs=[pl.BlockSpec((None, 2, 2, QUARTER, 2 * F_WIDTH), lambda b, p: (b, 0, 0, 0, 0)),
                  pl.BlockSpec((None, 2, QUARTER, 2 * QUARTER), lambda b, p: (p, 0, 0, 0))],
        out_specs=pl.BlockSpec((None, None, 2, QUARTER, F_WIDTH), lambda b, p: (b, p, 0, 0, 0)),
        out_shape=jax.ShapeDtypeStruct((BATCH, 2, 2, QUARTER, F_WIDTH), BF16),
        compiler_params=_params("parallel", "arbitrary"),
        name="sequence_dft",
    )(v, mats)


def _mla_kernel(q_ref, k_ref, v_ref, sq_ref, o_ref):
    v = v_ref[...]
    hp = pl.program_id(1)
    sq = jnp.max(sq_ref[...], axis=0)
    bounds = jnp.sqrt(sq[0:1] * sq[1:2]) * NORM_SLACK
    head_lane = lax.broadcasted_iota(I32, bounds.shape, 1)
    for j in range(2):
        cols = slice(HEAD_PAD * j, HEAD_PAD * (j + 1))
        out_cols = slice(V_HEAD_DIM * j, V_HEAD_DIM * (j + 1))
        bound = jnp.max(jnp.where(head_lane == 2 * hp + j, bounds, 0.0), axis=1, keepdims=True)
        safe = jnp.max(bound) <= SAFE_SHIFT

        def attend(row_shift):
            s = lax.dot_general(q_ref[:, cols], k_ref[:, cols], (((1,), (1,)), ((), ())),
                                preferred_element_type=F32)
            p = jnp.exp(s - row_shift(s))
            l = jnp.sum(p, axis=1, keepdims=True)
            o = jnp.dot(p.astype(BF16), v, preferred_element_type=F32) / l
            o_ref[:, out_cols] = o[:, out_cols].astype(BF16)

        pl.when(safe)(lambda: attend(lambda s: bound))
        pl.when(jnp.logical_not(safe))(lambda: attend(lambda s: jnp.max(s, axis=1, keepdims=True)))


def _mla_attention(q, k, v, ksq):
    q = q.reshape(BATCH, SEQ, -1)
    k = k.reshape(BATCH, SEQ, -1)
    v = v.reshape(BATCH, SEQ, -1)
    ksq = ksq.reshape(BATCH, SEQ // TM_IN, 2, LANES)
    out = pl.pallas_call(
        _mla_kernel,
        grid=(BATCH, MLA_HEADS // 2, SEQ // TQ),
        in_specs=[pl.BlockSpec((None, TQ, 2 * HEAD_PAD), lambda b, hp, i: (b, i, hp)),
                  pl.BlockSpec((None, SEQ, 2 * HEAD_PAD), lambda b, hp, i: (b, 0, hp)),
                  pl.BlockSpec((None, SEQ, 2 * V_HEAD_DIM), lambda b, hp, i: (b, 0, hp)),
                  pl.BlockSpec((None, SEQ // TM_IN, 2, LANES), lambda b, hp, i: (b, 0, 0, 0))],
        out_specs=pl.BlockSpec((None, TQ, 2 * V_HEAD_DIM), lambda b, hp, i: (b, i, hp)),
        out_shape=jax.ShapeDtypeStruct((BATCH, SEQ, MLA_HEADS * V_HEAD_DIM), BF16),
        compiler_params=_params("parallel", "parallel", "arbitrary"),
        name="mla_attention",
    )(q, k, v, ksq)
    return out.reshape(TOKENS, MLA_HEADS * V_HEAD_DIM)


def _memkv_kernel(mem_ref, g_ref, w_ref, k_ref, v_ref):
    mn = _rms(mem_ref[...], g_ref[...]).astype(BF16)
    kv = jnp.dot(mn, w_ref[...], preferred_element_type=F32)
    k_ref[...] = kv[:, :D_MODEL].astype(BF16)
    v_ref[...] = kv[:, D_MODEL:].astype(BF16)


def _memory_kv(mem, g_mem_kv, w_mem_kv):
    blk = pl.BlockSpec((None, MEM_LEN, D_MODEL), lambda b: (b, 0, 0))
    return pl.pallas_call(
        _memkv_kernel,
        grid=(BATCH,),
        in_specs=[blk, pl.BlockSpec((1, D_MODEL), lambda b: (0, 0)),
                  pl.BlockSpec((D_MODEL, 2 * D_MODEL), lambda b: (0, 0))],
        out_specs=[blk, blk],
        out_shape=[jax.ShapeDtypeStruct((BATCH, MEM_LEN, D_MODEL), BF16)] * 2,
        compiler_params=_params("parallel"),
        name="memory_kv",
    )(mem, g_mem_kv.reshape(1, -1), w_mem_kv.astype(BF16))


def _mix_kernel(x_ref, yf_ref, ya_ref, wo_ref, gq_ref, wmq_ref, mk_ref, mv_ref, wmo_ref, gf_ref, wr_ref,
                x2_ref, hext_ref, aff_ref, zscr_ref):
    tm = x_ref.shape[0]
    rows_per_group = tm // MIX_GROUPS
    for grp in range(MIX_GROUPS):
        rows = pl.ds(grp * rows_per_group, rows_per_group)
        half = pl.ds(grp * rows_per_group // 2, rows_per_group // 2)
        wo_f = wo_ref[:F_WIDTH, :]
        z_even = jnp.dot(yf_ref[0, half, :], wo_f, preferred_element_type=F32)
        z_odd = jnp.dot(yf_ref[1, half, :], wo_f, preferred_element_type=F32)
        for c in range(zscr_ref.shape[0]):
            cols = slice(LANES * c, LANES * (c + 1))
            zscr_ref[c, pl.ds(grp * rows_per_group, rows_per_group // 2, stride=2), :] = z_even[:, cols]
            zscr_ref[c, pl.ds(grp * rows_per_group + 1, rows_per_group // 2, stride=2), :] = z_odd[:, cols]
        z = jnp.concatenate([zscr_ref[c, rows, :] for c in range(zscr_ref.shape[0])], axis=1)
        x1 = x_ref[rows, :] + z + jnp.dot(ya_ref[rows, :], wo_ref[F_WIDTH:, :], preferred_element_type=F32)

        hq = _rms(x1, gq_ref[...]).astype(BF16)
        qm = (jnp.dot(hq, wmq_ref[...], preferred_element_type=F32) * (MEM_HEAD_DIM ** -0.5)).astype(BF16)
        heads = []
        for hd in range(MEM_HEADS):
            sl = slice(MEM_HEAD_DIM * hd, MEM_HEAD_DIM * (hd + 1))
            s = lax.dot_general(qm[:, sl], mk_ref[:, sl], (((1,), (1,)), ((), ())),
                                preferred_element_type=F32)
            p = jnp.exp(s - jnp.max(s, axis=1, keepdims=True))
            l = jnp.sum(p, axis=1, keepdims=True)
            heads.append((jnp.dot(p.astype(BF16), mv_ref[:, sl], preferred_element_type=F32) / l).astype(BF16))
        o = jnp.concatenate(heads, axis=1)
        x2 = x1 + jnp.dot(o, wmo_ref[...], preferred_element_type=F32)
        x2_ref[rows, :] = x2

        h3 = _rms(x2, gf_ref[...])
        h3_hi = h3.astype(BF16)
        hext_ref[rows, :D_MODEL] = h3_hi
        h3_lo = (h3 - h3_hi.astype(F32)).astype(BF16)
        hi_terms = jnp.dot(h3_hi, wr_ref[...], preferred_element_type=F32)
        logits = (hi_terms[:, :LANES] + hi_terms[:, LANES:]
                  + jnp.dot(h3_lo, wr_ref[:, :LANES], preferred_element_type=F32))
        lane = lax.broadcasted_iota(I32, logits.shape, 1)
        logits = jnp.where(lane < N_EXPERTS, logits, -jnp.inf)
        e = jnp.exp(logits - jnp.max(logits, axis=1, keepdims=True))
        aff = e / jnp.sum(e, axis=1, keepdims=True)
        aff_ref[rows, :] = aff
        hi = aff.astype(BF16)
        r1 = aff - hi.astype(F32)
        mid = r1.astype(BF16)
        lo = (r1 - mid.astype(F32)).astype(BF16)
        hext_ref[rows, D_MODEL:] = jnp.where(
            lane < N_EXPERTS, hi,
            jnp.where(lane < 2 * N_EXPERTS, pltpu.roll(mid.astype(F32), N_EXPERTS, 1).astype(BF16),
                      pltpu.roll(lo.astype(F32), 2 * N_EXPERTS, 1).astype(BF16)))


def _mixing(x2d, y_f, y_a, w_out, g_mem_q, w_mem_q, mk, mv, w_mem_o, g_ffn, w_router):
    tm = TM_MIX
    wr = jnp.concatenate([w_router, jnp.zeros((D_MODEL, LANES - N_EXPERTS), F32)], axis=1)
    wr_hi = wr.astype(BF16)
    wr_cat = jnp.concatenate([wr_hi, (wr - wr_hi.astype(F32)).astype(BF16)], axis=1)
    full = lambda shape: pl.BlockSpec(shape, lambda i: (0,) * len(shape))
    tile = lambda w: pl.BlockSpec((tm, w), lambda i: (i, 0))
    per_half = HALF // tm
    per_seq = SEQ // tm
    per_batch = pl.BlockSpec((None, MEM_LEN, D_MODEL), lambda i: (i // per_seq, 0, 0))
    yf_spec = pl.BlockSpec((None, 2, None, tm // 2, F_WIDTH),
                           lambda i: (i // per_seq, 0, (i % per_seq) // per_half, i % per_half, 0))
    return pl.pallas_call(
        _mix_kernel,
        grid=(TOKENS // tm,),
        in_specs=[tile(D_MODEL), yf_spec, tile(F_WIDTH), full((D_MODEL, D_MODEL)), full((1, D_MODEL)),
                  full((D_MODEL, D_MODEL)), per_batch, per_batch, full((D_MODEL, D_MODEL)), full((1, D_MODEL)),
                  full((D_MODEL, 2 * LANES))],
        out_specs=[tile(D_MODEL), tile(D_MODEL + LANES), tile(LANES)],
        out_shape=[jax.ShapeDtypeStruct((TOKENS, D_MODEL), F32),
                   jax.ShapeDtypeStruct((TOKENS, D_MODEL + LANES), BF16),
                   jax.ShapeDtypeStruct((TOKENS, LANES), F32)],
        scratch_shapes=[pltpu.VMEM((D_MODEL // LANES, tm, LANES), F32)],
        compiler_params=_params("parallel"),
        name="mix_memattn_router",
    )(x2d, y_f, y_a, w_out.astype(BF16), g_mem_q.reshape(1, -1), w_mem_q.astype(BF16), mk, mv,
      w_mem_o.astype(BF16), g_ffn.reshape(1, -1), wr_cat)


def _topk_kernel(aff_ref, slot_ref, offs_ref):
    aff = aff_ref[...]
    rows = aff.shape[0]

    thr = jnp.zeros((rows, 1), I32)
    for bit in range(30, -1, -1):
        cand = thr | (1 << bit)
        cnt = jnp.sum(jnp.where(aff >= pltpu.bitcast(cand, F32), 1.0, 0.0), axis=1, keepdims=True)
        thr = jnp.where(cnt >= CAPACITY, cand, thr)
    thr_f = pltpu.bitcast(thr, F32)

    chunk = 256
    r = lax.broadcasted_iota(I32, (chunk, chunk), 0)
    c = lax.broadcasted_iota(I32, (chunk, chunk), 1)
    tri = jnp.where(r < c, 1.0, 0.0).astype(BF16)

    def exclusive_count(mask):
        off = jnp.zeros((rows, 1), F32)
        outs = []
        for j in range(SEQ // chunk):
            mj = mask[:, chunk * j:chunk * (j + 1)]
            outs.append(jnp.dot(mj.astype(BF16), tri, preferred_element_type=F32) + off)
            off = off + jnp.sum(mj, axis=1, keepdims=True)
        return jnp.concatenate(outs, axis=1), off

    gt = aff > thr_f
    tie = jnp.where(aff == thr_f, 1.0, 0.0)
    n_gt = jnp.sum(jnp.where(gt, 1.0, 0.0), axis=1, keepdims=True)
    tie_rank, _ = exclusive_count(tie)
    sel = jnp.where(gt | ((tie > 0.0) & (tie_rank < CAPACITY - n_gt)), 1.0, 0.0)
    slot, _ = exclusive_count(sel)
    slot_ref[...] = jnp.where(sel > 0.0, slot.astype(I32), -1)
    tok = lax.broadcasted_iota(I32, (SEQ, LANES), 0)
    j = lax.broadcasted_iota(I32, (SEQ, LANES), 1)
    before = jnp.where(tok < j * TOKEN_CHUNK, 1.0, 0.0).astype(BF16)
    offs_ref[...] = jnp.dot(sel.astype(BF16), before, preferred_element_type=F32).astype(I32)


def _expert_slots(aff):
    aff_t = aff[:, :N_EXPERTS].reshape(BATCH, SEQ, N_EXPERTS).transpose(0, 2, 1).reshape(BATCH * N_EXPERTS, SEQ)
    slots, offs = pl.pallas_call(
        _topk_kernel,
        out_shape=[jax.ShapeDtypeStruct((BATCH * N_EXPERTS, SEQ), I32),
                   jax.ShapeDtypeStruct((BATCH * N_EXPERTS, LANES), I32)],
        compiler_params=pltpu.CompilerParams(vmem_limit_bytes=VMEM_LIMIT),
        name="expert_topk",
    )(aff_t)
    return slots, offs[:, :OFFS_STRIDE].reshape(-1)


def _window_start(first, win):
    start = jnp.minimum((first >> SLOT_ALIGN_SHIFT) << SLOT_ALIGN_SHIFT, CAPACITY - win)
    return pl.multiple_of(start, 1 << SLOT_ALIGN_SHIFT)


def _gather_kernel(offs_ref, slot_ref, h_ref, x_ref):
    win = GATHER_WIN
    last = CAPACITY - win
    b = pl.program_id(0)
    e0 = pl.program_id(1) * GATHER_EXPERTS
    j = pl.program_id(2)

    @pl.when(j == 0)
    def _():
        x_ref[...] = jnp.zeros_like(x_ref)

    def bounds(e):
        base = (b * N_EXPERTS + e0 + e) * OFFS_STRIDE + j
        return offs_ref[base], offs_ref[base + 1]

    row = lax.broadcasted_iota(I32, (win, TOKEN_CHUNK), 0)
    h_c = h_ref[...]
    starts = [_window_start(bounds(e)[0], win) for e in range(GATHER_EXPERTS)]
    onehot = jnp.concatenate(
        [jnp.where(row + starts[e] == slot_ref[e:e + 1, :], 1.0, 0.0).astype(BF16)
         for e in range(GATHER_EXPERTS)], axis=0)
    picked = jnp.dot(onehot, h_c, preferred_element_type=F32).astype(BF16)
    for e in range(GATHER_EXPERTS):
        rows = pl.ds(starts[e], win)
        x_ref[e, rows, :] = x_ref[e, rows, :] + picked[e * win:(e + 1) * win]

    for e in range(GATHER_EXPERTS):
        _, end = bounds(e)
        covered = starts[e] + win
        slot_e = slot_ref[e:e + 1, :]

        def extra_window(i, carry):
            lo = covered + i * win
            r0 = pl.multiple_of(jnp.minimum(lo, last), 1 << SLOT_ALIGN_SHIFT)
            hot = jnp.where((row + r0 == slot_e) & (slot_e >= lo), 1.0, 0.0).astype(BF16)
            rows = pl.ds(r0, win)
            x_ref[e, rows, :] = x_ref[e, rows, :] + jnp.dot(hot, h_c, preferred_element_type=F32).astype(BF16)
            return carry

        lax.fori_loop(0, jnp.maximum(end - covered + win - 1, 0) // win, extra_window, 0)


def _gather(offs, slots, h_ext):
    slots = slots.reshape(BATCH, N_EXPERTS, SEQ)
    h_ext = h_ext.reshape(BATCH, SEQ, D_MODEL + LANES)
    return pl.pallas_call(
        _gather_kernel,
        grid_spec=pltpu.PrefetchScalarGridSpec(
            num_scalar_prefetch=1,
            grid=(BATCH, N_EXPERTS // GATHER_EXPERTS, SEQ // TOKEN_CHUNK),
            in_specs=[pl.BlockSpec((None, GATHER_EXPERTS, TOKEN_CHUNK), lambda b, g, j, offs: (b, g, j)),
                      pl.BlockSpec((None, TOKEN_CHUNK, D_MODEL + LANES), lambda b, g, j, offs: (b, j, 0))],
            out_specs=pl.BlockSpec((None, GATHER_EXPERTS, None, CAPACITY, D_MODEL + LANES),
                                   lambda b, g, j, offs: (b // FFN_SEQS, g, b % FFN_SEQS, 0, 0))),
        out_shape=jax.ShapeDtypeStruct((BATCH // FFN_SEQS, N_EXPERTS, FFN_SEQS, CAPACITY, D_MODEL + LANES),
                                       BF16),
        compiler_params=_params("parallel", "parallel", "arbitrary"),
        name="expert_gather",
    )(offs, slots, h_ext)


def _expert_kernel(x_ref, wg_ref, wu_ref, wd_ref, y_ref, wg_s, wu_s, wd_s):
    g = pl.program_id(0)
    part = pl.program_id(1)
    slab = wg_ref.shape[0]

    @pl.when(g < N_EXPERTS)
    def _():
        rows = pl.ds(pl.multiple_of(part * slab, slab), slab)
        wg_s[g % 2, rows, :] = wg_ref[...].astype(BF16)
        wu_s[g % 2, rows, :] = wu_ref[...].astype(BF16)
        wd_s[g % 2, rows, :] = wd_ref[...].astype(BF16)

    @pl.when(g == 0)
    def _():
        y_ref[...] = jnp.zeros_like(y_ref)

    @pl.when(g > 0)
    def _():
        e = g - 1
        cur = e % 2
        rows = x_ref.shape[0] * x_ref.shape[1]
        xin = x_ref[:, :, :D_MODEL].reshape(rows, D_MODEL)
        ext = x_ref[:, :, D_MODEL:].reshape(rows, LANES).astype(F32)
        lane = lax.broadcasted_iota(I32, ext.shape, 1)
        mine = (lane == e) | (lane == e + N_EXPERTS) | (lane == e + 2 * N_EXPERTS)
        gate = jnp.sum(jnp.where(mine, ext, 0.0), axis=1, keepdims=True)

        a = jnp.dot(xin, wg_s[cur], preferred_element_type=F32)
        b = jnp.dot(xin, wu_s[cur], preferred_element_type=F32)
        hid = (a / (1.0 + jnp.exp(-a)) * b).astype(BF16)
        y = (jnp.dot(hid, wd_s[cur], preferred_element_type=F32) * gate).astype(BF16)
        y_ref[...] = y.reshape(y_ref.shape)


def _experts(xin, w_gate, w_up, w_down):
    parts = BATCH // FFN_SEQS
    slab = D_MODEL // parts
    prev = lambda g: jnp.maximum(g - 1, 0)
    wspec = pl.BlockSpec((None, slab, D_MODEL), lambda g, p: (jnp.minimum(g, N_EXPERTS - 1), p, 0))
    return pl.pallas_call(
        _expert_kernel,
        grid=(N_EXPERTS + 1, parts),
        in_specs=[pl.BlockSpec((None, None, FFN_SEQS, CAPACITY, D_MODEL + LANES),
                               lambda g, p: (p, prev(g), 0, 0, 0)),
                  wspec, wspec, wspec],
        out_specs=pl.BlockSpec((None, None, FFN_SEQS, CAPACITY, D_MODEL),
                               lambda g, p: (p, jnp.where(g == 0, N_EXPERTS, g - 1), 0, 0, 0)),
        out_shape=jax.ShapeDtypeStruct((parts, N_EXPERTS + 1, FFN_SEQS, CAPACITY, D_MODEL), BF16),
        scratch_shapes=[pltpu.VMEM((2, D_MODEL, D_MODEL), BF16)] * 3,
        compiler_params=_params("arbitrary", "arbitrary"),
        name="expert_ffn",
    )(xin, w_gate, w_up, w_down)


def _combine_kernel(offs_ref, x2_ref, slot_ref, y_ref, g_ref, o_ref):
    b = pl.program_id(0)
    j = pl.program_id(1)
    win = SCATTER_WIN
    lane = lax.broadcasted_iota(I32, (TOKEN_CHUNK, win), 1)
    slot = slot_ref[...]

    def bounds(e):
        base = (b * N_EXPERTS + e) * OFFS_STRIDE + j
        return offs_ref[base], offs_ref[base + 1]

    starts = [_window_start(bounds(e)[0], win) for e in range(N_EXPERTS)]
    onehot = jnp.concatenate(
        [jnp.where(lane + starts[e] == slot[:, e:e + 1], 1.0, 0.0).astype(BF16) for e in range(N_EXPERTS)],
        axis=1)
    rows = jnp.concatenate([y_ref[e, pl.ds(starts[e], win), :] for e in range(N_EXPERTS)], axis=0)
    x3 = x2_ref[...] + jnp.dot(onehot, rows, preferred_element_type=F32)

    leftover = [bounds(e)[1] > starts[e] + win for e in range(N_EXPERTS)]
    any_leftover = functools.reduce(jnp.logical_or, leftover)

    @pl.when(jnp.logical_not(any_leftover))
    def _():
        o_ref[...] = _rms(x3, g_ref[...])

    @pl.when(any_leftover)
    def _():
        o_ref[...] = x3
        tail_lane = lax.broadcasted_iota(I32, (TOKEN_CHUNK, CAPACITY - win), 1) + win
        for e in range(N_EXPERTS):
            @pl.when(leftover[e])
            def _():
                sl = slot[:, e:e + 1]
                hot = jnp.where((tail_lane == sl) & (sl >= starts[e] + win), 1.0, 0.0).astype(BF16)
                o_ref[...] += jnp.dot(hot, y_ref[e, win:, :], preferred_element_type=F32)

        o_ref[...] = _rms(o_ref[...], g_ref[...])


def _combine(offs, x2, slots, y, g_final):
    slots_t = slots.reshape(BATCH, N_EXPERTS, SEQ).transpose(0, 2, 1)
    x2 = x2.reshape(BATCH, SEQ, D_MODEL)
    tm = TOKEN_CHUNK
    return pl.pallas_call(
        _combine_kernel,
        grid_spec=pltpu.PrefetchScalarGridSpec(
            num_scalar_prefetch=1,
            grid=(BATCH, SEQ // tm),
            in_specs=[pl.BlockSpec((None, tm, D_MODEL), lambda b, i, offs: (b, i, 0)),
                      pl.BlockSpec((None, tm, N_EXPERTS), lambda b, i, offs: (b, i, 0)),
                      pl.BlockSpec((None, N_EXPERTS, None, CAPACITY, D_MODEL),
                                   lambda b, i, offs: (b // FFN_SEQS, 0, b % FFN_SEQS, 0, 0)),
                      pl.BlockSpec((1, D_MODEL), lambda b, i, offs: (0, 0))],
            out_specs=pl.BlockSpec((None, tm, D_MODEL), lambda b, i, offs: (b, i, 0))),
        out_shape=jax.ShapeDtypeStruct((BATCH, SEQ, D_MODEL), F32),
        compiler_params=_params("parallel", "arbitrary"),
        name="combine_final_norm",
    )(offs, x2, slots_t, y, g_final.reshape(1, -1))


def kernel(x, mem, positions, g_mix, w_in, g_q_lat, w_q_up, g_kv_lat, w_kv_up, w_fourier, w_out, g_mem_q,
           g_mem_kv, w_mem_q, w_mem_kv, w_mem_o, g_ffn, w_router, w_exp_gate, w_exp_up, w_exp_down, g_final):
    assert x.shape == (BATCH, SEQ, D_MODEL) and g_mix.shape[0] == 1
    x2d = x.reshape(TOKENS, D_MODEL)
    cos, sin = _rope_tables(positions)
    ab = _channel_mats(w_fourier[0])
    v12, q, k, v, ksq = _in_projection(x2d, g_mix[0], w_in[0], g_q_lat[0], w_q_up[0], g_kv_lat[0], w_kv_up[0],
                                       ab, cos, sin)
    y_f = _sequence_dft(v12)
    y_a = _mla_attention(q, k, v, ksq)
    mk, mv = _memory_kv(mem, g_mem_kv[0], w_mem_kv[0])
    x2, h_ext, aff = _mixing(x2d, y_f, y_a, w_out[0], g_mem_q[0], w_mem_q[0], mk, mv, w_mem_o[0], g_ffn[0],
                             w_router[0])
    slots, offs = _expert_slots(aff)
    xin = _gather(offs, slots, h_ext)
    y = _experts(xin, w_exp_gate[0], w_exp_up[0], w_exp_down[0])
    return _combine(offs, x2, slots, y, g_final)
```

```python
import functools

import numpy as np
import jax
import jax.numpy as jnp
from jax import lax
from jax.experimental import pallas as pl
from jax.experimental.pallas import tpu as pltpu

F32 = jnp.float32
BF16 = jnp.bfloat16
I32 = jnp.int32

D_MODEL = 1024
BATCH = 4
SEQ = 4096
TOKENS = BATCH * SEQ
MEM_LEN = 256
RMS_EPS = 1e-6
F_GROUPS = 8
F_GROUP_DIM = 64
F_WIDTH = F_GROUPS * F_GROUP_DIM
MLA_HEADS = 8
QK_NOPE_DIM = 64
QK_ROPE_DIM = 32
V_HEAD_DIM = 64
Q_LORA_RANK = 384
KV_LORA_RANK = 256
ROPE_THETA = 10000.0
MEM_HEADS = 4
MEM_HEAD_DIM = D_MODEL // MEM_HEADS
N_EXPERTS = 16
CAPACITY = 2 * SEQ // N_EXPERTS

NORM_SLACK = 1.01
SAFE_SHIFT = 30.0

LANES = 128
HEAD_PAD = 128
IN_PAD = 1280
HALF = SEQ // 2
QUARTER = SEQ // 4
VMEM_LIMIT = 56 * 1024 * 1024

TM_IN = 1024
IN_GROUPS = 1
TM_MIX = 1024
MIX_GROUPS = 2
TQ = 1024
TOKEN_CHUNK = 512
OFFS_STRIDE = SEQ // TOKEN_CHUNK + 1
GATHER_WIN = 96
SCATTER_WIN = 128
SLOT_ALIGN_SHIFT = 4
GATHER_EXPERTS = 16
FFN_SEQS = 2


def _rms(x, g):
    return x * lax.rsqrt(jnp.mean(x * x, axis=-1, keepdims=True) + RMS_EPS) * g


def _params(*sem):
    return pltpu.CompilerParams(dimension_semantics=sem, vmem_limit_bytes=VMEM_LIMIT)


def _rope_kernel(pos_ref, freq_ref, cos_ref, sin_ref):
    ang = pos_ref[...] * freq_ref[...]
    cos_ref[...] = jnp.cos(ang)
    sin_ref[...] = jnp.sin(ang)


def _rope_tables(positions):
    half = QK_ROPE_DIM // 2
    freqs = 1.0 / (ROPE_THETA ** (jnp.arange(0, QK_ROPE_DIM, 2, dtype=F32) / QK_ROPE_DIM))
    rows = TOKENS * half // LANES
    pos = jnp.repeat(positions.astype(F32).reshape(-1), half).reshape(rows, LANES)
    freq = jnp.tile(freqs, LANES // half).reshape(1, LANES)
    cos, sin = pl.pallas_call(
        _rope_kernel,
        out_shape=(jax.ShapeDtypeStruct((rows, LANES), F32),) * 2,
        name="rope_tables",
    )(pos, freq)
    return cos.reshape(TOKENS, half), sin.reshape(TOKENS, half)


def _rope_expanders():
    half = QK_ROPE_DIM // 2
    lane = np.arange(LANES)
    hit = (lane[None, :] % half) == np.arange(half)[:, None]
    is_sin = (lane // QK_ROPE_DIM) % 2 == 1
    return (jnp.asarray((hit & ~is_sin[None, :]).astype(np.float32)),
            jnp.asarray((hit & is_sin[None, :]).astype(np.float32)))


def _chan_kernel(cbd_ref, sbd_ref, w_ref, ab_ref):
    w = w_ref[...]
    ab_ref[:, :F_WIDTH] = jnp.dot(cbd_ref[...], w, precision=lax.Precision.HIGHEST,
                                  preferred_element_type=F32).astype(BF16)
    ab_ref[:, F_WIDTH:] = jnp.dot(sbd_ref[...], w, precision=lax.Precision.HIGHEST,
                                  preferred_element_type=F32).astype(BF16)


def _channel_mats(w_fourier):
    c = np.arange(F_GROUP_DIM)
    ang = 2.0 * np.pi * np.outer(c, c) / F_GROUP_DIM
    scale = F_GROUP_DIM ** -0.5
    eye = np.eye(F_GROUPS)
    cbd = np.kron(eye, np.cos(ang) * scale).astype(np.float32)
    sbd = np.kron(eye, np.sin(ang) * scale).astype(np.float32)
    wbd = (jnp.eye(F_GROUPS, dtype=F32)[:, None, :, None] * w_fourier[:, :, None, :]).reshape(F_WIDTH, F_WIDTH)
    return pl.pallas_call(
        _chan_kernel,
        out_shape=jax.ShapeDtypeStruct((F_WIDTH, 2 * F_WIDTH), BF16),
        name="channel_dft_fold",
    )(jnp.asarray(cbd), jnp.asarray(sbd), wbd)


def _inproj_kernel(x_ref, gmix_ref, win_ref, gq_ref, wq_ref, gkv_ref, wkv_ref, ab_ref, cos_ref, sin_ref,
                   ec_ref, es_ref, hsum_ref, v12_ref, q_ref, k_ref, v_ref, ksq_ref, vscr_ref):
    rows_per_group = x_ref.shape[0] // IN_GROUPS
    half_rows = rows_per_group // 2
    ksq_max = None
    for grp in range(IN_GROUPS):
        r0 = grp * rows_per_group
        rows = pl.ds(r0, rows_per_group)
        half = pl.ds(r0 // 2, half_rows)
        h = _rms(x_ref[rows, :], gmix_ref[...]).astype(BF16)
        u = jnp.dot(h, win_ref[...], preferred_element_type=F32)

        n_pairs = ab_ref.shape[0]
        for gp in range(n_pairs):
            pair = jnp.dot(u[:, LANES * gp:LANES * (gp + 1)].astype(BF16), ab_ref[gp],
                           preferred_element_type=F32)
            for part in range(2):
                c = part * n_pairs + gp
                cols = slice(LANES * c, LANES * (c + 1))
                vscr_ref[c, rows, :] = pair[:, LANES * part:LANES * (part + 1)]
                v12_ref[0, half, cols] = vscr_ref[c, pl.ds(r0, half_rows, stride=2), :].astype(BF16)
                v12_ref[1, half, cols] = vscr_ref[c, pl.ds(r0 + 1, half_rows, stride=2), :].astype(BF16)

        tab = (jnp.dot(cos_ref[rows, :], ec_ref[...], precision=lax.Precision.HIGHEST,
                       preferred_element_type=F32)
               + jnp.dot(sin_ref[rows, :], es_ref[...], precision=lax.Precision.HIGHEST,
                         preferred_element_type=F32))
        lane = lax.broadcasted_iota(I32, tab.shape, 1)
        rope0, rope1 = QK_NOPE_DIM, QK_NOPE_DIM + QK_ROPE_DIM
        back = LANES - QK_ROPE_DIM
        c1 = jnp.where(lane < rope0, 1.0, jnp.where(lane < rope1, tab, 0.0))
        c2 = jnp.where(lane >= rope1, tab, 0.0)

        q0 = F_WIDTH
        qn = _rms(u[:, q0:q0 + Q_LORA_RANK], gq_ref[...]).astype(BF16)
        qa = jnp.dot(qn, wq_ref[...], preferred_element_type=F32)
        q_blocks = []
        for hd in range(MLA_HEADS):
            blk = qa[:, HEAD_PAD * hd:HEAD_PAD * (hd + 1)]
            q_blocks.append(blk * c1 + pltpu.roll(blk * c2, back, 1))
            q_ref[rows, HEAD_PAD * hd:HEAD_PAD * (hd + 1)] = q_blocks[hd].astype(BF16)

        kv0 = q0 + Q_LORA_RANK
        kvn = _rms(u[:, kv0:kv0 + KV_LORA_RANK], gkv_ref[...]).astype(BF16)
        kv = jnp.dot(kvn, wkv_ref[...], preferred_element_type=F32)
        kr0 = kv0 + KV_LORA_RANK
        t = u[:, kr0:kr0 + LANES] * jnp.where(lane < 2 * QK_ROPE_DIM, tab, 0.0)
        kr = t + pltpu.roll(t, back, 1)
        kr = jnp.where((lane >= rope0) & (lane < rope1), pltpu.roll(kr, rope0, 1), 0.0)
        k_blocks = [kv[:, HEAD_PAD * hd:HEAD_PAD * (hd + 1)] + kr for hd in range(MLA_HEADS)]
        for hd in range(MLA_HEADS):
            k_ref[rows, HEAD_PAD * hd:HEAD_PAD * (hd + 1)] = k_blocks[hd].astype(BF16)
        v_ref[rows, :] = kv[:, MLA_HEADS * HEAD_PAD:].astype(BF16)
        sq = jnp.concatenate(
            [jnp.concatenate([(blk * blk).astype(BF16) for blk in blocks], axis=1)
             for blocks in (k_blocks, q_blocks)], axis=0)
        sq = jnp.dot(sq, hsum_ref[...], preferred_element_type=F32)
        grp_max = jnp.concatenate([jnp.max(sq[:rows_per_group], axis=0, keepdims=True),
                                   jnp.max(sq[rows_per_group:], axis=0, keepdims=True)], axis=0)
        ksq_max = grp_max if ksq_max is None else jnp.maximum(ksq_max, grp_max)
    ksq_ref[...] = ksq_max


def _in_projection(x2d, g_mix, w_in, g_q, w_q_up, g_kv, w_kv_up, ab, cos, sin):
    w_kr = w_in[:, -QK_ROPE_DIM:]
    half = QK_ROPE_DIM // 2
    w_kr_rot = jnp.concatenate([-w_kr[:, half:], w_kr[:, :half]], axis=1)
    win = jnp.concatenate(
        [w_in, w_kr_rot, jnp.zeros((D_MODEL, IN_PAD - w_in.shape[1] - QK_ROPE_DIM), F32)], axis=1).astype(BF16)
    scale = (QK_NOPE_DIM + QK_ROPE_DIM) ** -0.5
    wq = w_q_up.reshape(Q_LORA_RANK, MLA_HEADS, QK_NOPE_DIM + QK_ROPE_DIM) * scale
    wq_rope = wq[:, :, QK_NOPE_DIM:]
    wq_rot = jnp.concatenate([-wq_rope[:, :, half:], wq_rope[:, :, :half]], axis=2)
    wq = jnp.concatenate([wq, wq_rot], axis=2).reshape(Q_LORA_RANK, MLA_HEADS * HEAD_PAD).astype(BF16)
    wkv = w_kv_up.reshape(KV_LORA_RANK, MLA_HEADS, QK_NOPE_DIM + V_HEAD_DIM)
    wk = jnp.concatenate([wkv[:, :, :QK_NOPE_DIM],
                          jnp.zeros((KV_LORA_RANK, MLA_HEADS, HEAD_PAD - QK_NOPE_DIM), F32)], axis=2)
    wkv = jnp.concatenate([wk.reshape(KV_LORA_RANK, MLA_HEADS * HEAD_PAD),
                           wkv[:, :, QK_NOPE_DIM:].reshape(KV_LORA_RANK, MLA_HEADS * V_HEAD_DIM)],
                          axis=1).astype(BF16)

    ab = jnp.stack([jnp.concatenate([ab[LANES * g:LANES * (g + 1), LANES * g:LANES * (g + 1)],
                                     ab[LANES * g:LANES * (g + 1), F_WIDTH + LANES * g:F_WIDTH + LANES * (g + 1)]],
                                    axis=1) for g in range(F_WIDTH // LANES)])
    tm = TM_IN
    ec, es = _rope_expanders()
    hsum = jnp.asarray((np.arange(MLA_HEADS * HEAD_PAD)[:, None] // HEAD_PAD
                        == np.arange(LANES)[None, :]).astype(np.float32)).astype(BF16)
    full = lambda shape: pl.BlockSpec(shape, lambda i: (0,) * len(shape))
    tile = lambda w: pl.BlockSpec((tm, w), lambda i: (i, 0))
    per_half = HALF // tm
    per_seq = SEQ // tm
    v12_spec = pl.BlockSpec((None, 2, None, tm // 2, 2 * F_WIDTH),
                            lambda i: (i // per_seq, 0, (i % per_seq) // per_half, i % per_half, 0))
    return pl.pallas_call(
        _inproj_kernel,
        grid=(TOKENS // tm,),
        in_specs=[tile(D_MODEL), full((1, D_MODEL)), full(win.shape), full((1, Q_LORA_RANK)), full(wq.shape),
                  full((1, KV_LORA_RANK)), full(wkv.shape), full(ab.shape), tile(QK_ROPE_DIM // 2),
                  tile(QK_ROPE_DIM // 2), full(ec.shape), full(es.shape), full(hsum.shape)],
        out_specs=[v12_spec, tile(MLA_HEADS * HEAD_PAD), tile(MLA_HEADS * HEAD_PAD),
                   tile(MLA_HEADS * V_HEAD_DIM), pl.BlockSpec((None, 2, LANES), lambda i: (i, 0, 0))],
        out_shape=[jax.ShapeDtypeStruct((BATCH, 2, 2, QUARTER, 2 * F_WIDTH), BF16),
                   jax.ShapeDtypeStruct((TOKENS, MLA_HEADS * HEAD_PAD), BF16),
                   jax.ShapeDtypeStruct((TOKENS, MLA_HEADS * HEAD_PAD), BF16),
                   jax.ShapeDtypeStruct((TOKENS, MLA_HEADS * V_HEAD_DIM), BF16),
                   jax.ShapeDtypeStruct((TOKENS // tm, 2, LANES), F32)],
        scratch_shapes=[pltpu.VMEM((2 * F_WIDTH // LANES, tm, LANES), F32)],
        compiler_params=_params("parallel"),
        name="in_projection",
    )(x2d, g_mix.reshape(1, -1), win, g_q.reshape(1, -1), wq, g_kv.reshape(1, -1), wkv, ab, cos, sin, ec, es, hsum)


def _seq_dft_kernel(v_ref, m_ref, y_ref):
    sign = jnp.where(pl.program_id(1) == 0, 1.0, -1.0)
    parts = []
    for q in range(2):
        ab = (v_ref[q, 0].astype(F32) + sign * v_ref[q, 1].astype(F32)).astype(BF16)
        parts.append(jnp.dot(m_ref[q, :, :QUARTER], ab[:, :F_WIDTH], preferred_element_type=F32)
                     + jnp.dot(m_ref[q, :, QUARTER:], ab[:, F_WIDTH:], preferred_element_type=F32))
    y_ref[0] = (parts[0] + parts[1]).astype(BF16)
    y_ref[1] = (parts[0] - parts[1]).astype(BF16)


def _seq_dft_mats():
    i = np.arange(QUARTER)
    out = np.zeros((2, 2, QUARTER, 2 * QUARTER), np.float32)
    for p in range(2):
        for q in range(2):
            prod = np.outer(2 * i + p, 2 * i + q) % SEQ
            ang = 2.0 * np.pi * prod / SEQ
            out[p, q, :, :QUARTER] = np.cos(ang) / np.sqrt(SEQ)
            out[p, q, :, QUARTER:] = -np.sin(ang) / np.sqrt(SEQ)
    return out


def _sequence_dft(v):
    mats = jnp.asarray(_seq_dft_mats()).astype(BF16)
    return pl.pallas_call(
        _seq_dft_kernel,
        grid=(BATCH, 2),
        in_specs=[pl.BlockSpec((None, 2, 2, QUARTER, 2 * F_WIDTH), lambda b, p: (b, 0, 0, 0, 0)),
                  pl.BlockSpec((None, 2, QUARTER, 2 * QUARTER), lambda b, p: (p, 0, 0, 0))],
        out_specs=pl.BlockSpec((None, None, 2, QUARTER, F_WIDTH), lambda b, p: (b, p, 0, 0, 0)),
        out_shape=jax.ShapeDtypeStruct((BATCH, 2, 2, QUARTER, F_WIDTH), BF16),
        compiler_params=_params("parallel", "arbitrary"),
        name="sequence_dft",
    )(v, mats)


def _mla_kernel(q_ref, k_ref, v_ref, sq_ref, o_ref):
    v = v_ref[...]
    hp = pl.program_id(1)
    sq = jnp.max(sq_ref[...], axis=0)
    bounds = jnp.sqrt(sq[0:1] * sq[1:2]) * NORM_SLACK
    head_lane = lax.broadcasted_iota(I32, bounds.shape, 1)
    head_bounds = [jnp.max(jnp.where(head_lane == 2 * hp + j, bounds, 0.0), axis=1, keepdims=True)
                   for j in range(2)]
    safe = jnp.max(jnp.maximum(head_bounds[0], head_bounds[1])) <= SAFE_SHIFT

    def attend(j, row_shift):
        cols = slice(HEAD_PAD * j, HEAD_PAD * (j + 1))
        out_cols = slice(V_HEAD_DIM * j, V_HEAD_DIM * (j + 1))
        s = lax.dot_general(q_ref[:, cols], k_ref[:, cols], (((1,), (1,)), ((), ())),
                            preferred_element_type=F32)
        p = jnp.exp(s - row_shift(s))
        l = jnp.sum(p, axis=1, keepdims=True)
        o = jnp.dot(p.astype(BF16), v, preferred_element_type=F32) / l
        o_ref[:, out_cols] = o[:, out_cols].astype(BF16)

    @pl.when(safe)
    def _():
        for j in range(2):
            attend(j, lambda s: head_bounds[j])

    @pl.when(jnp.logical_not(safe))
    def _():
        for j in range(2):
            attend(j, lambda s: jnp.max(s, axis=1, keepdims=True))


def _mla_attention(q, k, v, ksq):
    q = q.reshape(BATCH, SEQ, -1)
    k = k.reshape(BATCH, SEQ, -1)
    v = v.reshape(BATCH, SEQ, -1)
    ksq = ksq.reshape(BATCH, SEQ // TM_IN, 2, LANES)
    out = pl.pallas_call(
        _mla_kernel,
        grid=(BATCH, MLA_HEADS // 2, SEQ // TQ),
        in_specs=[pl.BlockSpec((None, TQ, 2 * HEAD_PAD), lambda b, hp, i: (b, i, hp)),
                  pl.BlockSpec((None, SEQ, 2 * HEAD_PAD), lambda b, hp, i: (b, 0, hp)),
                  pl.BlockSpec((None, SEQ, 2 * V_HEAD_DIM), lambda b, hp, i: (b, 0, hp)),
                  pl.BlockSpec((None, SEQ // TM_IN, 2, LANES), lambda b, hp, i: (b, 0, 0, 0))],
        out_specs=pl.BlockSpec((None, TQ, 2 * V_HEAD_DIM), lambda b, hp, i: (b, i, hp)),
        out_shape=jax.ShapeDtypeStruct((BATCH, SEQ, MLA_HEADS * V_HEAD_DIM), BF16),
        compiler_params=_params("parallel", "parallel", "arbitrary"),
        name="mla_attention",
    )(q, k, v, ksq)
    return out.reshape(TOKENS, MLA_HEADS * V_HEAD_DIM)


def _memkv_kernel(mem_ref, g_ref, w_ref, k_ref, v_ref):
    mn = _rms(mem_ref[...], g_ref[...]).astype(BF16)
    kv = jnp.dot(mn, w_ref[...], preferred_element_type=F32)
    k_ref[...] = kv[:, :D_MODEL].astype(BF16)
    v_ref[...] = kv[:, D_MODEL:].astype(BF16)


def _memory_kv(mem, g_mem_kv, w_mem_kv):
    blk = pl.BlockSpec((None, MEM_LEN, D_MODEL), lambda b: (b, 0, 0))
    return pl.pallas_call(
        _memkv_kernel,
        grid=(BATCH,),
        in_specs=[blk, pl.BlockSpec((1, D_MODEL), lambda b: (0, 0)),
                  pl.BlockSpec((D_MODEL, 2 * D_MODEL), lambda b: (0, 0))],
        out_specs=[blk, blk],
        out_shape=[jax.ShapeDtypeStruct((BATCH, MEM_LEN, D_MODEL), BF16)] * 2,
        compiler_params=_params("parallel"),
        name="memory_kv",
    )(mem, g_mem_kv.reshape(1, -1), w_mem_kv.astype(BF16))


def _mix_kernel(x_ref, yf_ref, ya_ref, wo_ref, gq_ref, wmq_ref, mk_ref, mv_ref, wmo_ref, gf_ref, wr_ref,
                x2_ref, hext_ref, aff_ref, zscr_ref):
    tm = x_ref.shape[0]
    rows_per_group = tm // MIX_GROUPS
    for grp in range(MIX_GROUPS):
        rows = pl.ds(grp * rows_per_group, rows_per_group)
        half = pl.ds(grp * rows_per_group // 2, rows_per_group // 2)
        wo_f = wo_ref[:F_WIDTH, :]
        z_even = jnp.dot(yf_ref[0, half, :], wo_f, preferred_element_type=F32)
        z_odd = jnp.dot(yf_ref[1, half, :], wo_f, preferred_element_type=F32)
        for c in range(zscr_ref.shape[0]):
            cols = slice(LANES * c, LANES * (c + 1))
            zscr_ref[c, pl.ds(grp * rows_per_group, rows_per_group // 2, stride=2), :] = z_even[:, cols]
            zscr_ref[c, pl.ds(grp * rows_per_group + 1, rows_per_group // 2, stride=2), :] = z_odd[:, cols]
        z = jnp.concatenate([zscr_ref[c, rows, :] for c in range(zscr_ref.shape[0])], axis=1)
        x1 = x_ref[rows, :] + z + jnp.dot(ya_ref[rows, :], wo_ref[F_WIDTH:, :], preferred_element_type=F32)

        hq = _rms(x1, gq_ref[...]).astype(BF16)
        qm = (jnp.dot(hq, wmq_ref[...], preferred_element_type=F32) * (MEM_HEAD_DIM ** -0.5)).astype(BF16)
        heads = []
        for hd in range(MEM_HEADS):
            sl = slice(MEM_HEAD_DIM * hd, MEM_HEAD_DIM * (hd + 1))
            s = lax.dot_general(qm[:, sl], mk_ref[:, sl], (((1,), (1,)), ((), ())),
                                preferred_element_type=F32)
            p = jnp.exp(s - jnp.max(s, axis=1, keepdims=True))
            l = jnp.sum(p, axis=1, keepdims=True)
            heads.append((jnp.dot(p.astype(BF16), mv_ref[:, sl], preferred_element_type=F32) / l).astype(BF16))
        o = jnp.concatenate(heads, axis=1)
        x2 = x1 + jnp.dot(o, wmo_ref[...], preferred_element_type=F32)
        x2_ref[rows, :] = x2

        h3 = _rms(x2, gf_ref[...])
        h3_hi = h3.astype(BF16)
        hext_ref[rows, :D_MODEL] = h3_hi
        h3_lo = (h3 - h3_hi.astype(F32)).astype(BF16)
        hi_terms = jnp.dot(h3_hi, wr_ref[...], preferred_element_type=F32)
        logits = (hi_terms[:, :LANES] + hi_terms[:, LANES:]
                  + jnp.dot(h3_lo, wr_ref[:, :LANES], preferred_element_type=F32))
        lane = lax.broadcasted_iota(I32, logits.shape, 1)
        logits = jnp.where(lane < N_EXPERTS, logits, -jnp.inf)
        e = jnp.exp(logits - jnp.max(logits, axis=1, keepdims=True))
        aff = e / jnp.sum(e, axis=1, keepdims=True)
        aff_ref[rows, :] = aff
        hi = aff.astype(BF16)
        r1 = aff - hi.astype(F32)
        mid = r1.astype(BF16)
        lo = (r1 - mid.astype(F32)).astype(BF16)
        hext_ref[rows, D_MODEL:] = jnp.where(
            lane < N_EXPERTS, hi,
            jnp.where(lane < 2 * N_EXPERTS, pltpu.roll(mid.astype(F32), N_EXPERTS, 1).astype(BF16),
                      pltpu.roll(lo.astype(F32), 2 * N_EXPERTS, 1).astype(BF16)))


def _mixing(x2d, y_f, y_a, w_out, g_mem_q, w_mem_q, mk, mv, w_mem_o, g_ffn, w_router):
    tm = TM_MIX
    wr = jnp.concatenate([w_router, jnp.zeros((D_MODEL, LANES - N_EXPERTS), F32)], axis=1)
    wr_hi = wr.astype(BF16)
    wr_cat = jnp.concatenate([wr_hi, (wr - wr_hi.astype(F32)).astype(BF16)], axis=1)
    full = lambda shape: pl.BlockSpec(shape, lambda i: (0,) * len(shape))
    tile = lambda w: pl.BlockSpec((tm, w), lambda i: (i, 0))
    per_half = HALF // tm
    per_seq = SEQ // tm
    per_batch = pl.BlockSpec((None, MEM_LEN, D_MODEL), lambda i: (i // per_seq, 0, 0))
    yf_spec = pl.BlockSpec((None, 2, None, tm // 2, F_WIDTH),
                           lambda i: (i // per_seq, 0, (i % per_seq) // per_half, i % per_half, 0))
    return pl.pallas_call(
        _mix_kernel,
        grid=(TOKENS // tm,),
        in_specs=[tile(D_MODEL), yf_spec, tile(F_WIDTH), full((D_MODEL, D_MODEL)), full((1, D_MODEL)),
                  full((D_MODEL, D_MODEL)), per_batch, per_batch, full((D_MODEL, D_MODEL)), full((1, D_MODEL)),
                  full((D_MODEL, 2 * LANES))],
        out_specs=[tile(D_MODEL), tile(D_MODEL + LANES), tile(LANES)],
        out_shape=[jax.ShapeDtypeStruct((TOKENS, D_MODEL), F32),
                   jax.ShapeDtypeStruct((TOKENS, D_MODEL + LANES), BF16),
                   jax.ShapeDtypeStruct((TOKENS, LANES), F32)],
        scratch_shapes=[pltpu.VMEM((D_MODEL // LANES, tm, LANES), F32)],
        compiler_params=_params("parallel"),
        name="mix_memattn_router",
    )(x2d, y_f, y_a, w_out.astype(BF16), g_mem_q.reshape(1, -1), w_mem_q.astype(BF16), mk, mv,
      w_mem_o.astype(BF16), g_ffn.reshape(1, -1), wr_cat)


def _topk_kernel(aff_ref, slot_ref, offs_ref):
    aff = aff_ref[...]
    rows = aff.shape[0]

    thr = jnp.zeros((rows, 1), I32)
    for bit in range(30, -1, -1):
        cand = thr | (1 << bit)
        cnt = jnp.sum(jnp.where(aff >= pltpu.bitcast(cand, F32), 1.0, 0.0), axis=1, keepdims=True)
        thr = jnp.where(cnt >= CAPACITY, cand, thr)
    thr_f = pltpu.bitcast(thr, F32)

    chunk = 256
    r = lax.broadcasted_iota(I32, (chunk, chunk), 0)
    c = lax.broadcasted_iota(I32, (chunk, chunk), 1)
    tri = jnp.where(r < c, 1.0, 0.0).astype(BF16)

    def exclusive_count(mask):
        off = jnp.zeros((rows, 1), F32)
        outs = []
        for j in range(SEQ // chunk):
            mj = mask[:, chunk * j:chunk * (j + 1)]
            outs.append(jnp.dot(mj.astype(BF16), tri, preferred_element_type=F32) + off)
            off = off + jnp.sum(mj, axis=1, keepdims=True)
        return jnp.concatenate(outs, axis=1), off

    gt = aff > thr_f
    tie = jnp.where(aff == thr_f, 1.0, 0.0)
    n_gt = jnp.sum(jnp.where(gt, 1.0, 0.0), axis=1, keepdims=True)
    tie_rank, _ = exclusive_count(tie)
    sel = jnp.where(gt | ((tie > 0.0) & (tie_rank < CAPACITY - n_gt)), 1.0, 0.0)
    slot, _ = exclusive_count(sel)
    slot_ref[...] = jnp.where(sel > 0.0, slot.astype(I32), -1)
    tok = lax.broadcasted_iota(I32, (SEQ, LANES), 0)
    j = lax.broadcasted_iota(I32, (SEQ, LANES), 1)
    before = jnp.where(tok < j * TOKEN_CHUNK, 1.0, 0.0).astype(BF16)
    offs_ref[...] = jnp.dot(sel.astype(BF16), before, preferred_element_type=F32).astype(I32)


def _expert_slots(aff):
    aff_t = aff[:, :N_EXPERTS].reshape(BATCH, SEQ, N_EXPERTS).transpose(0, 2, 1).reshape(BATCH * N_EXPERTS, SEQ)
    slots, offs = pl.pallas_call(
        _topk_kernel,
        out_shape=[jax.ShapeDtypeStruct((BATCH * N_EXPERTS, SEQ), I32),
                   jax.ShapeDtypeStruct((BATCH * N_EXPERTS, LANES), I32)],
        compiler_params=pltpu.CompilerParams(vmem_limit_bytes=VMEM_LIMIT),
        name="expert_topk",
    )(aff_t)
    return slots, offs[:, :OFFS_STRIDE].reshape(-1)


def _window_start(first, win):
    start = jnp.minimum((first >> SLOT_ALIGN_SHIFT) << SLOT_ALIGN_SHIFT, CAPACITY - win)
    return pl.multiple_of(start, 1 << SLOT_ALIGN_SHIFT)


def _gather_kernel(offs_ref, slot_ref, h_ref, x_ref):
    win = GATHER_WIN
    last = CAPACITY - win
    b = pl.program_id(0)
    e0 = pl.program_id(1) * GATHER_EXPERTS
    j = pl.program_id(2)

    @pl.when(j == 0)
    def _():
        x_ref[...] = jnp.zeros_like(x_ref)

    def bounds(e):
        base = (b * N_EXPERTS + e0 + e) * OFFS_STRIDE + j
        return offs_ref[base], offs_ref[base + 1]

    row = lax.broadcasted_iota(I32, (win, TOKEN_CHUNK), 0)
    h_c = h_ref[...]
    starts = [_window_start(bounds(e)[0], win) for e in range(GATHER_EXPERTS)]
    onehot = jnp.concatenate(
        [jnp.where(row + starts[e] == slot_ref[e:e + 1, :], 1.0, 0.0).astype(BF16)
         for e in range(GATHER_EXPERTS)], axis=0)
    picked = jnp.dot(onehot, h_c, preferred_element_type=F32).astype(BF16)
    for e in range(GATHER_EXPERTS):
        rows = pl.ds(starts[e], win)
        x_ref[e, rows, :] = x_ref[e, rows, :] + picked[e * win:(e + 1) * win]

    for e in range(GATHER_EXPERTS):
        _, end = bounds(e)
        covered = starts[e] + win
        slot_e = slot_ref[e:e + 1, :]

        def extra_window(i, carry):
            lo = covered + i * win
            r0 = pl.multiple_of(jnp.minimum(lo, last), 1 << SLOT_ALIGN_SHIFT)
            hot = jnp.where((row + r0 == slot_e) & (slot_e >= lo), 1.0, 0.0).astype(BF16)
            rows = pl.ds(r0, win)
            x_ref[e, rows, :] = x_ref[e, rows, :] + jnp.dot(hot, h_c, preferred_element_type=F32).astype(BF16)
            return carry

        lax.fori_loop(0, jnp.maximum(end - covered + win - 1, 0) // win, extra_window, 0)


def _gather(offs, slots, h_ext):
    slots = slots.reshape(BATCH, N_EXPERTS, SEQ)
    h_ext = h_ext.reshape(BATCH, SEQ, D_MODEL + LANES)
    return pl.pallas_call(
        _gather_kernel,
        grid_spec=pltpu.PrefetchScalarGridSpec(
            num_scalar_prefetch=1,
            grid=(BATCH, N_EXPERTS // GATHER_EXPERTS, SEQ // TOKEN_CHUNK),
            in_specs=[pl.BlockSpec((None, GATHER_EXPERTS, TOKEN_CHUNK), lambda b, g, j, offs: (b, g, j)),
                      pl.BlockSpec((None, TOKEN_CHUNK, D_MODEL + LANES), lambda b, g, j, offs: (b, j, 0))],
            out_specs=pl.BlockSpec((None, GATHER_EXPERTS, None, CAPACITY, D_MODEL + LANES),
                                   lambda b, g, j, offs: (b // FFN_SEQS, g, b % FFN_SEQS, 0, 0))),
        out_shape=jax.ShapeDtypeStruct((BATCH // FFN_SEQS, N_EXPERTS, FFN_SEQS, CAPACITY, D_MODEL + LANES),
                                       BF16),
        compiler_params=_params("parallel", "parallel", "arbitrary"),
        name="expert_gather",
    )(offs, slots, h_ext)


def _expert_kernel(x_ref, wg_ref, wu_ref, wd_ref, y_ref, wg_s, wu_s, wd_s):
    g = pl.program_id(0)
    part = pl.program_id(1)
    slab = wg_ref.shape[0]

    @pl.when(g < N_EXPERTS)
    def _():
        rows = pl.ds(pl.multiple_of(part * slab, slab), slab)
        wg_s[g % 2, rows, :] = wg_ref[...].astype(BF16)
        wu_s[g % 2, rows, :] = wu_ref[...].astype(BF16)
        wd_s[g % 2, rows, :] = wd_ref[...].astype(BF16)

    @pl.when(g == 0)
    def _():
        y_ref[...] = jnp.zeros_like(y_ref)

    @pl.when(g > 0)
    def _():
        e = g - 1
        cur = e % 2
        rows = x_ref.shape[0] * x_ref.shape[1]
        xin = x_ref[:, :, :D_MODEL].reshape(rows, D_MODEL)
        ext = x_ref[:, :, D_MODEL:].reshape(rows, LANES).astype(F32)
        lane = lax.broadcasted_iota(I32, ext.shape, 1)
        mine = (lane == e) | (lane == e + N_EXPERTS) | (lane == e + 2 * N_EXPERTS)
        gate = jnp.sum(jnp.where(mine, ext, 0.0), axis=1, keepdims=True)

        a = jnp.dot(xin, wg_s[cur], preferred_element_type=F32)
        b = jnp.dot(xin, wu_s[cur], preferred_element_type=F32)
        hid = (a / (1.0 + jnp.exp(-a)) * b).astype(BF16)
        y = (jnp.dot(hid, wd_s[cur], preferred_element_type=F32) * gate).astype(BF16)
        y_ref[...] = y.reshape(y_ref.shape)


def _experts(xin, w_gate, w_up, w_down):
    parts = BATCH // FFN_SEQS
    slab = D_MODEL // parts
    prev = lambda g: jnp.maximum(g - 1, 0)
    wspec = pl.BlockSpec((None, slab, D_MODEL), lambda g, p: (jnp.minimum(g, N_EXPERTS - 1), p, 0))
    return pl.pallas_call(
        _expert_kernel,
        grid=(N_EXPERTS + 1, parts),
        in_specs=[pl.BlockSpec((None, None, FFN_SEQS, CAPACITY, D_MODEL + LANES),
                               lambda g, p: (p, prev(g), 0, 0, 0)),
                  wspec, wspec, wspec],
        out_specs=pl.BlockSpec((None, None, FFN_SEQS, CAPACITY, D_MODEL),
                               lambda g, p: (p, jnp.where(g == 0, N_EXPERTS, g - 1), 0, 0, 0)),
        out_shape=jax.ShapeDtypeStruct((parts, N_EXPERTS + 1, FFN_SEQS, CAPACITY, D_MODEL), BF16),
        scratch_shapes=[pltpu.VMEM((2, D_MODEL, D_MODEL), BF16)] * 3,
        compiler_params=_params("arbitrary", "arbitrary"),
        name="expert_ffn",
    )(xin, w_gate, w_up, w_down)


def _combine_kernel(offs_ref, x2_ref, slot_ref, y_ref, g_ref, o_ref):
    b = pl.program_id(0)
    j = pl.program_id(1)
    win = SCATTER_WIN
    lane = lax.broadcasted_iota(I32, (TOKEN_CHUNK, win), 1)
    slot = slot_ref[...]

    def bounds(e):
        base = (b * N_EXPERTS + e) * OFFS_STRIDE + j
        return offs_ref[base], offs_ref[base + 1]

    starts = [_window_start(bounds(e)[0], win) for e in range(N_EXPERTS)]
    onehot = jnp.concatenate(
        [jnp.where(lane + starts[e] == slot[:, e:e + 1], 1.0, 0.0).astype(BF16) for e in range(N_EXPERTS)],
        axis=1)
    rows = jnp.concatenate([y_ref[e, pl.ds(starts[e], win), :] for e in range(N_EXPERTS)], axis=0)
    x3 = x2_ref[...] + jnp.dot(onehot, rows, preferred_element_type=F32)

    leftover = [bounds(e)[1] > starts[e] + win for e in range(N_EXPERTS)]
    any_leftover = functools.reduce(jnp.logical_or, leftover)

    @pl.when(jnp.logical_not(any_leftover))
    def _():
        o_ref[...] = _rms(x3, g_ref[...])

    @pl.when(any_leftover)
    def _():
        o_ref[...] = x3
        tail_lane = lax.broadcasted_iota(I32, (TOKEN_CHUNK, CAPACITY - win), 1) + win
        for e in range(N_EXPERTS):
            @pl.when(leftover[e])
            def _():
                sl = slot[:, e:e + 1]
                hot = jnp.where((tail_lane == sl) & (sl >= starts[e] + win), 1.0, 0.0).astype(BF16)
                o_ref[...] += jnp.dot(hot, y_ref[e, win:, :], preferred_element_type=F32)

        o_ref[...] = _rms(o_ref[...], g_ref[...])


def _combine(offs, x2, slots, y, g_final):
    slots_t = slots.reshape(BATCH, N_EXPERTS, SEQ).transpose(0, 2, 1)
    x2 = x2.reshape(BATCH, SEQ, D_MODEL)
    tm = TOKEN_CHUNK
    return pl.pallas_call(
        _combine_kernel,
        grid_spec=pltpu.PrefetchScalarGridSpec(
            num_scalar_prefetch=1,
            grid=(BATCH, SEQ // tm),
            in_specs=[pl.BlockSpec((None, tm, D_MODEL), lambda b, i, offs: (b, i, 0)),
                      pl.BlockSpec((None, tm, N_EXPERTS), lambda b, i, offs: (b, i, 0)),
                      pl.BlockSpec((None, N_EXPERTS, None, CAPACITY, D_MODEL),
                                   lambda b, i, offs: (b // FFN_SEQS, 0, b % FFN_SEQS, 0, 0)),
                      pl.BlockSpec((1, D_MODEL), lambda b, i, offs: (0, 0))],
            out_specs=pl.BlockSpec((None, tm, D_MODEL), lambda b, i, offs: (b, i, 0))),
        out_shape=jax.ShapeDtypeStruct((BATCH, SEQ, D_MODEL), F32),
        compiler_params=_params("parallel", "arbitrary"),
        name="combine_final_norm",
    )(offs, x2, slots_t, y, g_final.reshape(1, -1))


def kernel(x, mem, positions, g_mix, w_in, g_q_lat, w_q_up, g_kv_lat, w_kv_up, w_fourier, w_out, g_mem_q,
           g_mem_kv, w_mem_q, w_mem_kv, w_mem_o, g_ffn, w_router, w_exp_gate, w_exp_up, w_exp_down, g_final):
    assert x.shape == (BATCH, SEQ, D_MODEL) and g_mix.shape[0] == 1
    x2d = x.reshape(TOKENS, D_MODEL)
    cos, sin = _rope_tables(positions)
    ab = _channel_mats(w_fourier[0])
    v12, q, k, v, ksq = _in_projection(x2d, g_mix[0], w_in[0], g_q_lat[0], w_q_up[0], g_kv_lat[0], w_kv_up[0],
                                       ab, cos, sin)
    y_f = _sequence_dft(v12)
    y_a = _mla_attention(q, k, v, ksq)
    mk, mv = _memory_kv(mem, g_mem_kv[0], w_mem_kv[0])
    x2, h_ext, aff = _mixing(x2d, y_f, y_a, w_out[0], g_mem_q[0], w_mem_q[0], mk, mv, w_mem_o[0], g_ffn[0],
                             w_router[0])
    slots, offs = _expert_slots(aff)
    xin = _gather(offs, slots, h_ext)
    y = _experts(xin, w_exp_gate[0], w_exp_up[0], w_exp_down[0])
    return _combine(offs, x2, slots, y, g_final)
```

```python
import functools

import numpy as np
import jax
import jax.numpy as jnp
from jax import lax
from jax.experimental import pallas as pl
from jax.experimental.pallas import tpu as pltpu

F32 = jnp.float32
BF16 = jnp.bfloat16
I32 = jnp.int32

D_MODEL = 1024
BATCH = 4
SEQ = 4096
TOKENS = BATCH * SEQ
MEM_LEN = 256
RMS_EPS = 1e-6
F_GROUPS = 8
F_GROUP_DIM = 64
F_WIDTH = F_GROUPS * F_GROUP_DIM
MLA_HEADS = 8
QK_NOPE_DIM = 64
QK_ROPE_DIM = 32
V_HEAD_DIM = 64
Q_LORA_RANK = 384
KV_LORA_RANK = 256
ROPE_THETA = 10000.0
MEM_HEADS = 4
MEM_HEAD_DIM = D_MODEL // MEM_HEADS
N_EXPERTS = 16
CAPACITY = 2 * SEQ // N_EXPERTS

NORM_SLACK = 1.01
SAFE_SHIFT = 30.0

LANES = 128
HEAD_PAD = 128
IN_PAD = 1280
HALF = SEQ // 2
QUARTER = SEQ // 4
VMEM_LIMIT = 56 * 1024 * 1024

TM_IN = 1024
IN_GROUPS = 1
TM_MIX = 1024
MIX_GROUPS = 2
TQ = 1024
TOKEN_CHUNK = 512
OFFS_STRIDE = SEQ // TOKEN_CHUNK + 1
GATHER_WIN = 96
SCATTER_WIN = 128
SLOT_ALIGN_SHIFT = 4
GATHER_EXPERTS = 16
FFN_SEQS = 2


def _rms(x, g):
    return x * lax.rsqrt(jnp.mean(x * x, axis=-1, keepdims=True) + RMS_EPS) * g


def _params(*sem):
    return pltpu.CompilerParams(dimension_semantics=sem, vmem_limit_bytes=VMEM_LIMIT)


def _rope_kernel(pos_ref, freq_ref, cos_ref, sin_ref):
    ang = pos_ref[...] * freq_ref[...]
    cos_ref[...] = jnp.cos(ang)
    sin_ref[...] = jnp.sin(ang)


def _rope_tables(positions):
    half = QK_ROPE_DIM // 2
    freqs = 1.0 / (ROPE_THETA ** (jnp.arange(0, QK_ROPE_DIM, 2, dtype=F32) / QK_ROPE_DIM))
    rows = TOKENS * half // LANES
    pos = jnp.repeat(positions.astype(F32).reshape(-1), half).reshape(rows, LANES)
    freq = jnp.tile(freqs, LANES // half).reshape(1, LANES)
    cos, sin = pl.pallas_call(
        _rope_kernel,
        out_shape=(jax.ShapeDtypeStruct((rows, LANES), F32),) * 2,
        name="rope_tables",
    )(pos, freq)
    cs = jnp.concatenate([cos.reshape(TOKENS, half), sin.reshape(TOKENS, half)], axis=1)
    hi = cs.astype(BF16)
    lo = (cs - hi.astype(F32)).astype(BF16)
    return jnp.concatenate([hi, lo], axis=1)


def _rope_expander():
    half = QK_ROPE_DIM // 2
    lane = np.arange(LANES)
    hit = (lane[None, :] % half) == np.arange(half)[:, None]
    is_sin = (lane // QK_ROPE_DIM) % 2 == 1
    block = np.concatenate([hit & ~is_sin[None, :], hit & is_sin[None, :]], axis=0)
    return jnp.asarray(np.concatenate([block, block], axis=0).astype(np.float32)).astype(BF16)


def _chan_kernel(cbd_ref, sbd_ref, w_ref, ab_ref):
    w = w_ref[...]
    ab_ref[:, :F_WIDTH] = jnp.dot(cbd_ref[...], w, precision=lax.Precision.HIGHEST,
                                  preferred_element_type=F32).astype(BF16)
    ab_ref[:, F_WIDTH:] = jnp.dot(sbd_ref[...], w, precision=lax.Precision.HIGHEST,
                                  preferred_element_type=F32).astype(BF16)


def _channel_mats(w_fourier):
    c = np.arange(F_GROUP_DIM)
    ang = 2.0 * np.pi * np.outer(c, c) / F_GROUP_DIM
    scale = F_GROUP_DIM ** -0.5
    eye = np.eye(F_GROUPS)
    cbd = np.kron(eye, np.cos(ang) * scale).astype(np.float32)
    sbd = np.kron(eye, np.sin(ang) * scale).astype(np.float32)
    wbd = (jnp.eye(F_GROUPS, dtype=F32)[:, None, :, None] * w_fourier[:, :, None, :]).reshape(F_WIDTH, F_WIDTH)
    return pl.pallas_call(
        _chan_kernel,
        out_shape=jax.ShapeDtypeStruct((F_WIDTH, 2 * F_WIDTH), BF16),
        name="channel_dft_fold",
    )(jnp.asarray(cbd), jnp.asarray(sbd), wbd)


def _inproj_kernel(x_ref, gmix_ref, win_ref, gq_ref, wq_ref, gkv_ref, wkv_ref, ab_ref, cs_ref, expand_ref,
                   hsum_ref, v12_ref, q_ref, k_ref, v_ref, ksq_ref, vscr_ref):
    rows_per_group = x_ref.shape[0] // IN_GROUPS
    half_rows = rows_per_group // 2
    ksq_max = None
    for grp in range(IN_GROUPS):
        r0 = grp * rows_per_group
        rows = pl.ds(r0, rows_per_group)
        half = pl.ds(r0 // 2, half_rows)
        h = _rms(x_ref[rows, :], gmix_ref[...]).astype(BF16)
        u = jnp.dot(h, win_ref[...], preferred_element_type=F32)

        n_pairs = ab_ref.shape[0]
        for gp in range(n_pairs):
            pair = jnp.dot(u[:, LANES * gp:LANES * (gp + 1)].astype(BF16), ab_ref[gp],
                           preferred_element_type=F32)
            for part in range(2):
                c = part * n_pairs + gp
                cols = slice(LANES * c, LANES * (c + 1))
                vscr_ref[c, rows, :] = pair[:, LANES * part:LANES * (part + 1)]
                v12_ref[0, half, cols] = vscr_ref[c, pl.ds(r0, half_rows, stride=2), :].astype(BF16)
                v12_ref[1, half, cols] = vscr_ref[c, pl.ds(r0 + 1, half_rows, stride=2), :].astype(BF16)

        tab = jnp.dot(cs_ref[rows, :], expand_ref[...], preferred_element_type=F32)
        lane = lax.broadcasted_iota(I32, tab.shape, 1)
        rope0, rope1 = QK_NOPE_DIM, QK_NOPE_DIM + QK_ROPE_DIM
        back = LANES - QK_ROPE_DIM
        c1 = jnp.where(lane < rope0, 1.0, jnp.where(lane < rope1, tab, 0.0))
        c2 = jnp.where(lane >= rope1, tab, 0.0)

        q0 = F_WIDTH
        qn = _rms(u[:, q0:q0 + Q_LORA_RANK], gq_ref[...]).astype(BF16)
        qa = jnp.dot(qn, wq_ref[...], preferred_element_type=F32)
        q_blocks = []
        for hd in range(MLA_HEADS):
            blk = qa[:, HEAD_PAD * hd:HEAD_PAD * (hd + 1)]
            q_blocks.append(blk * c1 + pltpu.roll(blk * c2, back, 1))
            q_ref[rows, HEAD_PAD * hd:HEAD_PAD * (hd + 1)] = q_blocks[hd].astype(BF16)

        kv0 = q0 + Q_LORA_RANK
        kvn = _rms(u[:, kv0:kv0 + KV_LORA_RANK], gkv_ref[...]).astype(BF16)
        kv = jnp.dot(kvn, wkv_ref[...], preferred_element_type=F32)
        kr0 = kv0 + KV_LORA_RANK
        t = u[:, kr0:kr0 + LANES] * jnp.where(lane < 2 * QK_ROPE_DIM, tab, 0.0)
        kr = t + pltpu.roll(t, back, 1)
        kr = jnp.where((lane >= rope0) & (lane < rope1), pltpu.roll(kr, rope0, 1), 0.0)
        k_blocks = [kv[:, HEAD_PAD * hd:HEAD_PAD * (hd + 1)] + kr for hd in range(MLA_HEADS)]
        for hd in range(MLA_HEADS):
            k_ref[rows, HEAD_PAD * hd:HEAD_PAD * (hd + 1)] = k_blocks[hd].astype(BF16)
        v_ref[rows, :] = kv[:, MLA_HEADS * HEAD_PAD:].astype(BF16)
        sq = jnp.concatenate(
            [jnp.concatenate([(blk * blk).astype(BF16) for blk in blocks], axis=1)
             for blocks in (k_blocks, q_blocks)], axis=0)
        sq = jnp.dot(sq, hsum_ref[...], preferred_element_type=F32)
        grp_max = jnp.concatenate([jnp.max(sq[:rows_per_group], axis=0, keepdims=True),
                                   jnp.max(sq[rows_per_group:], axis=0, keepdims=True)], axis=0)
        ksq_max = grp_max if ksq_max is None else jnp.maximum(ksq_max, grp_max)
    ksq_ref[...] = ksq_max


def _in_projection(x2d, g_mix, w_in, g_q, w_q_up, g_kv, w_kv_up, ab, rope_cs):
    w_kr = w_in[:, -QK_ROPE_DIM:]
    half = QK_ROPE_DIM // 2
    w_kr_rot = jnp.concatenate([-w_kr[:, half:], w_kr[:, :half]], axis=1)
    win = jnp.concatenate(
        [w_in, w_kr_rot, jnp.zeros((D_MODEL, IN_PAD - w_in.shape[1] - QK_ROPE_DIM), F32)], axis=1).astype(BF16)
    scale = (QK_NOPE_DIM + QK_ROPE_DIM) ** -0.5
    wq = w_q_up.reshape(Q_LORA_RANK, MLA_HEADS, QK_NOPE_DIM + QK_ROPE_DIM) * scale
    wq_rope = wq[:, :, QK_NOPE_DIM:]
    wq_rot = jnp.concatenate([-wq_rope[:, :, half:], wq_rope[:, :, :half]], axis=2)
    wq = jnp.concatenate([wq, wq_rot], axis=2).reshape(Q_LORA_RANK, MLA_HEADS * HEAD_PAD).astype(BF16)
    wkv = w_kv_up.reshape(KV_LORA_RANK, MLA_HEADS, QK_NOPE_DIM + V_HEAD_DIM)
    wk = jnp.concatenate([wkv[:, :, :QK_NOPE_DIM],
                          jnp.zeros((KV_LORA_RANK, MLA_HEADS, HEAD_PAD - QK_NOPE_DIM), F32)], axis=2)
    wkv = jnp.concatenate([wk.reshape(KV_LORA_RANK, MLA_HEADS * HEAD_PAD),
                           wkv[:, :, QK_NOPE_DIM:].reshape(KV_LORA_RANK, MLA_HEADS * V_HEAD_DIM)],
                          axis=1).astype(BF16)

    ab = jnp.stack([jnp.concatenate([ab[LANES * g:LANES * (g + 1), LANES * g:LANES * (g + 1)],
                                     ab[LANES * g:LANES * (g + 1), F_WIDTH + LANES * g:F_WIDTH + LANES * (g + 1)]],
                                    axis=1) for g in range(F_WIDTH // LANES)])
    tm = TM_IN
    expand = _rope_expander()
    hsum = jnp.asarray((np.arange(MLA_HEADS * HEAD_PAD)[:, None] // HEAD_PAD
                        == np.arange(LANES)[None, :]).astype(np.float32)).astype(BF16)
    full = lambda shape: pl.BlockSpec(shape, lambda i: (0,) * len(shape))
    tile = lambda w: pl.BlockSpec((tm, w), lambda i: (i, 0))
    per_half = HALF // tm
    per_seq = SEQ // tm
    v12_spec = pl.BlockSpec((None, 2, None, tm // 2, 2 * F_WIDTH),
                            lambda i: (i // per_seq, 0, (i % per_seq) // per_half, i % per_half, 0))
    return pl.pallas_call(
        _inproj_kernel,
        grid=(TOKENS // tm,),
        in_specs=[tile(D_MODEL), full((1, D_MODEL)), full(win.shape), full((1, Q_LORA_RANK)), full(wq.shape),
                  full((1, KV_LORA_RANK)), full(wkv.shape), full(ab.shape), tile(rope_cs.shape[1]),
                  full(expand.shape), full(hsum.shape)],
        out_specs=[v12_spec, tile(MLA_HEADS * HEAD_PAD), tile(MLA_HEADS * HEAD_PAD),
                   tile(MLA_HEADS * V_HEAD_DIM), pl.BlockSpec((None, 2, LANES), lambda i: (i, 0, 0))],
        out_shape=[jax.ShapeDtypeStruct((BATCH, 2, 2, QUARTER, 2 * F_WIDTH), BF16),
                   jax.ShapeDtypeStruct((TOKENS, MLA_HEADS * HEAD_PAD), BF16),
                   jax.ShapeDtypeStruct((TOKENS, MLA_HEADS * HEAD_PAD), BF16),
                   jax.ShapeDtypeStruct((TOKENS, MLA_HEADS * V_HEAD_DIM), BF16),
                   jax.ShapeDtypeStruct((TOKENS // tm, 2, LANES), F32)],
        scratch_shapes=[pltpu.VMEM((2 * F_WIDTH // LANES, tm, LANES), F32)],
        compiler_params=_params("parallel"),
        name="in_projection",
    )(x2d, g_mix.reshape(1, -1), win, g_q.reshape(1, -1), wq, g_kv.reshape(1, -1), wkv, ab, rope_cs, expand, hsum)


def _seq_dft_kernel(v_ref, m_ref, y_ref):
    sign = jnp.where(pl.program_id(1) == 0, 1.0, -1.0)
    parts = []
    for q in range(2):
        ab = (v_ref[q, 0].astype(F32) + sign * v_ref[q, 1].astype(F32)).astype(BF16)
        parts.append(jnp.dot(m_ref[q, :, :QUARTER], ab[:, :F_WIDTH], preferred_element_type=F32)
                     + jnp.dot(m_ref[q, :, QUARTER:], ab[:, F_WIDTH:], preferred_element_type=F32))
    y_ref[0] = (parts[0] + parts[1]).astype(BF16)
    y_ref[1] = (parts[0] - parts[1]).astype(BF16)


def _seq_dft_mats():
    i = np.arange(QUARTER)
    out = np.zeros((2, 2, QUARTER, 2 * QUARTER), np.float32)
    for p in range(2):
        for q in range(2):
            prod = np.outer(2 * i + p, 2 * i + q) % SEQ
            ang = 2.0 * np.pi * prod / SEQ
            out[p, q, :, :QUARTER] = np.cos(ang) / np.sqrt(SEQ)
            out[p, q, :, QUARTER:] = -np.sin(ang) / np.sqrt(SEQ)
    return out


def _sequence_dft(v):
    mats = jnp.asarray(_seq_dft_mats()).astype(BF16)
    return pl.pallas_call(
        _seq_dft_kernel,
        grid=(BATCH, 2),
        in_specs=[pl.BlockSpec((None, 2, 2, QUARTER, 2 * F_WIDTH), lambda b, p: (b, 0, 0, 0, 0)),
                  pl.BlockSpec((None, 2, QUARTER, 2 * QUARTER), lambda b, p: (p, 0, 0, 0))],
        out_specs=pl.BlockSpec((None, None, 2, QUARTER, F_WIDTH), lambda b, p: (b, p, 0, 0, 0)),
        out_shape=jax.ShapeDtypeStruct((BATCH, 2, 2, QUARTER, F_WIDTH), BF16),
        compiler_params=_params("parallel", "arbitrary"),
        name="sequence_dft",
    )(v, mats)


def _mla_kernel(q_ref, k_ref, v_ref, sq_ref, o_ref):
    v = v_ref[...]
    hp = pl.program_id(1)
    sq = jnp.max(sq_ref[...], axis=0)
    bounds = jnp.sqrt(sq[0:1] * sq[1:2]) * NORM_SLACK
    head_lane = lax.broadcasted_iota(I32, bounds.shape, 1)
    head_bounds = [jnp.max(jnp.where(head_lane == 2 * hp + j, bounds, 0.0), axis=1, keepdims=True)
                   for j in range(2)]
    safe = jnp.max(jnp.maximum(head_bounds[0], head_bounds[1])) <= SAFE_SHIFT

    def attend(j, row_shift):
        cols = slice(HEAD_PAD * j, HEAD_PAD * (j + 1))
        out_cols = slice(V_HEAD_DIM * j, V_HEAD_DIM * (j + 1))
        s = lax.dot_general(q_ref[:, cols], k_ref[:, cols], (((1,), (1,)), ((), ())),
                            preferred_element_type=F32)
        p = jnp.exp(s - row_shift(s))
        l = jnp.sum(p, axis=1, keepdims=True)
        o = jnp.dot(p.astype(BF16), v, preferred_element_type=F32) / l
        o_ref[:, out_cols] = o[:, out_cols].astype(BF16)

    @pl.when(safe)
    def _():
        for j in range(2):
            attend(j, lambda s: head_bounds[j])

    @pl.when(jnp.logical_not(safe))
    def _():
        for j in range(2):
            attend(j, lambda s: jnp.max(s, axis=1, keepdims=True))


def _mla_attention(q, k, v, ksq):
    q = q.reshape(BATCH, SEQ, -1)
    k = k.reshape(BATCH, SEQ, -1)
    v = v.reshape(BATCH, SEQ, -1)
    ksq = ksq.reshape(BATCH, SEQ // TM_IN, 2, LANES)
    out = pl.pallas_call(
        _mla_kernel,
        grid=(BATCH, MLA_HEADS // 2, SEQ // TQ),
        in_specs=[pl.BlockSpec((None, TQ, 2 * HEAD_PAD), lambda b, hp, i: (b, i, hp)),
                  pl.BlockSpec((None, SEQ, 2 * HEAD_PAD), lambda b, hp, i: (b, 0, hp)),
                  pl.BlockSpec((None, SEQ, 2 * V_HEAD_DIM), lambda b, hp, i: (b, 0, hp)),
                  pl.BlockSpec((None, SEQ // TM_IN, 2, LANES), lambda b, hp, i: (b, 0, 0, 0))],
        out_specs=pl.BlockSpec((None, TQ, 2 * V_HEAD_DIM), lambda b, hp, i: (b, i, hp)),
        out_shape=jax.ShapeDtypeStruct((BATCH, SEQ, MLA_HEADS * V_HEAD_DIM), BF16),
        compiler_params=_params("parallel", "parallel", "arbitrary"),
        name="mla_attention",
    )(q, k, v, ksq)
    return out.reshape(TOKENS, MLA_HEADS * V_HEAD_DIM)


def _memkv_kernel(mem_ref, g_ref, w_ref, k_ref, v_ref):
    mn = _rms(mem_ref[...], g_ref[...]).astype(BF16)
    kv = jnp.dot(mn, w_ref[...], preferred_element_type=F32)
    k_ref[...] = kv[:, :D_MODEL].astype(BF16)
    v_ref[...] = kv[:, D_MODEL:].astype(BF16)


def _memory_kv(mem, g_mem_kv, w_mem_kv):
    blk = pl.BlockSpec((None, MEM_LEN, D_MODEL), lambda b: (b, 0, 0))
    return pl.pallas_call(
        _memkv_kernel,
        grid=(BATCH,),
        in_specs=[blk, pl.BlockSpec((1, D_MODEL), lambda b: (0, 0)),
                  pl.BlockSpec((D_MODEL, 2 * D_MODEL), lambda b: (0, 0))],
        out_specs=[blk, blk],
        out_shape=[jax.ShapeDtypeStruct((BATCH, MEM_LEN, D_MODEL), BF16)] * 2,
        compiler_params=_params("parallel"),
        name="memory_kv",
    )(mem, g_mem_kv.reshape(1, -1), w_mem_kv.astype(BF16))


def _mix_kernel(x_ref, yf_ref, ya_ref, wo_ref, gq_ref, wmq_ref, mk_ref, mv_ref, wmo_ref, gf_ref, wr_ref,
                x2_ref, hext_ref, aff_ref, zscr_ref):
    tm = x_ref.shape[0]
    rows_per_group = tm // MIX_GROUPS
    for grp in range(MIX_GROUPS):
        rows = pl.ds(grp * rows_per_group, rows_per_group)
        half = pl.ds(grp * rows_per_group // 2, rows_per_group // 2)
        wo_f = wo_ref[:F_WIDTH, :]
        z_even = jnp.dot(yf_ref[0, half, :], wo_f, preferred_element_type=F32)
        z_odd = jnp.dot(yf_ref[1, half, :], wo_f, preferred_element_type=F32)
        for c in range(zscr_ref.shape[0]):
            cols = slice(LANES * c, LANES * (c + 1))
            zscr_ref[c, pl.ds(grp * rows_per_group, rows_per_group // 2, stride=2), :] = z_even[:, cols]
            zscr_ref[c, pl.ds(grp * rows_per_group + 1, rows_per_group // 2, stride=2), :] = z_odd[:, cols]
        z = jnp.concatenate([zscr_ref[c, rows, :] for c in range(zscr_ref.shape[0])], axis=1)
        x1 = x_ref[rows, :] + z + jnp.dot(ya_ref[rows, :], wo_ref[F_WIDTH:, :], preferred_element_type=F32)

        hq = _rms(x1, gq_ref[...]).astype(BF16)
        qm = (jnp.dot(hq, wmq_ref[...], preferred_element_type=F32) * (MEM_HEAD_DIM ** -0.5)).astype(BF16)
        heads = []
        for hd in range(MEM_HEADS):
            sl = slice(MEM_HEAD_DIM * hd, MEM_HEAD_DIM * (hd + 1))
            s = lax.dot_general(qm[:, sl], mk_ref[:, sl], (((1,), (1,)), ((), ())),
                                preferred_element_type=F32)
            p = jnp.exp(s - jnp.max(s, axis=1, keepdims=True))
            l = jnp.sum(p, axis=1, keepdims=True)
            heads.append((jnp.dot(p.astype(BF16), mv_ref[:, sl], preferred_element_type=F32) / l).astype(BF16))
        o = jnp.concatenate(heads, axis=1)
        x2 = x1 + jnp.dot(o, wmo_ref[...], preferred_element_type=F32)
        x2_ref[rows, :] = x2

        h3 = _rms(x2, gf_ref[...])
        h3_hi = h3.astype(BF16)
        hext_ref[rows, :D_MODEL] = h3_hi
        h3_lo = (h3 - h3_hi.astype(F32)).astype(BF16)
        hi_terms = jnp.dot(h3_hi, wr_ref[...], preferred_element_type=F32)
        logits = (hi_terms[:, :LANES] + hi_terms[:, LANES:]
                  + jnp.dot(h3_lo, wr_ref[:, :LANES], preferred_element_type=F32))
        lane = lax.broadcasted_iota(I32, logits.shape, 1)
        logits = jnp.where(lane < N_EXPERTS, logits, -jnp.inf)
        e = jnp.exp(logits - jnp.max(logits, axis=1, keepdims=True))
        aff = e / jnp.sum(e, axis=1, keepdims=True)
        aff_ref[rows, :] = aff
        hi = aff.astype(BF16)
        r1 = aff - hi.astype(F32)
        mid = r1.astype(BF16)
        lo = (r1 - mid.astype(F32)).astype(BF16)
        hext_ref[rows, D_MODEL:] = jnp.where(
            lane < N_EXPERTS, hi,
            jnp.where(lane < 2 * N_EXPERTS, pltpu.roll(mid.astype(F32), N_EXPERTS, 1).astype(BF16),
                      pltpu.roll(lo.astype(F32), 2 * N_EXPERTS, 1).astype(BF16)))


def _mixing(x2d, y_f, y_a, w_out, g_mem_q, w_mem_q, mk, mv, w_mem_o, g_ffn, w_router):
    tm = TM_MIX
    wr = jnp.concatenate([w_router, jnp.zeros((D_MODEL, LANES - N_EXPERTS), F32)], axis=1)
    wr_hi = wr.astype(BF16)
    wr_cat = jnp.concatenate([wr_hi, (wr - wr_hi.astype(F32)).astype(BF16)], axis=1)
    full = lambda shape: pl.BlockSpec(shape, lambda i: (0,) * len(shape))
    tile = lambda w: pl.BlockSpec((tm, w), lambda i: (i, 0))
    per_half = HALF // tm
    per_seq = SEQ // tm
    per_batch = pl.BlockSpec((None, MEM_LEN, D_MODEL), lambda i: (i // per_seq, 0, 0))
    yf_spec = pl.BlockSpec((None, 2, None, tm // 2, F_WIDTH),
                           lambda i: (i // per_seq, 0, (i % per_seq) // per_half, i % per_half, 0))
    return pl.pallas_call(
        _mix_kernel,
        grid=(TOKENS // tm,),
        in_specs=[tile(D_MODEL), yf_spec, tile(F_WIDTH), full((D_MODEL, D_MODEL)), full((1, D_MODEL)),
                  full((D_MODEL, D_MODEL)), per_batch, per_batch, full((D_MODEL, D_MODEL)), full((1, D_MODEL)),
                  full((D_MODEL, 2 * LANES))],
        out_specs=[tile(D_MODEL), tile(D_MODEL + LANES), tile(LANES)],
        out_shape=[jax.ShapeDtypeStruct((TOKENS, D_MODEL), F32),
                   jax.ShapeDtypeStruct((TOKENS, D_MODEL + LANES), BF16),
                   jax.ShapeDtypeStruct((TOKENS, LANES), F32)],
        scratch_shapes=[pltpu.VMEM((D_MODEL // LANES, tm, LANES), F32)],
        compiler_params=_params("parallel"),
        name="mix_memattn_router",
    )(x2d, y_f, y_a, w_out.astype(BF16), g_mem_q.reshape(1, -1), w_mem_q.astype(BF16), mk, mv,
      w_mem_o.astype(BF16), g_ffn.reshape(1, -1), wr_cat)


def _topk_kernel(aff_ref, slot_ref, offs_ref):
    aff = aff_ref[...]
    rows = aff.shape[0]

    thr = jnp.zeros((rows, 1), I32)
    for bit in range(30, -1, -1):
        cand = thr | (1 << bit)
        cnt = jnp.sum(jnp.where(aff >= pltpu.bitcast(cand, F32), 1.0, 0.0), axis=1, keepdims=True)
        thr = jnp.where(cnt >= CAPACITY, cand, thr)
    thr_f = pltpu.bitcast(thr, F32)

    chunk = 256
    r = lax.broadcasted_iota(I32, (chunk, chunk), 0)
    c = lax.broadcasted_iota(I32, (chunk, chunk), 1)
    tri = jnp.where(r < c, 1.0, 0.0).astype(BF16)

    def exclusive_count(mask):
        off = jnp.zeros((rows, 1), F32)
        outs = []
        for j in range(SEQ // chunk):
            mj = mask[:, chunk * j:chunk * (j + 1)]
            outs.append(jnp.dot(mj.astype(BF16), tri, preferred_element_type=F32) + off)
            off = off + jnp.sum(mj, axis=1, keepdims=True)
        return jnp.concatenate(outs, axis=1), off

    gt = aff > thr_f
    tie = jnp.where(aff == thr_f, 1.0, 0.0)
    n_gt = jnp.sum(jnp.where(gt, 1.0, 0.0), axis=1, keepdims=True)
    tie_rank, _ = exclusive_count(tie)
    sel = jnp.where(gt | ((tie > 0.0) & (tie_rank < CAPACITY - n_gt)), 1.0, 0.0)
    slot, _ = exclusive_count(sel)
    slot_ref[...] = jnp.where(sel > 0.0, slot.astype(I32), -1)
    tok = lax.broadcasted_iota(I32, (SEQ, LANES), 0)
    j = lax.broadcasted_iota(I32, (SEQ, LANES), 1)
    before = jnp.where(tok < j * TOKEN_CHUNK, 1.0, 0.0).astype(BF16)
    offs_ref[...] = jnp.dot(sel.astype(BF16), before, preferred_element_type=F32).astype(I32)


def _expert_slots(aff):
    aff_t = aff[:, :N_EXPERTS].reshape(BATCH, SEQ, N_EXPERTS).transpose(0, 2, 1).reshape(BATCH * N_EXPERTS, SEQ)
    slots, offs = pl.pallas_call(
        _topk_kernel,
        out_shape=[jax.ShapeDtypeStruct((BATCH * N_EXPERTS, SEQ), I32),
                   jax.ShapeDtypeStruct((BATCH * N_EXPERTS, LANES), I32)],
        compiler_params=pltpu.CompilerParams(vmem_limit_bytes=VMEM_LIMIT),
        name="expert_topk",
    )(aff_t)
    return slots, offs[:, :OFFS_STRIDE].reshape(-1)


def _window_start(first, win):
    start = jnp.minimum((first >> SLOT_ALIGN_SHIFT) << SLOT_ALIGN_SHIFT, CAPACITY - win)
    return pl.multiple_of(start, 1 << SLOT_ALIGN_SHIFT)


def _gather_kernel(offs_ref, slot_ref, h_ref, x_ref):
    win = GATHER_WIN
    last = CAPACITY - win
    b = pl.program_id(0)
    e0 = pl.program_id(1) * GATHER_EXPERTS
    j = pl.program_id(2)

    @pl.when(j == 0)
    def _():
        x_ref[...] = jnp.zeros_like(x_ref)

    def bounds(e):
        base = (b * N_EXPERTS + e0 + e) * OFFS_STRIDE + j
        return offs_ref[base], offs_ref[base + 1]

    row = lax.broadcasted_iota(I32, (win, TOKEN_CHUNK), 0)
    h_c = h_ref[...]
    starts = [_window_start(bounds(e)[0], win) for e in range(GATHER_EXPERTS)]
    onehot = jnp.concatenate(
        [jnp.where(row + starts[e] == slot_ref[e:e + 1, :], 1.0, 0.0).astype(BF16)
         for e in range(GATHER_EXPERTS)], axis=0)
    picked = jnp.dot(onehot, h_c, preferred_element_type=F32).astype(BF16)
    for e in range(GATHER_EXPERTS):
        rows = pl.ds(starts[e], win)
        x_ref[e, rows, :] = x_ref[e, rows, :] + picked[e * win:(e + 1) * win]

    for e in range(GATHER_EXPERTS):
        _, end = bounds(e)
        covered = starts[e] + win
        slot_e = slot_ref[e:e + 1, :]

        def extra_window(i, carry):
            lo = covered + i * win
            r0 = pl.multiple_of(jnp.minimum(lo, last), 1 << SLOT_ALIGN_SHIFT)
            hot = jnp.where((row + r0 == slot_e) & (slot_e >= lo), 1.0, 0.0).astype(BF16)
            rows = pl.ds(r0, win)
            x_ref[e, rows, :] = x_ref[e, rows, :] + jnp.dot(hot, h_c, preferred_element_type=F32).astype(BF16)
            return carry

        lax.fori_loop(0, jnp.maximum(end - covered + win - 1, 0) // win, extra_window, 0)


def _gather(offs, slots, h_ext):
    slots = slots.reshape(BATCH, N_EXPERTS, SEQ)
    h_ext = h_ext.reshape(BATCH, SEQ, D_MODEL + LANES)
    return pl.pallas_call(
        _gather_kernel,
        grid_spec=pltpu.PrefetchScalarGridSpec(
            num_scalar_prefetch=1,
            grid=(BATCH, N_EXPERTS // GATHER_EXPERTS, SEQ // TOKEN_CHUNK),
            in_specs=[pl.BlockSpec((None, GATHER_EXPERTS, TOKEN_CHUNK), lambda b, g, j, offs: (b, g, j)),
                      pl.BlockSpec((None, TOKEN_CHUNK, D_MODEL + LANES), lambda b, g, j, offs: (b, j, 0))],
            out_specs=pl.BlockSpec((None, GATHER_EXPERTS, None, CAPACITY, D_MODEL + LANES),
                                   lambda b, g, j, offs: (b // FFN_SEQS, g, b % FFN_SEQS, 0, 0))),
        out_shape=jax.ShapeDtypeStruct((BATCH // FFN_SEQS, N_EXPERTS, FFN_SEQS, CAPACITY, D_MODEL + LANES),
                                       BF16),
        compiler_params=_params("parallel", "parallel", "arbitrary"),
        name="expert_gather",
    )(offs, slots, h_ext)


def _expert_kernel(x_ref, wg_ref, wu_ref, wd_ref, y_ref, wg_s, wu_s, wd_s):
    g = pl.program_id(0)
    part = pl.program_id(1)
    slab = wg_ref.shape[0]

    @pl.when(g < N_EXPERTS)
    def _():
        rows = pl.ds(pl.multiple_of(part * slab, slab), slab)
        wg_s[g % 2, rows, :] = wg_ref[...].astype(BF16)
        wu_s[g % 2, rows, :] = wu_ref[...].astype(BF16)
        wd_s[g % 2, rows, :] = wd_ref[...].astype(BF16)

    @pl.when(g == 0)
    def _():
        y_ref[...] = jnp.zeros_like(y_ref)

    @pl.when(g > 0)
    def _():
        e = g - 1
        cur = e % 2
        rows = x_ref.shape[0] * x_ref.shape[1]
        xin = x_ref[:, :, :D_MODEL].reshape(rows, D_MODEL)
        ext = x_ref[:, :, D_MODEL:].reshape(rows, LANES).astype(F32)
        lane = lax.broadcasted_iota(I32, ext.shape, 1)
        mine = (lane == e) | (lane == e + N_EXPERTS) | (lane == e + 2 * N_EXPERTS)
        gate = jnp.sum(jnp.where(mine, ext, 0.0), axis=1, keepdims=True)

        a = jnp.dot(xin, wg_s[cur], preferred_element_type=F32)
        b = jnp.dot(xin, wu_s[cur], preferred_element_type=F32)
        hid = (a / (1.0 + jnp.exp(-a)) * b).astype(BF16)
        y = (jnp.dot(hid, wd_s[cur], preferred_element_type=F32) * gate).astype(BF16)
        y_ref[...] = y.reshape(y_ref.shape)


def _experts(xin, w_gate, w_up, w_down):
    parts = BATCH // FFN_SEQS
    slab = D_MODEL // parts
    prev = lambda g: jnp.maximum(g - 1, 0)
    wspec = pl.BlockSpec((None, slab, D_MODEL), lambda g, p: (jnp.minimum(g, N_EXPERTS - 1), p, 0))
    return pl.pallas_call(
        _expert_kernel,
        grid=(N_EXPERTS + 1, parts),
        in_specs=[pl.BlockSpec((None, None, FFN_SEQS, CAPACITY, D_MODEL + LANES),
                               lambda g, p: (p, prev(g), 0, 0, 0)),
                  wspec, wspec, wspec],
        out_specs=pl.BlockSpec((None, None, FFN_SEQS, CAPACITY, D_MODEL),
                               lambda g, p: (p, jnp.where(g == 0, N_EXPERTS, g - 1), 0, 0, 0)),
        out_shape=jax.ShapeDtypeStruct((parts, N_EXPERTS + 1, FFN_SEQS, CAPACITY, D_MODEL), BF16),
        scratch_shapes=[pltpu.VMEM((2, D_MODEL, D_MODEL), BF16)] * 3,
        compiler_params=_params("arbitrary", "arbitrary"),
        name="expert_ffn",
    )(xin, w_gate, w_up, w_down)


def _combine_kernel(offs_ref, x2_ref, slot_ref, y_ref, g_ref, o_ref):
    b = pl.program_id(0)
    j = pl.program_id(1)
    win = SCATTER_WIN
    lane = lax.broadcasted_iota(I32, (TOKEN_CHUNK, win), 1)
    slot = slot_ref[...]

    def bounds(e):
        base = (b * N_EXPERTS + e) * OFFS_STRIDE + j
        return offs_ref[base], offs_ref[base + 1]

    starts = [_window_start(bounds(e)[0], win) for e in range(N_EXPERTS)]
    onehot = jnp.concatenate(
        [jnp.where(lane + starts[e] == slot[:, e:e + 1], 1.0, 0.0).astype(BF16) for e in range(N_EXPERTS)],
        axis=1)
    rows = jnp.concatenate([y_ref[e, pl.ds(starts[e], win), :] for e in range(N_EXPERTS)], axis=0)
    x3 = x2_ref[...] + jnp.dot(onehot, rows, preferred_element_type=F32)

    leftover = [bounds(e)[1] > starts[e] + win for e in range(N_EXPERTS)]
    any_leftover = functools.reduce(jnp.logical_or, leftover)

    @pl.when(jnp.logical_not(any_leftover))
    def _():
        o_ref[...] = _rms(x3, g_ref[...])

    @pl.when(any_leftover)
    def _():
        o_ref[...] = x3
        tail_lane = lax.broadcasted_iota(I32, (TOKEN_CHUNK, CAPACITY - win), 1) + win
        for e in range(N_EXPERTS):
            @pl.when(leftover[e])
            def _():
                sl = slot[:, e:e + 1]
                hot = jnp.where((tail_lane == sl) & (sl >= starts[e] + win), 1.0, 0.0).astype(BF16)
                o_ref[...] += jnp.dot(hot, y_ref[e, win:, :], preferred_element_type=F32)

        o_ref[...] = _rms(o_ref[...], g_ref[...])


def _combine(offs, x2, slots, y, g_final):
    slots_t = slots.reshape(BATCH, N_EXPERTS, SEQ).transpose(0, 2, 1)
    x2 = x2.reshape(BATCH, SEQ, D_MODEL)
    tm = TOKEN_CHUNK
    return pl.pallas_call(
        _combine_kernel,
        grid_spec=pltpu.PrefetchScalarGridSpec(
            num_scalar_prefetch=1,
            grid=(BATCH, SEQ // tm),
            in_specs=[pl.BlockSpec((None, tm, D_MODEL), lambda b, i, offs: (b, i, 0)),
                      pl.BlockSpec((None, tm, N_EXPERTS), lambda b, i, offs: (b, i, 0)),
                      pl.BlockSpec((None, N_EXPERTS, None, CAPACITY, D_MODEL),
                                   lambda b, i, offs: (b // FFN_SEQS, 0, b % FFN_SEQS, 0, 0)),
                      pl.BlockSpec((1, D_MODEL), lambda b, i, offs: (0, 0))],
            out_specs=pl.BlockSpec((None, tm, D_MODEL), lambda b, i, offs: (b, i, 0))),
        out_shape=jax.ShapeDtypeStruct((BATCH, SEQ, D_MODEL), F32),
        compiler_params=_params("parallel", "arbitrary"),
        name="combine_final_norm",
    )(offs, x2, slots_t, y, g_final.reshape(1, -1))


def kernel(x, mem, positions, g_mix, w_in, g_q_lat, w_q_up, g_kv_lat, w_kv_up, w_fourier, w_out, g_mem_q,
           g_mem_kv, w_mem_q, w_mem_kv, w_mem_o, g_ffn, w_router, w_exp_gate, w_exp_up, w_exp_down, g_final):
    assert x.shape == (BATCH, SEQ, D_MODEL) and g_mix.shape[0] == 1
    x2d = x.reshape(TOKENS, D_MODEL)
    rope_cs = _rope_tables(positions)
    ab = _channel_mats(w_fourier[0])
    v12, q, k, v, ksq = _in_projection(x2d, g_mix[0], w_in[0], g_q_lat[0], w_q_up[0], g_kv_lat[0], w_kv_up[0],
                                       ab, rope_cs)
    y_f = _sequence_dft(v12)
    y_a = _mla_attention(q, k, v, ksq)
    mk, mv = _memory_kv(mem, g_mem_kv[0], w_mem_kv[0])
    x2, h_ext, aff = _mixing(x2d, y_f, y_a, w_out[0], g_mem_q[0], w_mem_q[0], mk, mv, w_mem_o[0], g_ffn[0],
                             w_router[0])
    slots, offs = _expert_slots(aff)
    xin = _gather(offs, slots, h_ext)
    y = _experts(xin, w_exp_gate[0], w_exp_up[0], w_exp_down[0])
    return _combine(offs, x2, slots, y, g_final)
```

```python
import functools

import numpy as np
import jax
import jax.numpy as jnp
from jax import lax
from jax.experimental import pallas as pl
from jax.experimental.pallas import tpu as pltpu

F32 = jnp.float32
BF16 = jnp.bfloat16
I32 = jnp.int32

D_MODEL = 1024
BATCH = 4
SEQ = 4096
TOKENS = BATCH * SEQ
MEM_LEN = 256
RMS_EPS = 1e-6
F_GROUPS = 8
F_GROUP_DIM = 64
F_WIDTH = F_GROUPS * F_GROUP_DIM
MLA_HEADS = 8
QK_NOPE_DIM = 64
QK_ROPE_DIM = 32
V_HEAD_DIM = 64
Q_LORA_RANK = 384
KV_LORA_RANK = 256
ROPE_THETA = 10000.0
MEM_HEADS = 4
MEM_HEAD_DIM = D_MODEL // MEM_HEADS
N_EXPERTS = 16
CAPACITY = 2 * SEQ // N_EXPERTS

NORM_SLACK = 1.01
SAFE_SHIFT = 30.0

LANES = 128
HEAD_PAD = 128
IN_PAD = 1280
HALF = SEQ // 2
QUARTER = SEQ // 4
VMEM_LIMIT = 56 * 1024 * 1024

TM_IN = 1024
IN_GROUPS = 1
TM_MIX = 1024
MIX_GROUPS = 2
TQ = 1024
ATTN_HEADS = 4
TOKEN_CHUNK = 512
OFFS_STRIDE = SEQ // TOKEN_CHUNK + 1
GATHER_WIN = 96
SCATTER_WIN = 128
SLOT_ALIGN_SHIFT = 4
GATHER_EXPERTS = 16
FFN_SEQS = 2


def _rms(x, g):
    return x * lax.rsqrt(jnp.mean(x * x, axis=-1, keepdims=True) + RMS_EPS) * g


def _params(*sem):
    return pltpu.CompilerParams(dimension_semantics=sem, vmem_limit_bytes=VMEM_LIMIT)


def _rope_kernel(pos_ref, freq_ref, cos_ref, sin_ref):
    ang = pos_ref[...] * freq_ref[...]
    cos_ref[...] = jnp.cos(ang)
    sin_ref[...] = jnp.sin(ang)


def _rope_tables(positions):
    half = QK_ROPE_DIM // 2
    freqs = 1.0 / (ROPE_THETA ** (jnp.arange(0, QK_ROPE_DIM, 2, dtype=F32) / QK_ROPE_DIM))
    rows = TOKENS * half // LANES
    pos = jnp.repeat(positions.astype(F32).reshape(-1), half).reshape(rows, LANES)
    freq = jnp.tile(freqs, LANES // half).reshape(1, LANES)
    cos, sin = pl.pallas_call(
        _rope_kernel,
        out_shape=(jax.ShapeDtypeStruct((rows, LANES), F32),) * 2,
        name="rope_tables",
    )(pos, freq)
    return cos.reshape(TOKENS, half), sin.reshape(TOKENS, half)


def _rope_expanders():
    half = QK_ROPE_DIM // 2
    lane = np.arange(LANES)
    hit = (lane[None, :] % half) == np.arange(half)[:, None]
    is_sin = (lane // QK_ROPE_DIM) % 2 == 1
    return (jnp.asarray((hit & ~is_sin[None, :]).astype(np.float32)).astype(BF16),
            jnp.asarray((hit & is_sin[None, :]).astype(np.float32)).astype(BF16))


def _chan_kernel(cbd_ref, sbd_ref, w_ref, ab_ref):
    w = w_ref[...]
    ab_ref[:, :F_WIDTH] = jnp.dot(cbd_ref[...], w, precision=lax.Precision.HIGHEST,
                                  preferred_element_type=F32).astype(BF16)
    ab_ref[:, F_WIDTH:] = jnp.dot(sbd_ref[...], w, precision=lax.Precision.HIGHEST,
                                  preferred_element_type=F32).astype(BF16)


def _channel_mats(w_fourier):
    c = np.arange(F_GROUP_DIM)
    ang = 2.0 * np.pi * np.outer(c, c) / F_GROUP_DIM
    scale = F_GROUP_DIM ** -0.5
    eye = np.eye(F_GROUPS)
    cbd = np.kron(eye, np.cos(ang) * scale).astype(np.float32)
    sbd = np.kron(eye, np.sin(ang) * scale).astype(np.float32)
    wbd = (jnp.eye(F_GROUPS, dtype=F32)[:, None, :, None] * w_fourier[:, :, None, :]).reshape(F_WIDTH, F_WIDTH)
    return pl.pallas_call(
        _chan_kernel,
        out_shape=jax.ShapeDtypeStruct((F_WIDTH, 2 * F_WIDTH), BF16),
        name="channel_dft_fold",
    )(jnp.asarray(cbd), jnp.asarray(sbd), wbd)


def _inproj_kernel(x_ref, gmix_ref, win_ref, gq_ref, wq_ref, gkv_ref, wkv_ref, ab_ref, cos_ref, sin_ref,
                   ec_ref, es_ref, hsum_ref, v12_ref, q_ref, k_ref, v_ref, ksq_ref, vscr_ref):
    rows_per_group = x_ref.shape[0] // IN_GROUPS
    half_rows = rows_per_group // 2
    ksq_max = None
    for grp in range(IN_GROUPS):
        r0 = grp * rows_per_group
        rows = pl.ds(r0, rows_per_group)
        half = pl.ds(r0 // 2, half_rows)
        h = _rms(x_ref[rows, :], gmix_ref[...]).astype(BF16)
        u = jnp.dot(h, win_ref[...], preferred_element_type=F32)

        n_pairs = ab_ref.shape[0]
        for gp in range(n_pairs):
            pair = jnp.dot(u[:, LANES * gp:LANES * (gp + 1)].astype(BF16), ab_ref[gp],
                           preferred_element_type=F32)
            for part in range(2):
                c = part * n_pairs + gp
                cols = slice(LANES * c, LANES * (c + 1))
                vscr_ref[c, rows, :] = pair[:, LANES * part:LANES * (part + 1)]
                v12_ref[0, half, cols] = vscr_ref[c, pl.ds(r0, half_rows, stride=2), :].astype(BF16)
                v12_ref[1, half, cols] = vscr_ref[c, pl.ds(r0 + 1, half_rows, stride=2), :].astype(BF16)

        tab = None
        for col_ref, exp_ref in ((cos_ref, ec_ref), (sin_ref, es_ref)):
            col = col_ref[rows, :]
            hi = col.astype(BF16)
            lo = (col - hi.astype(F32)).astype(BF16)
            term = (jnp.dot(hi, exp_ref[...], preferred_element_type=F32)
                    + jnp.dot(lo, exp_ref[...], preferred_element_type=F32))
            tab = term if tab is None else tab + term
        lane = lax.broadcasted_iota(I32, tab.shape, 1)
        rope0, rope1 = QK_NOPE_DIM, QK_NOPE_DIM + QK_ROPE_DIM
        back = LANES - QK_ROPE_DIM
        c1 = jnp.where(lane < rope0, 1.0, jnp.where(lane < rope1, tab, 0.0))
        c2 = jnp.where(lane >= rope1, tab, 0.0)

        q0 = F_WIDTH
        qn = _rms(u[:, q0:q0 + Q_LORA_RANK], gq_ref[...]).astype(BF16)
        qa = jnp.dot(qn, wq_ref[...], preferred_element_type=F32)
        q_blocks = []
        for hd in range(MLA_HEADS):
            blk = qa[:, HEAD_PAD * hd:HEAD_PAD * (hd + 1)]
            q_blocks.append(blk * c1 + pltpu.roll(blk * c2, back, 1))
            q_ref[rows, HEAD_PAD * hd:HEAD_PAD * (hd + 1)] = q_blocks[hd].astype(BF16)

        kv0 = q0 + Q_LORA_RANK
        kvn = _rms(u[:, kv0:kv0 + KV_LORA_RANK], gkv_ref[...]).astype(BF16)
        kv = jnp.dot(kvn, wkv_ref[...], preferred_element_type=F32)
        kr0 = kv0 + KV_LORA_RANK
        t = u[:, kr0:kr0 + LANES] * jnp.where(lane < 2 * QK_ROPE_DIM, tab, 0.0)
        kr = t + pltpu.roll(t, back, 1)
        kr = jnp.where((lane >= rope0) & (lane < rope1), pltpu.roll(kr, rope0, 1), 0.0)
        k_blocks = [kv[:, HEAD_PAD * hd:HEAD_PAD * (hd + 1)] + kr for hd in range(MLA_HEADS)]
        for hd in range(MLA_HEADS):
            k_ref[rows, HEAD_PAD * hd:HEAD_PAD * (hd + 1)] = k_blocks[hd].astype(BF16)
        v_ref[rows, :] = kv[:, MLA_HEADS * HEAD_PAD:].astype(BF16)
        sq = jnp.concatenate(
            [jnp.concatenate([(blk * blk).astype(BF16) for blk in blocks], axis=1)
             for blocks in (k_blocks, q_blocks)], axis=0)
        sq = jnp.dot(sq, hsum_ref[...], preferred_element_type=F32)
        grp_max = jnp.concatenate([jnp.max(sq[:rows_per_group], axis=0, keepdims=True),
                                   jnp.max(sq[rows_per_group:], axis=0, keepdims=True)], axis=0)
        ksq_max = grp_max if ksq_max is None else jnp.maximum(ksq_max, grp_max)
    ksq_ref[...] = ksq_max


def _in_projection(x2d, g_mix, w_in, g_q, w_q_up, g_kv, w_kv_up, ab, cos, sin):
    w_kr = w_in[:, -QK_ROPE_DIM:]
    half = QK_ROPE_DIM // 2
    w_kr_rot = jnp.concatenate([-w_kr[:, half:], w_kr[:, :half]], axis=1)
    win = jnp.concatenate(
        [w_in, w_kr_rot, jnp.zeros((D_MODEL, IN_PAD - w_in.shape[1] - QK_ROPE_DIM), F32)], axis=1).astype(BF16)
    scale = (QK_NOPE_DIM + QK_ROPE_DIM) ** -0.5
    wq = w_q_up.reshape(Q_LORA_RANK, MLA_HEADS, QK_NOPE_DIM + QK_ROPE_DIM) * scale
    wq_rope = wq[:, :, QK_NOPE_DIM:]
    wq_rot = jnp.concatenate([-wq_rope[:, :, half:], wq_rope[:, :, :half]], axis=2)
    wq = jnp.concatenate([wq, wq_rot], axis=2).reshape(Q_LORA_RANK, MLA_HEADS * HEAD_PAD).astype(BF16)
    wkv = w_kv_up.reshape(KV_LORA_RANK, MLA_HEADS, QK_NOPE_DIM + V_HEAD_DIM)
    wk = jnp.concatenate([wkv[:, :, :QK_NOPE_DIM],
                          jnp.zeros((KV_LORA_RANK, MLA_HEADS, HEAD_PAD - QK_NOPE_DIM), F32)], axis=2)
    wkv = jnp.concatenate([wk.reshape(KV_LORA_RANK, MLA_HEADS * HEAD_PAD),
                           wkv[:, :, QK_NOPE_DIM:].reshape(KV_LORA_RANK, MLA_HEADS * V_HEAD_DIM)],
                          axis=1).astype(BF16)

    ab = jnp.stack([jnp.concatenate([ab[LANES * g:LANES * (g + 1), LANES * g:LANES * (g + 1)],
                                     ab[LANES * g:LANES * (g + 1), F_WIDTH + LANES * g:F_WIDTH + LANES * (g + 1)]],
                                    axis=1) for g in range(F_WIDTH // LANES)])
    tm = TM_IN
    ec, es = _rope_expanders()
    hsum = jnp.asarray((np.arange(MLA_HEADS * HEAD_PAD)[:, None] // HEAD_PAD
                        == np.arange(LANES)[None, :]).astype(np.float32)).astype(BF16)
    full = lambda shape: pl.BlockSpec(shape, lambda i: (0,) * len(shape))
    tile = lambda w: pl.BlockSpec((tm, w), lambda i: (i, 0))
    per_half = HALF // tm
    per_seq = SEQ // tm
    v12_spec = pl.BlockSpec((None, 2, None, tm // 2, 2 * F_WIDTH),
                            lambda i: (i // per_seq, 0, (i % per_seq) // per_half, i % per_half, 0))
    return pl.pallas_call(
        _inproj_kernel,
        grid=(TOKENS // tm,),
        in_specs=[tile(D_MODEL), full((1, D_MODEL)), full(win.shape), full((1, Q_LORA_RANK)), full(wq.shape),
                  full((1, KV_LORA_RANK)), full(wkv.shape), full(ab.shape), tile(QK_ROPE_DIM // 2),
                  tile(QK_ROPE_DIM // 2), full(ec.shape), full(es.shape), full(hsum.shape)],
        out_specs=[v12_spec, tile(MLA_HEADS * HEAD_PAD), tile(MLA_HEADS * HEAD_PAD),
                   tile(MLA_HEADS * V_HEAD_DIM), pl.BlockSpec((None, 2, LANES), lambda i: (i, 0, 0))],
        out_shape=[jax.ShapeDtypeStruct((BATCH, 2, 2, QUARTER, 2 * F_WIDTH), BF16),
                   jax.ShapeDtypeStruct((TOKENS, MLA_HEADS * HEAD_PAD), BF16),
                   jax.ShapeDtypeStruct((TOKENS, MLA_HEADS * HEAD_PAD), BF16),
                   jax.ShapeDtypeStruct((TOKENS, MLA_HEADS * V_HEAD_DIM), BF16),
                   jax.ShapeDtypeStruct((TOKENS // tm, 2, LANES), F32)],
        scratch_shapes=[pltpu.VMEM((2 * F_WIDTH // LANES, tm, LANES), F32)],
        compiler_params=_params("parallel"),
        name="in_projection",
    )(x2d, g_mix.reshape(1, -1), win, g_q.reshape(1, -1), wq, g_kv.reshape(1, -1), wkv, ab, cos, sin, ec, es, hsum)


def _seq_dft_kernel(v_ref, m_ref, y_ref):
    sign = jnp.where(pl.program_id(1) == 0, 1.0, -1.0)
    parts = []
    for q in range(2):
        ab = (v_ref[q, 0].astype(F32) + sign * v_ref[q, 1].astype(F32)).astype(BF16)
        parts.append(jnp.dot(m_ref[q, :, :QUARTER], ab[:, :F_WIDTH], preferred_element_type=F32)
                     + jnp.dot(m_ref[q, :, QUARTER:], ab[:, F_WIDTH:], preferred_element_type=F32))
    y_ref[0] = (parts[0] + parts[1]).astype(BF16)
    y_ref[1] = (parts[0] - parts[1]).astype(BF16)


def _seq_dft_mats():
    i = np.arange(QUARTER)
    out = np.zeros((2, 2, QUARTER, 2 * QUARTER), np.float32)
    for p in range(2):
        for q in range(2):
            prod = np.outer(2 * i + p, 2 * i + q) % SEQ
            ang = 2.0 * np.pi * prod / SEQ
            out[p, q, :, :QUARTER] = np.cos(ang) / np.sqrt(SEQ)
            out[p, q, :, QUARTER:] = -np.sin(ang) / np.sqrt(SEQ)
    return out


def _sequence_dft(v):
    mats = jnp.asarray(_seq_dft_mats()).astype(BF16)
    return pl.pallas_call(
        _seq_dft_kernel,
        grid=(BATCH, 2),
        in_specs=[pl.BlockSpec((None, 2, 2, QUARTER, 2 * F_WIDTH), lambda b, p: (b, 0, 0, 0, 0)),
                  pl.BlockSpec((None, 2, QUARTER, 2 * QUARTER), lambda b, p: (p, 0, 0, 0))],
        out_specs=pl.BlockSpec((None, None, 2, QUARTER, F_WIDTH), lambda b, p: (b, p, 0, 0, 0)),
        out_shape=jax.ShapeDtypeStruct((BATCH, 2, 2, QUARTER, F_WIDTH), BF16),
        compiler_params=_params("parallel", "arbitrary"),
        name="sequence_dft",
    )(v, mats)


def _mla_kernel(q_ref, k_ref, v_ref, sq_ref, o_ref):
    v = v_ref[...]
    hg = pl.program_id(1)
    sq = jnp.max(sq_ref[...], axis=0)
    bounds = jnp.sqrt(sq[0:1] * sq[1:2]) * NORM_SLACK
    head_lane = lax.broadcasted_iota(I32, bounds.shape, 1)
    head_bounds = [jnp.max(jnp.where(head_lane == ATTN_HEADS * hg + j, bounds, 0.0), axis=1, keepdims=True)
                   for j in range(ATTN_HEADS)]
    safe = jnp.max(functools.reduce(jnp.maximum, head_bounds)) <= SAFE_SHIFT

    def attend(j, row_shift):
        cols = slice(HEAD_PAD * j, HEAD_PAD * (j + 1))
        out_cols = slice(V_HEAD_DIM * j, V_HEAD_DIM * (j + 1))
        s = lax.dot_general(q_ref[:, cols], k_ref[:, cols], (((1,), (1,)), ((), ())),
                            preferred_element_type=F32)
        p = jnp.exp(s - row_shift(s))
        l = jnp.sum(p, axis=1, keepdims=True)
        o = jnp.dot(p.astype(BF16), v, preferred_element_type=F32) / l
        o_ref[:, out_cols] = o[:, out_cols].astype(BF16)

    @pl.when(safe)
    def _():
        for j in range(ATTN_HEADS):
            attend(j, lambda s: head_bounds[j])

    @pl.when(jnp.logical_not(safe))
    def _():
        for j in range(ATTN_HEADS):
            attend(j, lambda s: jnp.max(s, axis=1, keepdims=True))


def _mla_attention(q, k, v, ksq):
    q = q.reshape(BATCH, SEQ, -1)
    k = k.reshape(BATCH, SEQ, -1)
    v = v.reshape(BATCH, SEQ, -1)
    ksq = ksq.reshape(BATCH, SEQ // TM_IN, 2, LANES)
    out = pl.pallas_call(
        _mla_kernel,
        grid=(BATCH, MLA_HEADS // ATTN_HEADS, SEQ // TQ),
        in_specs=[pl.BlockSpec((None, TQ, ATTN_HEADS * HEAD_PAD), lambda b, hg, i: (b, i, hg)),
                  pl.BlockSpec((None, SEQ, ATTN_HEADS * HEAD_PAD), lambda b, hg, i: (b, 0, hg)),
                  pl.BlockSpec((None, SEQ, ATTN_HEADS * V_HEAD_DIM), lambda b, hg, i: (b, 0, hg)),
                  pl.BlockSpec((None, SEQ // TM_IN, 2, LANES), lambda b, hg, i: (b, 0, 0, 0))],
        out_specs=pl.BlockSpec((None, TQ, ATTN_HEADS * V_HEAD_DIM), lambda b, hg, i: (b, i, hg)),
        out_shape=jax.ShapeDtypeStruct((BATCH, SEQ, MLA_HEADS * V_HEAD_DIM), BF16),
        compiler_params=_params("parallel", "parallel", "arbitrary"),
        name="mla_attention",
    )(q, k, v, ksq)
    return out.reshape(TOKENS, MLA_HEADS * V_HEAD_DIM)


def _memkv_kernel(mem_ref, g_ref, w_ref, k_ref, v_ref):
    mn = _rms(mem_ref[...], g_ref[...]).astype(BF16)
    kv = jnp.dot(mn, w_ref[...], preferred_element_type=F32)
    k_ref[...] = kv[:, :D_MODEL].astype(BF16)
    v_ref[...] = kv[:, D_MODEL:].astype(BF16)


def _memory_kv(mem, g_mem_kv, w_mem_kv):
    blk = pl.BlockSpec((None, MEM_LEN, D_MODEL), lambda b: (b, 0, 0))
    return pl.pallas_call(
        _memkv_kernel,
        grid=(BATCH,),
        in_specs=[blk, pl.BlockSpec((1, D_MODEL), lambda b: (0, 0)),
                  pl.BlockSpec((D_MODEL, 2 * D_MODEL), lambda b: (0, 0))],
        out_specs=[blk, blk],
        out_shape=[jax.ShapeDtypeStruct((BATCH, MEM_LEN, D_MODEL), BF16)] * 2,
        compiler_params=_params("parallel"),
        name="memory_kv",
    )(mem, g_mem_kv.reshape(1, -1), w_mem_kv.astype(BF16))


def _mix_kernel(x_ref, yf_ref, ya_ref, wo_ref, gq_ref, wmq_ref, mk_ref, mv_ref, wmo_ref, gf_ref, wr_ref,
                x2_ref, hext_ref, aff_ref, zscr_ref):
    tm = x_ref.shape[0]
    rows_per_group = tm // MIX_GROUPS
    for grp in range(MIX_GROUPS):
        rows = pl.ds(grp * rows_per_group, rows_per_group)
        half = pl.ds(grp * rows_per_group // 2, rows_per_group // 2)
        wo_f = wo_ref[:F_WIDTH, :]
        z_even = jnp.dot(yf_ref[0, half, :], wo_f, preferred_element_type=F32)
        z_odd = jnp.dot(yf_ref[1, half, :], wo_f, preferred_element_type=F32)
        for c in range(zscr_ref.shape[0]):
            cols = slice(LANES * c, LANES * (c + 1))
            zscr_ref[c, pl.ds(grp * rows_per_group, rows_per_group // 2, stride=2), :] = z_even[:, cols]
            zscr_ref[c, pl.ds(grp * rows_per_group + 1, rows_per_group // 2, stride=2), :] = z_odd[:, cols]
        z = jnp.concatenate([zscr_ref[c, rows, :] for c in range(zscr_ref.shape[0])], axis=1)
        x1 = x_ref[rows, :] + z + jnp.dot(ya_ref[rows, :], wo_ref[F_WIDTH:, :], preferred_element_type=F32)

        hq = _rms(x1, gq_ref[...]).astype(BF16)
        qm = (jnp.dot(hq, wmq_ref[...], preferred_element_type=F32) * (MEM_HEAD_DIM ** -0.5)).astype(BF16)
        heads = []
        for hd in range(MEM_HEADS):
            sl = slice(MEM_HEAD_DIM * hd, MEM_HEAD_DIM * (hd + 1))
            s = lax.dot_general(qm[:, sl], mk_ref[:, sl], (((1,), (1,)), ((), ())),
                                preferred_element_type=F32)
            p = jnp.exp(s - jnp.max(s, axis=1, keepdims=True))
            l = jnp.sum(p, axis=1, keepdims=True)
            heads.append((jnp.dot(p.astype(BF16), mv_ref[:, sl], preferred_element_type=F32) / l).astype(BF16))
        o = jnp.concatenate(heads, axis=1)
        x2 = x1 + jnp.dot(o, wmo_ref[...], preferred_element_type=F32)
        x2_ref[rows, :] = x2

        h3 = _rms(x2, gf_ref[...])
        h3_hi = h3.astype(BF16)
        hext_ref[rows, :D_MODEL] = h3_hi
        h3_lo = (h3 - h3_hi.astype(F32)).astype(BF16)
        hi_terms = jnp.dot(h3_hi, wr_ref[...], preferred_element_type=F32)
        logits = (hi_terms[:, :LANES] + hi_terms[:, LANES:]
                  + jnp.dot(h3_lo, wr_ref[:, :LANES], preferred_element_type=F32))
        lane = lax.broadcasted_iota(I32, logits.shape, 1)
        logits = jnp.where(lane < N_EXPERTS, logits, -jnp.inf)
        e = jnp.exp(logits - jnp.max(logits, axis=1, keepdims=True))
        aff = e / jnp.sum(e, axis=1, keepdims=True)
        aff_ref[rows, :] = aff
        hi = aff.astype(BF16)
        r1 = aff - hi.astype(F32)
        mid = r1.astype(BF16)
        lo = (r1 - mid.astype(F32)).astype(BF16)
        hext_ref[rows, D_MODEL:] = jnp.where(
            lane < N_EXPERTS, hi,
            jnp.where(lane < 2 * N_EXPERTS, pltpu.roll(mid.astype(F32), N_EXPERTS, 1).astype(BF16),
                      pltpu.roll(lo.astype(F32), 2 * N_EXPERTS, 1).astype(BF16)))


def _mixing(x2d, y_f, y_a, w_out, g_mem_q, w_mem_q, mk, mv, w_mem_o, g_ffn, w_router):
    tm = TM_MIX
    wr = jnp.concatenate([w_router, jnp.zeros((D_MODEL, LANES - N_EXPERTS), F32)], axis=1)
    wr_hi = wr.astype(BF16)
    wr_cat = jnp.concatenate([wr_hi, (wr - wr_hi.astype(F32)).astype(BF16)], axis=1)
    full = lambda shape: pl.BlockSpec(shape, lambda i: (0,) * len(shape))
    tile = lambda w: pl.BlockSpec((tm, w), lambda i: (i, 0))
    per_half = HALF // tm
    per_seq = SEQ // tm
    per_batch = pl.BlockSpec((None, MEM_LEN, D_MODEL), lambda i: (i // per_seq, 0, 0))
    yf_spec = pl.BlockSpec((None, 2, None, tm // 2, F_WIDTH),
                           lambda i: (i // per_seq, 0, (i % per_seq) // per_half, i % per_half, 0))
    return pl.pallas_call(
        _mix_kernel,
        grid=(TOKENS // tm,),
        in_specs=[tile(D_MODEL), yf_spec, tile(F_WIDTH), full((D_MODEL, D_MODEL)), full((1, D_MODEL)),
                  full((D_MODEL, D_MODEL)), per_batch, per_batch, full((D_MODEL, D_MODEL)), full((1, D_MODEL)),
                  full((D_MODEL, 2 * LANES))],
        out_specs=[tile(D_MODEL), tile(D_MODEL + LANES), tile(LANES)],
        out_shape=[jax.ShapeDtypeStruct((TOKENS, D_MODEL), F32),
                   jax.ShapeDtypeStruct((TOKENS, D_MODEL + LANES), BF16),
                   jax.ShapeDtypeStruct((TOKENS, LANES), F32)],
        scratch_shapes=[pltpu.VMEM((D_MODEL // LANES, tm, LANES), F32)],
        compiler_params=_params("parallel"),
        name="mix_memattn_router",
    )(x2d, y_f, y_a, w_out.astype(BF16), g_mem_q.reshape(1, -1), w_mem_q.astype(BF16), mk, mv,
      w_mem_o.astype(BF16), g_ffn.reshape(1, -1), wr_cat)


def _topk_kernel(aff_ref, slot_ref, offs_ref):
    aff = aff_ref[...]
    rows = aff.shape[0]

    thr = jnp.zeros((rows, 1), I32)
    for bit in range(30, -1, -1):
        cand = thr | (1 << bit)
        cnt = jnp.sum(jnp.where(aff >= pltpu.bitcast(cand, F32), 1.0, 0.0), axis=1, keepdims=True)
        thr = jnp.where(cnt >= CAPACITY, cand, thr)
    thr_f = pltpu.bitcast(thr, F32)

    chunk = 256
    r = lax.broadcasted_iota(I32, (chunk, chunk), 0)
    c = lax.broadcasted_iota(I32, (chunk, chunk), 1)
    tri = jnp.where(r < c, 1.0, 0.0).astype(BF16)

    def exclusive_count(mask):
        off = jnp.zeros((rows, 1), F32)
        outs = []
        for j in range(SEQ // chunk):
            mj = mask[:, chunk * j:chunk * (j + 1)]
            outs.append(jnp.dot(mj.astype(BF16), tri, preferred_element_type=F32) + off)
            off = off + jnp.sum(mj, axis=1, keepdims=True)
        return jnp.concatenate(outs, axis=1), off

    gt = aff > thr_f
    tie = jnp.where(aff == thr_f, 1.0, 0.0)
    n_gt = jnp.sum(jnp.where(gt, 1.0, 0.0), axis=1, keepdims=True)
    tie_rank, _ = exclusive_count(tie)
    sel = jnp.where(gt | ((tie > 0.0) & (tie_rank < CAPACITY - n_gt)), 1.0, 0.0)
    slot, _ = exclusive_count(sel)
    slot_ref[...] = jnp.where(sel > 0.0, slot.astype(I32), -1)
    tok = lax.broadcasted_iota(I32, (SEQ, LANES), 0)
    j = lax.broadcasted_iota(I32, (SEQ, LANES), 1)
    before = jnp.where(tok < j * TOKEN_CHUNK, 1.0, 0.0).astype(BF16)
    offs_ref[...] = jnp.dot(sel.astype(BF16), before, preferred_element_type=F32).astype(I32)


def _expert_slots(aff):
    aff_t = aff[:, :N_EXPERTS].reshape(BATCH, SEQ, N_EXPERTS).transpose(0, 2, 1).reshape(BATCH * N_EXPERTS, SEQ)
    slots, offs = pl.pallas_call(
        _topk_kernel,
        out_shape=[jax.ShapeDtypeStruct((BATCH * N_EXPERTS, SEQ), I32),
                   jax.ShapeDtypeStruct((BATCH * N_EXPERTS, LANES), I32)],
        compiler_params=pltpu.CompilerParams(vmem_limit_bytes=VMEM_LIMIT),
        name="expert_topk",
    )(aff_t)
    return slots, offs[:, :OFFS_STRIDE].reshape(-1)


def _window_start(first, win):
    start = jnp.minimum((first >> SLOT_ALIGN_SHIFT) << SLOT_ALIGN_SHIFT, CAPACITY - win)
    return pl.multiple_of(start, 1 << SLOT_ALIGN_SHIFT)


def _gather_kernel(offs_ref, slot_ref, h_ref, x_ref):
    win = GATHER_WIN
    last = CAPACITY - win
    b = pl.program_id(0)
    e0 = pl.program_id(1) * GATHER_EXPERTS
    j = pl.program_id(2)

    @pl.when(j == 0)
    def _():
        x_ref[...] = jnp.zeros_like(x_ref)

    def bounds(e):
        base = (b * N_EXPERTS + e0 + e) * OFFS_STRIDE + j
        return offs_ref[base], offs_ref[base + 1]

    row = lax.broadcasted_iota(I32, (win, TOKEN_CHUNK), 0)
    h_c = h_ref[...]
    starts = [_window_start(bounds(e)[0], win) for e in range(GATHER_EXPERTS)]
    onehot = jnp.concatenate(
        [jnp.where(row + starts[e] == slot_ref[e:e + 1, :], 1.0, 0.0).astype(BF16)
         for e in range(GATHER_EXPERTS)], axis=0)
    picked = jnp.dot(onehot, h_c, preferred_element_type=F32).astype(BF16)
    for e in range(GATHER_EXPERTS):
        rows = pl.ds(starts[e], win)
        x_ref[e, rows, :] = x_ref[e, rows, :] + picked[e * win:(e + 1) * win]

    for e in range(GATHER_EXPERTS):
        _, end = bounds(e)
        covered = starts[e] + win
        slot_e = slot_ref[e:e + 1, :]

        def extra_window(i, carry):
            lo = covered + i * win
            r0 = pl.multiple_of(jnp.minimum(lo, last), 1 << SLOT_ALIGN_SHIFT)
            hot = jnp.where((row + r0 == slot_e) & (slot_e >= lo), 1.0, 0.0).astype(BF16)
            rows = pl.ds(r0, win)
            x_ref[e, rows, :] = x_ref[e, rows, :] + jnp.dot(hot, h_c, preferred_element_type=F32).astype(BF16)
            return carry

        lax.fori_loop(0, jnp.maximum(end - covered + win - 1, 0) // win, extra_window, 0)


def _gather(offs, slots, h_ext):
    slots = slots.reshape(BATCH, N_EXPERTS, SEQ)
    h_ext = h_ext.reshape(BATCH, SEQ, D_MODEL + LANES)
    return pl.pallas_call(
        _gather_kernel,
        grid_spec=pltpu.PrefetchScalarGridSpec(
            num_scalar_prefetch=1,
            grid=(BATCH, N_EXPERTS // GATHER_EXPERTS, SEQ // TOKEN_CHUNK),
            in_specs=[pl.BlockSpec((None, GATHER_EXPERTS, TOKEN_CHUNK), lambda b, g, j, offs: (b, g, j)),
                      pl.BlockSpec((None, TOKEN_CHUNK, D_MODEL + LANES), lambda b, g, j, offs: (b, j, 0))],
            out_specs=pl.BlockSpec((None, GATHER_EXPERTS, None, CAPACITY, D_MODEL + LANES),
                                   lambda b, g, j, offs: (b // FFN_SEQS, g, b % FFN_SEQS, 0, 0))),
        out_shape=jax.ShapeDtypeStruct((BATCH // FFN_SEQS, N_EXPERTS, FFN_SEQS, CAPACITY, D_MODEL + LANES),
                                       BF16),
        compiler_params=_params("parallel", "parallel", "arbitrary"),
        name="expert_gather",
    )(offs, slots, h_ext)


def _expert_kernel(x_ref, wg_ref, wu_ref, wd_ref, y_ref, wg_s, wu_s, wd_s):
    g = pl.program_id(0)
    part = pl.program_id(1)
    slab = wg_ref.shape[0]

    @pl.when(g < N_EXPERTS)
    def _():
        rows = pl.ds(pl.multiple_of(part * slab, slab), slab)
        wg_s[g % 2, rows, :] = wg_ref[...].astype(BF16)
        wu_s[g % 2, rows, :] = wu_ref[...].astype(BF16)
        wd_s[g % 2, rows, :] = wd_ref[...].astype(BF16)

    @pl.when(g == 0)
    def _():
        y_ref[...] = jnp.zeros_like(y_ref)

    @pl.when(g > 0)
    def _():
        e = g - 1
        cur = e % 2
        rows = x_ref.shape[0] * x_ref.shape[1]
        xin = x_ref[:, :, :D_MODEL].reshape(rows, D_MODEL)
        ext = x_ref[:, :, D_MODEL:].reshape(rows, LANES).astype(F32)
        lane = lax.broadcasted_iota(I32, ext.shape, 1)
        mine = (lane == e) | (lane == e + N_EXPERTS) | (lane == e + 2 * N_EXPERTS)
        gate = jnp.sum(jnp.where(mine, ext, 0.0), axis=1, keepdims=True)

        a = jnp.dot(xin, wg_s[cur], preferred_element_type=F32)
        b = jnp.dot(xin, wu_s[cur], preferred_element_type=F32)
        hid = (a / (1.0 + jnp.exp(-a)) * b).astype(BF16)
        y = (jnp.dot(hid, wd_s[cur], preferred_element_type=F32) * gate).astype(BF16)
        y_ref[...] = y.reshape(y_ref.shape)


def _experts(xin, w_gate, w_up, w_down):
    parts = BATCH // FFN_SEQS
    slab = D_MODEL // parts
    prev = lambda g: jnp.maximum(g - 1, 0)
    wspec = pl.BlockSpec((None, slab, D_MODEL), lambda g, p: (jnp.minimum(g, N_EXPERTS - 1), p, 0))
    return pl.pallas_call(
        _expert_kernel,
        grid=(N_EXPERTS + 1, parts),
        in_specs=[pl.BlockSpec((None, None, FFN_SEQS, CAPACITY, D_MODEL + LANES),
                               lambda g, p: (p, prev(g), 0, 0, 0)),
                  wspec, wspec, wspec],
        out_specs=pl.BlockSpec((None, None, FFN_SEQS, CAPACITY, D_MODEL),
                               lambda g, p: (p, jnp.where(g == 0, N_EXPERTS, g - 1), 0, 0, 0)),
        out_shape=jax.ShapeDtypeStruct((parts, N_EXPERTS + 1, FFN_SEQS, CAPACITY, D_MODEL), BF16),
        scratch_shapes=[pltpu.VMEM((2, D_MODEL, D_MODEL), BF16)] * 3,
        compiler_params=_params("arbitrary", "arbitrary"),
        name="expert_ffn",
    )(xin, w_gate, w_up, w_down)


def _combine_kernel(offs_ref, x2_ref, slot_ref, y_ref, g_ref, o_ref):
    b = pl.program_id(0)
    j = pl.program_id(1)
    win = SCATTER_WIN
    lane = lax.broadcasted_iota(I32, (TOKEN_CHUNK, win), 1)
    slot = slot_ref[...]

    def bounds(e):
        base = (b * N_EXPERTS + e) * OFFS_STRIDE + j
        return offs_ref[base], offs_ref[base + 1]

    starts = [_window_start(bounds(e)[0], win) for e in range(N_EXPERTS)]
    onehot = jnp.concatenate(
        [jnp.where(lane + starts[e] == slot[:, e:e + 1], 1.0, 0.0).astype(BF16) for e in range(N_EXPERTS)],
        axis=1)
    rows = jnp.concatenate([y_ref[e, pl.ds(starts[e], win), :] for e in range(N_EXPERTS)], axis=0)
    x3 = x2_ref[...] + jnp.dot(onehot, rows, preferred_element_type=F32)

    leftover = [bounds(e)[1] > starts[e] + win for e in range(N_EXPERTS)]
    any_leftover = functools.reduce(jnp.logical_or, leftover)

    @pl.when(jnp.logical_not(any_leftover))
    def _():
        o_ref[...] = _rms(x3, g_ref[...])

    @pl.when(any_leftover)
    def _():
        o_ref[...] = x3
        tail_lane = lax.broadcasted_iota(I32, (TOKEN_CHUNK, CAPACITY - win), 1) + win
        for e in range(N_EXPERTS):
            @pl.when(leftover[e])
            def _():
                sl = slot[:, e:e + 1]
                hot = jnp.where((tail_lane == sl) & (sl >= starts[e] + win), 1.0, 0.0).astype(BF16)
                o_ref[...] += jnp.dot(hot, y_ref[e, win:, :], preferred_element_type=F32)

        o_ref[...] = _rms(o_ref[...], g_ref[...])


def _combine(offs, x2, slots, y, g_final):
    slots_t = slots.reshape(BATCH, N_EXPERTS, SEQ).transpose(0, 2, 1)
    x2 = x2.reshape(BATCH, SEQ, D_MODEL)
    tm = TOKEN_CHUNK
    return pl.pallas_call(
        _combine_kernel,
        grid_spec=pltpu.PrefetchScalarGridSpec(
            num_scalar_prefetch=1,
            grid=(BATCH, SEQ // tm),
            in_specs=[pl.BlockSpec((None, tm, D_MODEL), lambda b, i, offs: (b, i, 0)),
                      pl.BlockSpec((None, tm, N_EXPERTS), lambda b, i, offs: (b, i, 0)),
                      pl.BlockSpec((None, N_EXPERTS, None, CAPACITY, D_MODEL),
                                   lambda b, i, offs: (b // FFN_SEQS, 0, b % FFN_SEQS, 0, 0)),
                      pl.BlockSpec((1, D_MODEL), lambda b, i, offs: (0, 0))],
            out_specs=pl.BlockSpec((None, tm, D_MODEL), lambda b, i, offs: (b, i, 0))),
        out_shape=jax.ShapeDtypeStruct((BATCH, SEQ, D_MODEL), F32),
        compiler_params=_params("parallel", "arbitrary"),
        name="combine_final_norm",
    )(offs, x2, slots_t, y, g_final.reshape(1, -1))


def kernel(x, mem, positions, g_mix, w_in, g_q_lat, w_q_up, g_kv_lat, w_kv_up, w_fourier, w_out, g_mem_q,
           g_mem_kv, w_mem_q, w_mem_kv, w_mem_o, g_ffn, w_router, w_exp_gate, w_exp_up, w_exp_down, g_final):
    assert x.shape == (BATCH, SEQ, D_MODEL) and g_mix.shape[0] == 1
    x2d = x.reshape(TOKENS, D_MODEL)
    cos, sin = _rope_tables(positions)
    ab = _channel_mats(w_fourier[0])
    v12, q, k, v, ksq = _in_projection(x2d, g_mix[0], w_in[0], g_q_lat[0], w_q_up[0], g_kv_lat[0], w_kv_up[0],
                                       ab, cos, sin)
    y_f = _sequence_dft(v12)
    y_a = _mla_attention(q, k, v, ksq)
    mk, mv = _memory_kv(mem, g_mem_kv[0], w_mem_kv[0])
    x2, h_ext, aff = _mixing(x2d, y_f, y_a, w_out[0], g_mem_q[0], w_mem_q[0], mk, mv, w_mem_o[0], g_ffn[0],
                             w_router[0])
    slots, offs = _expert_slots(aff)
    xin = _gather(offs, slots, h_ext)
    y = _experts(xin, w_exp_gate[0], w_exp_up[0], w_exp_down[0])
    return _combine(offs, x2, slots, y, g_final)
```

```python
import functools

import numpy as np
import jax
import jax.numpy as jnp
from jax import lax
from jax.experimental import pallas as pl
from jax.experimental.pallas import tpu as pltpu

F32 = jnp.float32
BF16 = jnp.bfloat16
I32 = jnp.int32

D_MODEL = 1024
BATCH = 4
SEQ = 4096
TOKENS = BATCH * SEQ
MEM_LEN = 256
RMS_EPS = 1e-6
F_GROUPS = 8
F_GROUP_DIM = 64
F_WIDTH = F_GROUPS * F_GROUP_DIM
MLA_HEADS = 8
QK_NOPE_DIM = 64
QK_ROPE_DIM = 32
V_HEAD_DIM = 64
Q_LORA_RANK = 384
KV_LORA_RANK = 256
ROPE_THETA = 10000.0
MEM_HEADS = 4
MEM_HEAD_DIM = D_MODEL // MEM_HEADS
N_EXPERTS = 16
CAPACITY = 2 * SEQ // N_EXPERTS

NORM_SLACK = 1.01
SAFE_SHIFT = 30.0

LANES = 128
HEAD_PAD = 128
IN_PAD = 1280
HALF = SEQ // 2
QUARTER = SEQ // 4
VMEM_LIMIT = 56 * 1024 * 1024

TM_IN = 1024
IN_GROUPS = 1
TM_MIX = 1024
MIX_GROUPS = 2
TQ = 1024
ATTN_HEADS = 2
TOKEN_CHUNK = 512
OFFS_STRIDE = SEQ // TOKEN_CHUNK + 1
GATHER_WIN = 96
SCATTER_WIN = 128
SLOT_ALIGN_SHIFT = 4
GATHER_EXPERTS = 16
FFN_SEQS = 2


def _rms(x, g):
    return x * lax.rsqrt(jnp.mean(x * x, axis=-1, keepdims=True) + RMS_EPS) * g


def _params(*sem):
    return pltpu.CompilerParams(dimension_semantics=sem, vmem_limit_bytes=VMEM_LIMIT)


def _rope_kernel(pos_ref, freq_ref, cos_ref, sin_ref):
    ang = pos_ref[...] * freq_ref[...]
    cos_ref[...] = jnp.cos(ang)
    sin_ref[...] = jnp.sin(ang)


def _rope_tables(positions):
    half = QK_ROPE_DIM // 2
    freqs = 1.0 / (ROPE_THETA ** (jnp.arange(0, QK_ROPE_DIM, 2, dtype=F32) / QK_ROPE_DIM))
    rows = TOKENS * half // LANES
    pos = jnp.repeat(positions.astype(F32).reshape(-1), half).reshape(rows, LANES)
    freq = jnp.tile(freqs, LANES // half).reshape(1, LANES)
    cos, sin = pl.pallas_call(
        _rope_kernel,
        out_shape=(jax.ShapeDtypeStruct((rows, LANES), F32),) * 2,
        name="rope_tables",
    )(pos, freq)
    return cos.reshape(TOKENS, half), sin.reshape(TOKENS, half)


def _rope_expanders():
    half = QK_ROPE_DIM // 2
    lane = np.arange(LANES)
    hit = (lane[None, :] % half) == np.arange(half)[:, None]
    is_sin = (lane // QK_ROPE_DIM) % 2 == 1
    return (jnp.asarray((hit & ~is_sin[None, :]).astype(np.float32)).astype(BF16),
            jnp.asarray((hit & is_sin[None, :]).astype(np.float32)).astype(BF16))


def _chan_kernel(cbd_ref, sbd_ref, w_ref, ab_ref):
    w = w_ref[...]
    ab_ref[:, :F_WIDTH] = jnp.dot(cbd_ref[...], w, precision=lax.Precision.HIGHEST,
                                  preferred_element_type=F32).astype(BF16)
    ab_ref[:, F_WIDTH:] = jnp.dot(sbd_ref[...], w, precision=lax.Precision.HIGHEST,
                                  preferred_element_type=F32).astype(BF16)


def _channel_mats(w_fourier):
    c = np.arange(F_GROUP_DIM)
    ang = 2.0 * np.pi * np.outer(c, c) / F_GROUP_DIM
    scale = F_GROUP_DIM ** -0.5
    eye = np.eye(F_GROUPS)
    cbd = np.kron(eye, np.cos(ang) * scale).astype(np.float32)
    sbd = np.kron(eye, np.sin(ang) * scale).astype(np.float32)
    wbd = (jnp.eye(F_GROUPS, dtype=F32)[:, None, :, None] * w_fourier[:, :, None, :]).reshape(F_WIDTH, F_WIDTH)
    return pl.pallas_call(
        _chan_kernel,
        out_shape=jax.ShapeDtypeStruct((F_WIDTH, 2 * F_WIDTH), BF16),
        name="channel_dft_fold",
    )(jnp.asarray(cbd), jnp.asarray(sbd), wbd)


def _inproj_kernel(x_ref, gmix_ref, win_ref, gq_ref, wq_ref, gkv_ref, wkv_ref, ab_ref, cos_ref, sin_ref,
                   ec_ref, es_ref, hsum_ref, v12_ref, q_ref, k_ref, v_ref, ksq_ref, vscr_ref):
    rows_per_group = x_ref.shape[0] // IN_GROUPS
    half_rows = rows_per_group // 2
    ksq_max = None
    for grp in range(IN_GROUPS):
        r0 = grp * rows_per_group
        rows = pl.ds(r0, rows_per_group)
        half = pl.ds(r0 // 2, half_rows)
        h = _rms(x_ref[rows, :], gmix_ref[...]).astype(BF16)
        u = jnp.dot(h, win_ref[...], preferred_element_type=F32)

        n_pairs = ab_ref.shape[0]
        for gp in range(n_pairs):
            pair = jnp.dot(u[:, LANES * gp:LANES * (gp + 1)].astype(BF16), ab_ref[gp],
                           preferred_element_type=F32)
            for part in range(2):
                c = part * n_pairs + gp
                cols = slice(LANES * c, LANES * (c + 1))
                vscr_ref[c, rows, :] = pair[:, LANES * part:LANES * (part + 1)]
                v12_ref[0, half, cols] = vscr_ref[c, pl.ds(r0, half_rows, stride=2), :].astype(BF16)
                v12_ref[1, half, cols] = vscr_ref[c, pl.ds(r0 + 1, half_rows, stride=2), :].astype(BF16)

        tab = None
        for col_ref, exp_ref in ((cos_ref, ec_ref), (sin_ref, es_ref)):
            col = col_ref[rows, :]
            hi = col.astype(BF16)
            lo = (col - hi.astype(F32)).astype(BF16)
            term = (jnp.dot(hi, exp_ref[...], preferred_element_type=F32)
                    + jnp.dot(lo, exp_ref[...], preferred_element_type=F32))
            tab = term if tab is None else tab + term
        lane = lax.broadcasted_iota(I32, tab.shape, 1)
        rope0, rope1 = QK_NOPE_DIM, QK_NOPE_DIM + QK_ROPE_DIM
        back = LANES - QK_ROPE_DIM
        c1 = jnp.where(lane < rope0, 1.0, jnp.where(lane < rope1, tab, 0.0))
        c2 = jnp.where(lane >= rope1, tab, 0.0)

        q0 = F_WIDTH
        qn = _rms(u[:, q0:q0 + Q_LORA_RANK], gq_ref[...]).astype(BF16)
        qa = jnp.dot(qn, wq_ref[...], preferred_element_type=F32)
        q_blocks = []
        for hd in range(MLA_HEADS):
            blk = qa[:, HEAD_PAD * hd:HEAD_PAD * (hd + 1)]
            q_blocks.append(blk * c1 + pltpu.roll(blk * c2, back, 1))
            q_ref[rows, HEAD_PAD * hd:HEAD_PAD * (hd + 1)] = q_blocks[hd].astype(BF16)

        kv0 = q0 + Q_LORA_RANK
        kvn = _rms(u[:, kv0:kv0 + KV_LORA_RANK], gkv_ref[...]).astype(BF16)
        kv = jnp.dot(kvn, wkv_ref[...], preferred_element_type=F32)
        kr0 = kv0 + KV_LORA_RANK
        t = u[:, kr0:kr0 + LANES] * jnp.where(lane < 2 * QK_ROPE_DIM, tab, 0.0)
        kr = t + pltpu.roll(t, back, 1)
        kr = jnp.where((lane >= rope0) & (lane < rope1), pltpu.roll(kr, rope0, 1), 0.0)
        k_blocks = [kv[:, HEAD_PAD * hd:HEAD_PAD * (hd + 1)] + kr for hd in range(MLA_HEADS)]
        for hd in range(MLA_HEADS):
            k_ref[rows, HEAD_PAD * hd:HEAD_PAD * (hd + 1)] = k_blocks[hd].astype(BF16)
        v_ref[rows, :] = kv[:, MLA_HEADS * HEAD_PAD:].astype(BF16)
        sq = jnp.concatenate(
            [jnp.concatenate([(blk * blk).astype(BF16) for blk in blocks], axis=1)
             for blocks in (k_blocks, q_blocks)], axis=0)
        sq = jnp.dot(sq, hsum_ref[...], preferred_element_type=F32)
        grp_max = jnp.concatenate([jnp.max(sq[:rows_per_group], axis=0, keepdims=True),
                                   jnp.max(sq[rows_per_group:], axis=0, keepdims=True)], axis=0)
        ksq_max = grp_max if ksq_max is None else jnp.maximum(ksq_max, grp_max)
    ksq_ref[...] = ksq_max


def _in_projection(x2d, g_mix, w_in, g_q, w_q_up, g_kv, w_kv_up, ab, cos, sin):
    w_kr = w_in[:, -QK_ROPE_DIM:]
    half = QK_ROPE_DIM // 2
    w_kr_rot = jnp.concatenate([-w_kr[:, half:], w_kr[:, :half]], axis=1)
    win = jnp.concatenate(
        [w_in, w_kr_rot, jnp.zeros((D_MODEL, IN_PAD - w_in.shape[1] - QK_ROPE_DIM), F32)], axis=1).astype(BF16)
    scale = (QK_NOPE_DIM + QK_ROPE_DIM) ** -0.5
    wq = w_q_up.reshape(Q_LORA_RANK, MLA_HEADS, QK_NOPE_DIM + QK_ROPE_DIM) * scale
    wq_rope = wq[:, :, QK_NOPE_DIM:]
    wq_rot = jnp.concatenate([-wq_rope[:, :, half:], wq_rope[:, :, :half]], axis=2)
    wq = jnp.concatenate([wq, wq_rot], axis=2).reshape(Q_LORA_RANK, MLA_HEADS * HEAD_PAD).astype(BF16)
    wkv = w_kv_up.reshape(KV_LORA_RANK, MLA_HEADS, QK_NOPE_DIM + V_HEAD_DIM)
    wk = jnp.concatenate([wkv[:, :, :QK_NOPE_DIM],
                          jnp.zeros((KV_LORA_RANK, MLA_HEADS, HEAD_PAD - QK_NOPE_DIM), F32)], axis=2)
    wkv = jnp.concatenate([wk.reshape(KV_LORA_RANK, MLA_HEADS * HEAD_PAD),
                           wkv[:, :, QK_NOPE_DIM:].reshape(KV_LORA_RANK, MLA_HEADS * V_HEAD_DIM)],
                          axis=1).astype(BF16)

    ab = jnp.stack([jnp.concatenate([ab[LANES * g:LANES * (g + 1), LANES * g:LANES * (g + 1)],
                                     ab[LANES * g:LANES * (g + 1), F_WIDTH + LANES * g:F_WIDTH + LANES * (g + 1)]],
                                    axis=1) for g in range(F_WIDTH // LANES)])
    tm = TM_IN
    ec, es = _rope_expanders()
    hsum = jnp.asarray((np.arange(MLA_HEADS * HEAD_PAD)[:, None] // HEAD_PAD
                        == np.arange(LANES)[None, :]).astype(np.float32)).astype(BF16)
    full = lambda shape: pl.BlockSpec(shape, lambda i: (0,) * len(shape))
    tile = lambda w: pl.BlockSpec((tm, w), lambda i: (i, 0))
    per_half = HALF // tm
    per_seq = SEQ // tm
    v12_spec = pl.BlockSpec((None, 2, None, tm // 2, 2 * F_WIDTH),
                            lambda i: (i // per_seq, 0, (i % per_seq) // per_half, i % per_half, 0))
    return pl.pallas_call(
        _inproj_kernel,
        grid=(TOKENS // tm,),
        in_specs=[tile(D_MODEL), full((1, D_MODEL)), full(win.shape), full((1, Q_LORA_RANK)), full(wq.shape),
                  full((1, KV_LORA_RANK)), full(wkv.shape), full(ab.shape), tile(QK_ROPE_DIM // 2),
                  tile(QK_ROPE_DIM // 2), full(ec.shape), full(es.shape), full(hsum.shape)],
        out_specs=[v12_spec, tile(MLA_HEADS * HEAD_PAD), tile(MLA_HEADS * HEAD_PAD),
                   tile(MLA_HEADS * V_HEAD_DIM), pl.BlockSpec((None, 2, LANES), lambda i: (i, 0, 0))],
        out_shape=[jax.ShapeDtypeStruct((BATCH, 2, 2, QUARTER, 2 * F_WIDTH), BF16),
                   jax.ShapeDtypeStruct((TOKENS, MLA_HEADS * HEAD_PAD), BF16),
                   jax.ShapeDtypeStruct((TOKENS, MLA_HEADS * HEAD_PAD), BF16),
                   jax.ShapeDtypeStruct((TOKENS, MLA_HEADS * V_HEAD_DIM), BF16),
                   jax.ShapeDtypeStruct((TOKENS // tm, 2, LANES), F32)],
        scratch_shapes=[pltpu.VMEM((2 * F_WIDTH // LANES, tm, LANES), F32)],
        compiler_params=_params("parallel"),
        name="in_projection",
    )(x2d, g_mix.reshape(1, -1), win, g_q.reshape(1, -1), wq, g_kv.reshape(1, -1), wkv, ab, cos, sin, ec, es, hsum)


def _seq_dft_kernel(v_ref, m_ref, y_ref):
    sign = jnp.where(pl.program_id(1) == 0, 1.0, -1.0)
    parts = []
    for q in range(2):
        ab = (v_ref[q, 0].astype(F32) + sign * v_ref[q, 1].astype(F32)).astype(BF16)
        parts.append(jnp.dot(m_ref[q, :, :QUARTER], ab[:, :F_WIDTH], preferred_element_type=F32)
                     + jnp.dot(m_ref[q, :, QUARTER:], ab[:, F_WIDTH:], preferred_element_type=F32))
    y_ref[0] = (parts[0] + parts[1]).astype(BF16)
    y_ref[1] = (parts[0] - parts[1]).astype(BF16)


def _seq_dft_mats():
    i = np.arange(QUARTER)
    out = np.zeros((2, 2, QUARTER, 2 * QUARTER), np.float32)
    for p in range(2):
        for q in range(2):
            prod = np.outer(2 * i + p, 2 * i + q) % SEQ
            ang = 2.0 * np.pi * prod / SEQ
            out[p, q, :, :QUARTER] = np.cos(ang) / np.sqrt(SEQ)
            out[p, q, :, QUARTER:] = -np.sin(ang) / np.sqrt(SEQ)
    return out


def _sequence_dft(v):
    mats = jnp.asarray(_seq_dft_mats()).astype(BF16)
    return pl.pallas_call(
        _seq_dft_kernel,
        grid=(BATCH, 2),
        in_specs=[pl.BlockSpec((None, 2, 2, QUARTER, 2 * F_WIDTH), lambda b, p: (b, 0, 0, 0, 0)),
                  pl.BlockSpec((None, 2, QUARTER, 2 * QUARTER), lambda b, p: (p, 0, 0, 0))],
        out_specs=pl.BlockSpec((None, None, 2, QUARTER, F_WIDTH), lambda b, p: (b, p, 0, 0, 0)),
        out_shape=jax.ShapeDtypeStruct((BATCH, 2, 2, QUARTER, F_WIDTH), BF16),
        compiler_params=_params("parallel", "arbitrary"),
        name="sequence_dft",
    )(v, mats)


def _mla_kernel(q_ref, k_ref, v_ref, sq_ref, o_ref):
    v = v_ref[...]
    hg = pl.program_id(1)
    sq = jnp.max(sq_ref[...], axis=0)
    bounds = jnp.sqrt(sq[0:1] * sq[1:2]) * NORM_SLACK
    head_lane = lax.broadcasted_iota(I32, bounds.shape, 1)
    head_bounds = [jnp.max(jnp.where(head_lane == ATTN_HEADS * hg + j, bounds, 0.0), axis=1, keepdims=True)
                   for j in range(ATTN_HEADS)]
    safe = jnp.max(functools.reduce(jnp.maximum, head_bounds)) <= SAFE_SHIFT

    def attend(j, row_shift):
        cols = slice(HEAD_PAD * j, HEAD_PAD * (j + 1))
        out_cols = slice(V_HEAD_DIM * j, V_HEAD_DIM * (j + 1))
        s = lax.dot_general(q_ref[:, cols], k_ref[:, cols], (((1,), (1,)), ((), ())),
                            preferred_element_type=F32)
        p = jnp.exp(s - row_shift(s))
        l = jnp.sum(p, axis=1, keepdims=True)
        o = jnp.dot(p.astype(BF16), v, preferred_element_type=F32) / l
        o_ref[:, out_cols] = o[:, out_cols].astype(BF16)

    @pl.when(safe)
    def _():
        for j in range(ATTN_HEADS):
            attend(j, lambda s: head_bounds[j])

    @pl.when(jnp.logical_not(safe))
    def _():
        for j in range(ATTN_HEADS):
            attend(j, lambda s: jnp.max(s, axis=1, keepdims=True))


def _mla_attention(q, k, v, ksq):
    q = q.reshape(BATCH, SEQ, -1)
    k = k.reshape(BATCH, SEQ, -1)
    v = v.reshape(BATCH, SEQ, -1)
    ksq = ksq.reshape(BATCH, SEQ // TM_IN, 2, LANES)
    out = pl.pallas_call(
        _mla_kernel,
        grid=(BATCH, MLA_HEADS // ATTN_HEADS, SEQ // TQ),
        in_specs=[pl.BlockSpec((None, TQ, ATTN_HEADS * HEAD_PAD), lambda b, hg, i: (b, i, hg)),
                  pl.BlockSpec((None, SEQ, ATTN_HEADS * HEAD_PAD), lambda b, hg, i: (b, 0, hg)),
                  pl.BlockSpec((None, SEQ, ATTN_HEADS * V_HEAD_DIM), lambda b, hg, i: (b, 0, hg)),
                  pl.BlockSpec((None, SEQ // TM_IN, 2, LANES), lambda b, hg, i: (b, 0, 0, 0))],
        out_specs=pl.BlockSpec((None, TQ, ATTN_HEADS * V_HEAD_DIM), lambda b, hg, i: (b, i, hg)),
        out_shape=jax.ShapeDtypeStruct((BATCH, SEQ, MLA_HEADS * V_HEAD_DIM), BF16),
        compiler_params=_params("parallel", "parallel", "arbitrary"),
        name="mla_attention",
    )(q, k, v, ksq)
    return out.reshape(TOKENS, MLA_HEADS * V_HEAD_DIM)


def _memkv_kernel(mem_ref, g_ref, w_ref, k_ref, v_ref):
    mn = _rms(mem_ref[...], g_ref[...]).astype(BF16)
    kv = jnp.dot(mn, w_ref[...], preferred_element_type=F32)
    k_ref[...] = kv[:, :D_MODEL].astype(BF16)
    v_ref[...] = kv[:, D_MODEL:].astype(BF16)


def _memory_kv(mem, g_mem_kv, w_mem_kv):
    blk = pl.BlockSpec((None, MEM_LEN, D_MODEL), lambda b: (b, 0, 0))
    return pl.pallas_call(
        _memkv_kernel,
        grid=(BATCH,),
        in_specs=[blk, pl.BlockSpec((1, D_MODEL), lambda b: (0, 0)),
                  pl.BlockSpec((D_MODEL, 2 * D_MODEL), lambda b: (0, 0))],
        out_specs=[blk, blk],
        out_shape=[jax.ShapeDtypeStruct((BATCH, MEM_LEN, D_MODEL), BF16)] * 2,
        compiler_params=_params("parallel"),
        name="memory_kv",
    )(mem, g_mem_kv.reshape(1, -1), w_mem_kv.astype(BF16))


def _mix_kernel(x_ref, yf_ref, ya_ref, wo_ref, gq_ref, wmq_ref, mk_ref, mv_ref, wmo_ref, gf_ref, wr_ref,
                x2_ref, hext_ref, aff_ref, zscr_ref):
    tm = x_ref.shape[0]
    rows_per_group = tm // MIX_GROUPS
    for grp in range(MIX_GROUPS):
        rows = pl.ds(grp * rows_per_group, rows_per_group)
        half = pl.ds(grp * rows_per_group // 2, rows_per_group // 2)
        wo_f = wo_ref[:F_WIDTH, :]
        z_even = jnp.dot(yf_ref[0, half, :], wo_f, preferred_element_type=F32)
        z_odd = jnp.dot(yf_ref[1, half, :], wo_f, preferred_element_type=F32)
        for c in range(zscr_ref.shape[0]):
            cols = slice(LANES * c, LANES * (c + 1))
            zscr_ref[c, pl.ds(grp * rows_per_group, rows_per_group // 2, stride=2), :] = z_even[:, cols]
            zscr_ref[c, pl.ds(grp * rows_per_group + 1, rows_per_group // 2, stride=2), :] = z_odd[:, cols]
        z = jnp.concatenate([zscr_ref[c, rows, :] for c in range(zscr_ref.shape[0])], axis=1)
        x1 = x_ref[rows, :] + z + jnp.dot(ya_ref[rows, :], wo_ref[F_WIDTH:, :], preferred_element_type=F32)

        hq = _rms(x1, gq_ref[...]).astype(BF16)
        qm = (jnp.dot(hq, wmq_ref[...], preferred_element_type=F32) * (MEM_HEAD_DIM ** -0.5)).astype(BF16)
        heads = []
        for hd in range(MEM_HEADS):
            sl = slice(MEM_HEAD_DIM * hd, MEM_HEAD_DIM * (hd + 1))
            s = lax.dot_general(qm[:, sl], mk_ref[:, sl], (((1,), (1,)), ((), ())),
                                preferred_element_type=F32)
            p = jnp.exp(s - jnp.max(s, axis=1, keepdims=True))
            l = jnp.sum(p, axis=1, keepdims=True)
            heads.append((jnp.dot(p.astype(BF16), mv_ref[:, sl], preferred_element_type=F32) / l).astype(BF16))
        o = jnp.concatenate(heads, axis=1)
        x2 = x1 + jnp.dot(o, wmo_ref[...], preferred_element_type=F32)
        x2_ref[rows, :] = x2

        h3 = _rms(x2, gf_ref[...])
        h3_hi = h3.astype(BF16)
        hext_ref[rows, :D_MODEL] = h3_hi
        h3_lo = (h3 - h3_hi.astype(F32)).astype(BF16)
        hi_terms = jnp.dot(h3_hi, wr_ref[...], preferred_element_type=F32)
        logits = (hi_terms[:, :LANES] + hi_terms[:, LANES:]
                  + jnp.dot(h3_lo, wr_ref[:, :LANES], preferred_element_type=F32))
        lane = lax.broadcasted_iota(I32, logits.shape, 1)
        logits = jnp.where(lane < N_EXPERTS, logits, -jnp.inf)
        e = jnp.exp(logits - jnp.max(logits, axis=1, keepdims=True))
        aff = e / jnp.sum(e, axis=1, keepdims=True)
        aff_ref[rows, :] = aff
        hi = aff.astype(BF16)
        r1 = aff - hi.astype(F32)
        mid = r1.astype(BF16)
        lo = (r1 - mid.astype(F32)).astype(BF16)
        hext_ref[rows, D_MODEL:] = jnp.where(
            lane < N_EXPERTS, hi,
            jnp.where(lane < 2 * N_EXPERTS, pltpu.roll(mid.astype(F32), N_EXPERTS, 1).astype(BF16),
                      pltpu.roll(lo.astype(F32), 2 * N_EXPERTS, 1).astype(BF16)))


def _mixing(x2d, y_f, y_a, w_out, g_mem_q, w_mem_q, mk, mv, w_mem_o, g_ffn, w_router):
    tm = TM_MIX
    wr = jnp.concatenate([w_router, jnp.zeros((D_MODEL, LANES - N_EXPERTS), F32)], axis=1)
    wr_hi = wr.astype(BF16)
    wr_cat = jnp.concatenate([wr_hi, (wr - wr_hi.astype(F32)).astype(BF16)], axis=1)
    full = lambda shape: pl.BlockSpec(shape, lambda i: (0,) * len(shape))
    tile = lambda w: pl.BlockSpec((tm, w), lambda i: (i, 0))
    per_half = HALF // tm
    per_seq = SEQ // tm
    per_batch = pl.BlockSpec((None, MEM_LEN, D_MODEL), lambda i: (i // per_seq, 0, 0))
    yf_spec = pl.BlockSpec((None, 2, None, tm // 2, F_WIDTH),
                           lambda i: (i // per_seq, 0, (i % per_seq) // per_half, i % per_half, 0))
    return pl.pallas_call(
        _mix_kernel,
        grid=(TOKENS // tm,),
        in_specs=[tile(D_MODEL), yf_spec, tile(F_WIDTH), full((D_MODEL, D_MODEL)), full((1, D_MODEL)),
                  full((D_MODEL, D_MODEL)), per_batch, per_batch, full((D_MODEL, D_MODEL)), full((1, D_MODEL)),
                  full((D_MODEL, 2 * LANES))],
        out_specs=[tile(D_MODEL), tile(D_MODEL + LANES), tile(LANES)],
        out_shape=[jax.ShapeDtypeStruct((TOKENS, D_MODEL), F32),
                   jax.ShapeDtypeStruct((TOKENS, D_MODEL + LANES), BF16),
                   jax.ShapeDtypeStruct((TOKENS, LANES), F32)],
        scratch_shapes=[pltpu.VMEM((D_MODEL // LANES, tm, LANES), F32)],
        compiler_params=_params("parallel"),
        name="mix_memattn_router",
    )(x2d, y_f, y_a, w_out.astype(BF16), g_mem_q.reshape(1, -1), w_mem_q.astype(BF16), mk, mv,
      w_mem_o.astype(BF16), g_ffn.reshape(1, -1), wr_cat)


def _topk_kernel(aff_ref, slot_ref, offs_ref):
    aff = aff_ref[...]
    rows = aff.shape[0]

    thr = jnp.zeros((rows, 1), I32)
    for bit in range(30, -1, -1):
        cand = thr | (1 << bit)
        cnt = jnp.sum(jnp.where(aff >= pltpu.bitcast(cand, F32), 1.0, 0.0), axis=1, keepdims=True)
        thr = jnp.where(cnt >= CAPACITY, cand, thr)
    thr_f = pltpu.bitcast(thr, F32)

    chunk = 256
    r = lax.broadcasted_iota(I32, (chunk, chunk), 0)
    c = lax.broadcasted_iota(I32, (chunk, chunk), 1)
    tri = jnp.where(r < c, 1.0, 0.0).astype(BF16)

    def exclusive_count(mask):
        off = jnp.zeros((rows, 1), F32)
        outs = []
        for j in range(SEQ // chunk):
            mj = mask[:, chunk * j:chunk * (j + 1)]
            outs.append(jnp.dot(mj.astype(BF16), tri, preferred_element_type=F32) + off)
            off = off + jnp.sum(mj, axis=1, keepdims=True)
        return jnp.concatenate(outs, axis=1), off

    gt = aff > thr_f
    tie = jnp.where(aff == thr_f, 1.0, 0.0)
    n_gt = jnp.sum(jnp.where(gt, 1.0, 0.0), axis=1, keepdims=True)
    tie_rank, _ = exclusive_count(tie)
    sel = jnp.where(gt | ((tie > 0.0) & (tie_rank < CAPACITY - n_gt)), 1.0, 0.0)
    slot, _ = exclusive_count(sel)
    slot_ref[...] = jnp.where(sel > 0.0, slot.astype(I32), -1)
    tok = lax.broadcasted_iota(I32, (SEQ, LANES), 0)
    j = lax.broadcasted_iota(I32, (SEQ, LANES), 1)
    before = jnp.where(tok < j * TOKEN_CHUNK, 1.0, 0.0).astype(BF16)
    offs_ref[...] = jnp.dot(sel.astype(BF16), before, preferred_element_type=F32).astype(I32)


def _expert_slots(aff):
    aff_t = aff[:, :N_EXPERTS].reshape(BATCH, SEQ, N_EXPERTS).transpose(0, 2, 1).reshape(BATCH * N_EXPERTS, SEQ)
    slots, offs = pl.pallas_call(
        _topk_kernel,
        out_shape=[jax.ShapeDtypeStruct((BATCH * N_EXPERTS, SEQ), I32),
                   jax.ShapeDtypeStruct((BATCH * N_EXPERTS, LANES), I32)],
        compiler_params=pltpu.CompilerParams(vmem_limit_bytes=VMEM_LIMIT),
        name="expert_topk",
    )(aff_t)
    return slots, offs[:, :OFFS_STRIDE].reshape(-1)


def _window_start(first, win):
    start = jnp.minimum((first >> SLOT_ALIGN_SHIFT) << SLOT_ALIGN_SHIFT, CAPACITY - win)
    return pl.multiple_of(start, 1 << SLOT_ALIGN_SHIFT)


def _gather_kernel(offs_ref, slot_ref, h_ref, x_ref):
    win = GATHER_WIN
    last = CAPACITY - win
    b = pl.program_id(0)
    e0 = pl.program_id(1) * GATHER_EXPERTS
    j = pl.program_id(2)

    @pl.when(j == 0)
    def _():
        x_ref[...] = jnp.zeros_like(x_ref)

    def bounds(e):
        base = (b * N_EXPERTS + e0 + e) * OFFS_STRIDE + j
        return offs_ref[base], offs_ref[base + 1]

    row = lax.broadcasted_iota(I32, (win, TOKEN_CHUNK), 0)
    h_c = h_ref[...]
    starts = [_window_start(bounds(e)[0], win) for e in range(GATHER_EXPERTS)]
    onehot = jnp.concatenate(
        [jnp.where(row + starts[e] == slot_ref[e:e + 1, :], 1.0, 0.0).astype(BF16)
         for e in range(GATHER_EXPERTS)], axis=0)
    picked = jnp.dot(onehot, h_c, preferred_element_type=F32).astype(BF16)
    for e in range(GATHER_EXPERTS):
        rows = pl.ds(starts[e], win)
        x_ref[e, rows, :] = x_ref[e, rows, :] + picked[e * win:(e + 1) * win]

    for e in range(GATHER_EXPERTS):
        _, end = bounds(e)
        covered = starts[e] + win
        slot_e = slot_ref[e:e + 1, :]

        def extra_window(i, carry):
            lo = covered + i * win
            r0 = pl.multiple_of(jnp.minimum(lo, last), 1 << SLOT_ALIGN_SHIFT)
            hot = jnp.where((row + r0 == slot_e) & (slot_e >= lo), 1.0, 0.0).astype(BF16)
            rows = pl.ds(r0, win)
            x_ref[e, rows, :] = x_ref[e, rows, :] + jnp.dot(hot, h_c, preferred_element_type=F32).astype(BF16)
            return carry

        lax.fori_loop(0, jnp.maximum(end - covered + win - 1, 0) // win, extra_window, 0)


def _gather(offs, slots, h_ext):
    slots = slots.reshape(BATCH, N_EXPERTS, SEQ)
    h_ext = h_ext.reshape(BATCH, SEQ, D_MODEL + LANES)
    return pl.pallas_call(
        _gather_kernel,
        grid_spec=pltpu.PrefetchScalarGridSpec(
            num_scalar_prefetch=1,
            grid=(BATCH, N_EXPERTS // GATHER_EXPERTS, SEQ // TOKEN_CHUNK),
            in_specs=[pl.BlockSpec((None, GATHER_EXPERTS, TOKEN_CHUNK), lambda b, g, j, offs: (b, g, j)),
                      pl.BlockSpec((None, TOKEN_CHUNK, D_MODEL + LANES), lambda b, g, j, offs: (b, j, 0))],
            out_specs=pl.BlockSpec((None, GATHER_EXPERTS, None, CAPACITY, D_MODEL + LANES),
                                   lambda b, g, j, offs: (b // FFN_SEQS, g, b % FFN_SEQS, 0, 0))),
        out_shape=jax.ShapeDtypeStruct((BATCH // FFN_SEQS, N_EXPERTS, FFN_SEQS, CAPACITY, D_MODEL + LANES),
                                       BF16),
        compiler_params=_params("parallel", "parallel", "arbitrary"),
        name="expert_gather",
    )(offs, slots, h_ext)


def _expert_kernel(x_ref, wg_ref, wu_ref, wd_ref, y_ref, wg_s, wu_s, wd_s):
    g = pl.program_id(0)
    part = pl.program_id(1)
    slab = wg_ref.shape[0]

    @pl.when(g < N_EXPERTS)
    def _():
        rows = pl.ds(pl.multiple_of(part * slab, slab), slab)
        wg_s[g % 2, rows, :] = wg_ref[...].astype(BF16)
        wu_s[g % 2, rows, :] = wu_ref[...].astype(BF16)
        wd_s[g % 2, rows, :] = wd_ref[...].astype(BF16)

    @pl.when(g == 0)
    def _():
        y_ref[...] = jnp.zeros_like(y_ref)

    @pl.when(g > 0)
    def _():
        e = g - 1
        cur = e % 2
        rows = x_ref.shape[0] * x_ref.shape[1]
        xin = x_ref[:, :, :D_MODEL].reshape(rows, D_MODEL)
        ext = x_ref[:, :, D_MODEL:].reshape(rows, LANES).astype(F32)
        lane = lax.broadcasted_iota(I32, ext.shape, 1)
        mine = (lane == e) | (lane == e + N_EXPERTS) | (lane == e + 2 * N_EXPERTS)
        gate = jnp.sum(jnp.where(mine, ext, 0.0), axis=1, keepdims=True)

        a = jnp.dot(xin, wg_s[cur], preferred_element_type=F32)
        b = jnp.dot(xin, wu_s[cur], preferred_element_type=F32)
        hid = (a / (1.0 + jnp.exp(-a)) * b).astype(BF16)
        y = (jnp.dot(hid, wd_s[cur], preferred_element_type=F32) * gate).astype(BF16)
        y_ref[...] = y.reshape(y_ref.shape)


def _experts(xin, w_gate, w_up, w_down):
    parts = BATCH // FFN_SEQS
    slab = D_MODEL // parts
    prev = lambda g: jnp.maximum(g - 1, 0)
    wspec = pl.BlockSpec((None, slab, D_MODEL), lambda g, p: (jnp.minimum(g, N_EXPERTS - 1), p, 0))
    return pl.pallas_call(
        _expert_kernel,
        grid=(N_EXPERTS + 1, parts),
        in_specs=[pl.BlockSpec((None, None, FFN_SEQS, CAPACITY, D_MODEL + LANES),
                               lambda g, p: (p, prev(g), 0, 0, 0)),
                  wspec, wspec, wspec],
        out_specs=pl.BlockSpec((None, None, FFN_SEQS, CAPACITY, D_MODEL),
                               lambda g, p: (p, jnp.where(g == 0, N_EXPERTS, g - 1), 0, 0, 0)),
        out_shape=jax.ShapeDtypeStruct((parts, N_EXPERTS + 1, FFN_SEQS, CAPACITY, D_MODEL), BF16),
        scratch_shapes=[pltpu.VMEM((2, D_MODEL, D_MODEL), BF16)] * 3,
        compiler_params=_params("arbitrary", "arbitrary"),
        name="expert_ffn",
    )(xin, w_gate, w_up, w_down)


def _combine_kernel(offs_ref, x2_ref, slot_ref, y_ref, g_ref, o_ref):
    b = pl.program_id(0)
    j = pl.program_id(1)
    win = SCATTER_WIN
    lane = lax.broadcasted_iota(I32, (TOKEN_CHUNK, win), 1)
    slot = slot_ref[...]

    def bounds(e):
        base = (b * N_EXPERTS + e) * OFFS_STRIDE + j
        return offs_ref[base], offs_ref[base + 1]

    starts = [_window_start(bounds(e)[0], win) for e in range(N_EXPERTS)]
    onehot = jnp.concatenate(
        [jnp.where(lane + starts[e] == slot[:, e:e + 1], 1.0, 0.0).astype(BF16) for e in range(N_EXPERTS)],
        axis=1)
    rows = jnp.concatenate([y_ref[e, pl.ds(starts[e], win), :] for e in range(N_EXPERTS)], axis=0)
    x3 = x2_ref[...] + jnp.dot(onehot, rows, preferred_element_type=F32)

    leftover = [bounds(e)[1] > starts[e] + win for e in range(N_EXPERTS)]
    any_leftover = functools.reduce(jnp.logical_or, leftover)

    @pl.when(jnp.logical_not(any_leftover))
    def _():
        o_ref[...] = _rms(x3, g_ref[...])

    @pl.when(any_leftover)
    def _():
        o_ref[...] = x3
        tail_lane = lax.broadcasted_iota(I32, (TOKEN_CHUNK, CAPACITY - win), 1) + win
        for e in range(N_EXPERTS):
            @pl.when(leftover[e])
            def _():
                sl = slot[:, e:e + 1]
                hot = jnp.where((tail_lane == sl) & (sl >= starts[e] + win), 1.0, 0.0).astype(BF16)
                o_ref[...] += jnp.dot(hot, y_ref[e, win:, :], preferred_element_type=F32)

        o_ref[...] = _rms(o_ref[...], g_ref[...])


def _combine(offs, x2, slots, y, g_final):
    slots_t = slots.reshape(BATCH, N_EXPERTS, SEQ).transpose(0, 2, 1)
    x2 = x2.reshape(BATCH, SEQ, D_MODEL)
    tm = TOKEN_CHUNK
    return pl.pallas_call(
        _combine_kernel,
        grid_spec=pltpu.PrefetchScalarGridSpec(
            num_scalar_prefetch=1,
            grid=(BATCH, SEQ // tm),
            in_specs=[pl.BlockSpec((None, tm, D_MODEL), lambda b, i, offs: (b, i, 0)),
                      pl.BlockSpec((None, tm, N_EXPERTS), lambda b, i, offs: (b, i, 0)),
                      pl.BlockSpec((None, N_EXPERTS, None, CAPACITY, D_MODEL),
                                   lambda b, i, offs: (b // FFN_SEQS, 0, b % FFN_SEQS, 0, 0)),
                      pl.BlockSpec((1, D_MODEL), lambda b, i, offs: (0, 0))],
            out_specs=pl.BlockSpec((None, tm, D_MODEL), lambda b, i, offs: (b, i, 0))),
        out_shape=jax.ShapeDtypeStruct((BATCH, SEQ, D_MODEL), F32),
        compiler_params=_params("parallel", "arbitrary"),
        name="combine_final_norm",
    )(offs, x2, slots_t, y, g_final.reshape(1, -1))


def kernel(x, mem, positions, g_mix, w_in, g_q_lat, w_q_up, g_kv_lat, w_kv_up, w_fourier, w_out, g_mem_q,
           g_mem_kv, w_mem_q, w_mem_kv, w_mem_o, g_ffn, w_router, w_exp_gate, w_exp_up, w_exp_down, g_final):
    assert x.shape == (BATCH, SEQ, D_MODEL) and g_mix.shape[0] == 1
    x2d = x.reshape(TOKENS, D_MODEL)
    cos, sin = _rope_tables(positions)
    ab = _channel_mats(w_fourier[0])
    v12, q, k, v, ksq = _in_projection(x2d, g_mix[0], w_in[0], g_q_lat[0], w_q_up[0], g_kv_lat[0], w_kv_up[0],
                                       ab, cos, sin)
    y_f = _sequence_dft(v12)
    y_a = _mla_attention(q, k, v, ksq)
    mk, mv = _memory_kv(mem, g_mem_kv[0], w_mem_kv[0])
    x2, h_ext, aff = _mixing(x2d, y_f, y_a, w_out[0], g_mem_q[0], w_mem_q[0], mk, mv, w_mem_o[0], g_ffn[0],
                             w_router[0])
    slots, offs = _expert_slots(aff)
    xin = _gather(offs, slots, h_ext)
    y = _experts(xin, w_exp_gate[0], w_exp_up[0], w_exp_down[0])
    return _combine(offs, x2, slots, y, g_final)
```

```python
import functools

import numpy as np
import jax
import jax.numpy as jnp
from jax import lax
from jax.experimental import pallas as pl
from jax.experimental.pallas import tpu as pltpu

F32 = jnp.float32
BF16 = jnp.bfloat16
I32 = jnp.int32

D_MODEL = 1024
BATCH = 4
SEQ = 4096
TOKENS = BATCH * SEQ
MEM_LEN = 256
RMS_EPS = 1e-6
F_GROUPS = 8
F_GROUP_DIM = 64
F_WIDTH = F_GROUPS * F_GROUP_DIM
MLA_HEADS = 8
QK_NOPE_DIM = 64
QK_ROPE_DIM = 32
V_HEAD_DIM = 64
Q_LORA_RANK = 384
KV_LORA_RANK = 256
ROPE_THETA = 10000.0
MEM_HEADS = 4
MEM_HEAD_DIM = D_MODEL // MEM_HEADS
N_EXPERTS = 16
CAPACITY = 2 * SEQ // N_EXPERTS

NORM_SLACK = 1.01
SAFE_SHIFT = 30.0

LANES = 128
HEAD_PAD = 128
IN_PAD = 1280
HALF = SEQ // 2
QUARTER = SEQ // 4
VMEM_LIMIT = 56 * 1024 * 1024

TM_IN = 1024
IN_GROUPS = 1
TM_MIX = 1024
MIX_GROUPS = 2
TQ = 1024
ATTN_HEADS = 2
TOKEN_CHUNK = 512
OFFS_STRIDE = SEQ // TOKEN_CHUNK + 1
GATHER_WIN = 96
SCATTER_WIN = 128
SLOT_ALIGN_SHIFT = 4
GATHER_EXPERTS = 16
FFN_SEQS = 2


def _rms(x, g):
    return x * lax.rsqrt(jnp.mean(x * x, axis=-1, keepdims=True) + RMS_EPS) * g


def _params(*sem):
    return pltpu.CompilerParams(dimension_semantics=sem, vmem_limit_bytes=VMEM_LIMIT)


def _rope_kernel(pos_ref, freq_ref, cos_ref, sin_ref):
    ang = pos_ref[...] * freq_ref[...]
    cos_ref[...] = jnp.cos(ang)
    sin_ref[...] = jnp.sin(ang)


def _rope_tables(positions):
    half = QK_ROPE_DIM // 2
    freqs = 1.0 / (ROPE_THETA ** (jnp.arange(0, QK_ROPE_DIM, 2, dtype=F32) / QK_ROPE_DIM))
    return pl.pallas_call(
        _rope_kernel,
        out_shape=(jax.ShapeDtypeStruct((half, TOKENS), F32),) * 2,
        name="rope_tables",
    )(positions.astype(F32).reshape(1, TOKENS), freqs.reshape(half, 1))


def _rope_expanders():
    half = QK_ROPE_DIM // 2
    lane = np.arange(LANES)
    hit = (lane[None, :] % half) == np.arange(half)[:, None]
    is_sin = (lane // QK_ROPE_DIM) % 2 == 1
    return (jnp.asarray((hit & ~is_sin[None, :]).astype(np.float32)).astype(BF16),
            jnp.asarray((hit & is_sin[None, :]).astype(np.float32)).astype(BF16))


def _chan_kernel(cbd_ref, sbd_ref, w_ref, ab_ref):
    w = w_ref[...]
    ab_ref[:, :F_WIDTH] = jnp.dot(cbd_ref[...], w, precision=lax.Precision.HIGHEST,
                                  preferred_element_type=F32).astype(BF16)
    ab_ref[:, F_WIDTH:] = jnp.dot(sbd_ref[...], w, precision=lax.Precision.HIGHEST,
                                  preferred_element_type=F32).astype(BF16)


def _channel_mats(w_fourier):
    c = np.arange(F_GROUP_DIM)
    ang = 2.0 * np.pi * np.outer(c, c) / F_GROUP_DIM
    scale = F_GROUP_DIM ** -0.5
    eye = np.eye(F_GROUPS)
    cbd = np.kron(eye, np.cos(ang) * scale).astype(np.float32)
    sbd = np.kron(eye, np.sin(ang) * scale).astype(np.float32)
    wbd = (jnp.eye(F_GROUPS, dtype=F32)[:, None, :, None] * w_fourier[:, :, None, :]).reshape(F_WIDTH, F_WIDTH)
    return pl.pallas_call(
        _chan_kernel,
        out_shape=jax.ShapeDtypeStruct((F_WIDTH, 2 * F_WIDTH), BF16),
        name="channel_dft_fold",
    )(jnp.asarray(cbd), jnp.asarray(sbd), wbd)


def _inproj_kernel(x_ref, gmix_ref, win_ref, gq_ref, wq_ref, gkv_ref, wkv_ref, ab_ref, cos_ref, sin_ref,
                   ec_ref, es_ref, hsum_ref, v12_ref, q_ref, k_ref, v_ref, ksq_ref, vscr_ref):
    rows_per_group = x_ref.shape[0] // IN_GROUPS
    half_rows = rows_per_group // 2
    ksq_max = None
    for grp in range(IN_GROUPS):
        r0 = grp * rows_per_group
        rows = pl.ds(r0, rows_per_group)
        half = pl.ds(r0 // 2, half_rows)
        h = _rms(x_ref[rows, :], gmix_ref[...]).astype(BF16)
        u = jnp.dot(h, win_ref[...], preferred_element_type=F32)

        n_pairs = ab_ref.shape[0]
        for gp in range(n_pairs):
            pair = jnp.dot(u[:, LANES * gp:LANES * (gp + 1)].astype(BF16), ab_ref[gp],
                           preferred_element_type=F32)
            for part in range(2):
                c = part * n_pairs + gp
                cols = slice(LANES * c, LANES * (c + 1))
                vscr_ref[c, rows, :] = pair[:, LANES * part:LANES * (part + 1)]
                v12_ref[0, half, cols] = vscr_ref[c, pl.ds(r0, half_rows, stride=2), :].astype(BF16)
                v12_ref[1, half, cols] = vscr_ref[c, pl.ds(r0 + 1, half_rows, stride=2), :].astype(BF16)

        tab = None
        tn = (((0,), (0,)), ((), ()))
        for col_ref, exp_ref in ((cos_ref, ec_ref), (sin_ref, es_ref)):
            col = col_ref[:, rows]
            hi = col.astype(BF16)
            lo = (col - hi.astype(F32)).astype(BF16)
            term = (lax.dot_general(hi, exp_ref[...], tn, preferred_element_type=F32)
                    + lax.dot_general(lo, exp_ref[...], tn, preferred_element_type=F32))
            tab = term if tab is None else tab + term
        lane = lax.broadcasted_iota(I32, tab.shape, 1)
        rope0, rope1 = QK_NOPE_DIM, QK_NOPE_DIM + QK_ROPE_DIM
        back = LANES - QK_ROPE_DIM
        c1 = jnp.where(lane < rope0, 1.0, jnp.where(lane < rope1, tab, 0.0))
        c2 = jnp.where(lane >= rope1, tab, 0.0)

        q0 = F_WIDTH
        qn = _rms(u[:, q0:q0 + Q_LORA_RANK], gq_ref[...]).astype(BF16)
        qa = jnp.dot(qn, wq_ref[...], preferred_element_type=F32)
        q_blocks = []
        for hd in range(MLA_HEADS):
            blk = qa[:, HEAD_PAD * hd:HEAD_PAD * (hd + 1)]
            q_blocks.append(blk * c1 + pltpu.roll(blk * c2, back, 1))
            q_ref[rows, HEAD_PAD * hd:HEAD_PAD * (hd + 1)] = q_blocks[hd].astype(BF16)

        kv0 = q0 + Q_LORA_RANK
        kvn = _rms(u[:, kv0:kv0 + KV_LORA_RANK], gkv_ref[...]).astype(BF16)
        kv = jnp.dot(kvn, wkv_ref[...], preferred_element_type=F32)
        kr0 = kv0 + KV_LORA_RANK
        t = u[:, kr0:kr0 + LANES] * jnp.where(lane < 2 * QK_ROPE_DIM, tab, 0.0)
        kr = t + pltpu.roll(t, back, 1)
        kr = jnp.where((lane >= rope0) & (lane < rope1), pltpu.roll(kr, rope0, 1), 0.0)
        k_blocks = [kv[:, HEAD_PAD * hd:HEAD_PAD * (hd + 1)] + kr for hd in range(MLA_HEADS)]
        for hd in range(MLA_HEADS):
            k_ref[rows, HEAD_PAD * hd:HEAD_PAD * (hd + 1)] = k_blocks[hd].astype(BF16)
        v_ref[rows, :] = kv[:, MLA_HEADS * HEAD_PAD:].astype(BF16)
        sq = jnp.concatenate(
            [jnp.concatenate([(blk * blk).astype(BF16) for blk in blocks], axis=1)
             for blocks in (k_blocks, q_blocks)], axis=0)
        sq = jnp.dot(sq, hsum_ref[...], preferred_element_type=F32)
        grp_max = jnp.concatenate([jnp.max(sq[:rows_per_group], axis=0, keepdims=True),
                                   jnp.max(sq[rows_per_group:], axis=0, keepdims=True)], axis=0)
        ksq_max = grp_max if ksq_max is None else jnp.maximum(ksq_max, grp_max)
    ksq_ref[...] = ksq_max


def _in_projection(x2d, g_mix, w_in, g_q, w_q_up, g_kv, w_kv_up, ab, cos, sin):
    w_kr = w_in[:, -QK_ROPE_DIM:]
    half = QK_ROPE_DIM // 2
    w_kr_rot = jnp.concatenate([-w_kr[:, half:], w_kr[:, :half]], axis=1)
    win = jnp.concatenate(
        [w_in, w_kr_rot, jnp.zeros((D_MODEL, IN_PAD - w_in.shape[1] - QK_ROPE_DIM), F32)], axis=1).astype(BF16)
    scale = (QK_NOPE_DIM + QK_ROPE_DIM) ** -0.5
    wq = w_q_up.reshape(Q_LORA_RANK, MLA_HEADS, QK_NOPE_DIM + QK_ROPE_DIM) * scale
    wq_rope = wq[:, :, QK_NOPE_DIM:]
    wq_rot = jnp.concatenate([-wq_rope[:, :, half:], wq_rope[:, :, :half]], axis=2)
    wq = jnp.concatenate([wq, wq_rot], axis=2).reshape(Q_LORA_RANK, MLA_HEADS * HEAD_PAD).astype(BF16)
    wkv = w_kv_up.reshape(KV_LORA_RANK, MLA_HEADS, QK_NOPE_DIM + V_HEAD_DIM)
    wk = jnp.concatenate([wkv[:, :, :QK_NOPE_DIM],
                          jnp.zeros((KV_LORA_RANK, MLA_HEADS, HEAD_PAD - QK_NOPE_DIM), F32)], axis=2)
    wkv = jnp.concatenate([wk.reshape(KV_LORA_RANK, MLA_HEADS * HEAD_PAD),
                           wkv[:, :, QK_NOPE_DIM:].reshape(KV_LORA_RANK, MLA_HEADS * V_HEAD_DIM)],
                          axis=1).astype(BF16)

    ab = jnp.stack([jnp.concatenate([ab[LANES * g:LANES * (g + 1), LANES * g:LANES * (g + 1)],
                                     ab[LANES * g:LANES * (g + 1), F_WIDTH + LANES * g:F_WIDTH + LANES * (g + 1)]],
                                    axis=1) for g in range(F_WIDTH // LANES)])
    tm = TM_IN
    ec, es = _rope_expanders()
    hsum = jnp.asarray((np.arange(MLA_HEADS * HEAD_PAD)[:, None] // HEAD_PAD
                        == np.arange(LANES)[None, :]).astype(np.float32)).astype(BF16)
    full = lambda shape: pl.BlockSpec(shape, lambda i: (0,) * len(shape))
    tile = lambda w: pl.BlockSpec((tm, w), lambda i: (i, 0))
    table = pl.BlockSpec((QK_ROPE_DIM // 2, tm), lambda i: (0, i))
    per_half = HALF // tm
    per_seq = SEQ // tm
    v12_spec = pl.BlockSpec((None, 2, None, tm // 2, 2 * F_WIDTH),
                            lambda i: (i // per_seq, 0, (i % per_seq) // per_half, i % per_half, 0))
    return pl.pallas_call(
        _inproj_kernel,
        grid=(TOKENS // tm,),
        in_specs=[tile(D_MODEL), full((1, D_MODEL)), full(win.shape), full((1, Q_LORA_RANK)), full(wq.shape),
                  full((1, KV_LORA_RANK)), full(wkv.shape), full(ab.shape), table, table,
                  full(ec.shape), full(es.shape), full(hsum.shape)],
        out_specs=[v12_spec, tile(MLA_HEADS * HEAD_PAD), tile(MLA_HEADS * HEAD_PAD),
                   tile(MLA_HEADS * V_HEAD_DIM), pl.BlockSpec((None, 2, LANES), lambda i: (i, 0, 0))],
        out_shape=[jax.ShapeDtypeStruct((BATCH, 2, 2, QUARTER, 2 * F_WIDTH), BF16),
                   jax.ShapeDtypeStruct((TOKENS, MLA_HEADS * HEAD_PAD), BF16),
                   jax.ShapeDtypeStruct((TOKENS, MLA_HEADS * HEAD_PAD), BF16),
                   jax.ShapeDtypeStruct((TOKENS, MLA_HEADS * V_HEAD_DIM), BF16),
                   jax.ShapeDtypeStruct((TOKENS // tm, 2, LANES), F32)],
        scratch_shapes=[pltpu.VMEM((2 * F_WIDTH // LANES, tm, LANES), F32)],
        compiler_params=_params("parallel"),
        name="in_projection",
    )(x2d, g_mix.reshape(1, -1), win, g_q.reshape(1, -1), wq, g_kv.reshape(1, -1), wkv, ab, cos, sin, ec, es, hsum)


def _seq_dft_kernel(v_ref, m_ref, y_ref):
    sign = jnp.where(pl.program_id(1) == 0, 1.0, -1.0)
    parts = []
    for q in range(2):
        ab = (v_ref[q, 0].astype(F32) + sign * v_ref[q, 1].astype(F32)).astype(BF16)
        parts.append(jnp.dot(m_ref[q, :, :QUARTER], ab[:, :F_WIDTH], preferred_element_type=F32)
                     + jnp.dot(m_ref[q, :, QUARTER:], ab[:, F_WIDTH:], preferred_element_type=F32))
    y_ref[0] = (parts[0] + parts[1]).astype(BF16)
    y_ref[1] = (parts[0] - parts[1]).astype(BF16)


def _seq_dft_mats():
    i = np.arange(QUARTER)
    out = np.zeros((2, 2, QUARTER, 2 * QUARTER), np.float32)
    for p in range(2):
        for q in range(2):
            prod = np.outer(2 * i + p, 2 * i + q) % SEQ
            ang = 2.0 * np.pi * prod / SEQ
            out[p, q, :, :QUARTER] = np.cos(ang) / np.sqrt(SEQ)
            out[p, q, :, QUARTER:] = -np.sin(ang) / np.sqrt(SEQ)
    return out


def _sequence_dft(v):
    mats = jnp.asarray(_seq_dft_mats()).astype(BF16)
    return pl.pallas_call(
        _seq_dft_kernel,
        grid=(BATCH, 2),
        in_specs=[pl.BlockSpec((None, 2, 2, QUARTER, 2 * F_WIDTH), lambda b, p: (b, 0, 0, 0, 0)),
                  pl.BlockSpec((None, 2, QUARTER, 2 * QUARTER), lambda b, p: (p, 0, 0, 0))],
        out_specs=pl.BlockSpec((None, None, 2, QUARTER, F_WIDTH), lambda b, p: (b, p, 0, 0, 0)),
        out_shape=jax.ShapeDtypeStruct((BATCH, 2, 2, QUARTER, F_WIDTH), BF16),
        compiler_params=_params("parallel", "arbitrary"),
        name="sequence_dft",
    )(v, mats)


def _mla_kernel(q_ref, k_ref, v_ref, sq_ref, o_ref):
    v = v_ref[...]
    hg = pl.program_id(1)
    sq = jnp.max(sq_ref[...], axis=0)
    bounds = jnp.sqrt(sq[0:1] * sq[1:2]) * NORM_SLACK
    head_lane = lax.broadcasted_iota(I32, bounds.shape, 1)
    head_bounds = [jnp.max(jnp.where(head_lane == ATTN_HEADS * hg + j, bounds, 0.0), axis=1, keepdims=True)
                   for j in range(ATTN_HEADS)]
    safe = jnp.max(functools.reduce(jnp.maximum, head_bounds)) <= SAFE_SHIFT

    def attend(j, row_shift):
        cols = slice(HEAD_PAD * j, HEAD_PAD * (j + 1))
        out_cols = slice(V_HEAD_DIM * j, V_HEAD_DIM * (j + 1))
        s = lax.dot_general(q_ref[:, cols], k_ref[:, cols], (((1,), (1,)), ((), ())),
                            preferred_element_type=F32)
        p = jnp.exp(s - row_shift(s))
        l = jnp.sum(p, axis=1, keepdims=True)
        o = jnp.dot(p.astype(BF16), v, preferred_element_type=F32) / l
        o_ref[:, out_cols] = o[:, out_cols].astype(BF16)

    @pl.when(safe)
    def _():
        for j in range(ATTN_HEADS):
            attend(j, lambda s: head_bounds[j])

    @pl.when(jnp.logical_not(safe))
    def _():
        for j in range(ATTN_HEADS):
            attend(j, lambda s: jnp.max(s, axis=1, keepdims=True))


def _mla_attention(q, k, v, ksq):
    q = q.reshape(BATCH, SEQ, -1)
    k = k.reshape(BATCH, SEQ, -1)
    v = v.reshape(BATCH, SEQ, -1)
    ksq = ksq.reshape(BATCH, SEQ // TM_IN, 2, LANES)
    out = pl.pallas_call(
        _mla_kernel,
        grid=(BATCH, MLA_HEADS // ATTN_HEADS, SEQ // TQ),
        in_specs=[pl.BlockSpec((None, TQ, ATTN_HEADS * HEAD_PAD), lambda b, hg, i: (b, i, hg)),
                  pl.BlockSpec((None, SEQ, ATTN_HEADS * HEAD_PAD), lambda b, hg, i: (b, 0, hg)),
                  pl.BlockSpec((None, SEQ, ATTN_HEADS * V_HEAD_DIM), lambda b, hg, i: (b, 0, hg)),
                  pl.BlockSpec((None, SEQ // TM_IN, 2, LANES), lambda b, hg, i: (b, 0, 0, 0))],
        out_specs=pl.BlockSpec((None, TQ, ATTN_HEADS * V_HEAD_DIM), lambda b, hg, i: (b, i, hg)),
        out_shape=jax.ShapeDtypeStruct((BATCH, SEQ, MLA_HEADS * V_HEAD_DIM), BF16),
        compiler_params=_params("parallel", "parallel", "arbitrary"),
        name="mla_attention",
    )(q, k, v, ksq)
    return out.reshape(TOKENS, MLA_HEADS * V_HEAD_DIM)


def _memkv_kernel(mem_ref, g_ref, w_ref, k_ref, v_ref):
    mn = _rms(mem_ref[...], g_ref[...]).astype(BF16)
    kv = jnp.dot(mn, w_ref[...], preferred_element_type=F32)
    k_ref[...] = kv[:, :D_MODEL].astype(BF16)
    v_ref[...] = kv[:, D_MODEL:].astype(BF16)


def _memory_kv(mem, g_mem_kv, w_mem_kv):
    blk = pl.BlockSpec((None, MEM_LEN, D_MODEL), lambda b: (b, 0, 0))
    return pl.pallas_call(
        _memkv_kernel,
        grid=(BATCH,),
        in_specs=[blk, pl.BlockSpec((1, D_MODEL), lambda b: (0, 0)),
                  pl.BlockSpec((D_MODEL, 2 * D_MODEL), lambda b: (0, 0))],
        out_specs=[blk, blk],
        out_shape=[jax.ShapeDtypeStruct((BATCH, MEM_LEN, D_MODEL), BF16)] * 2,
        compiler_params=_params("parallel"),
        name="memory_kv",
    )(mem, g_mem_kv.reshape(1, -1), w_mem_kv.astype(BF16))


def _mix_kernel(x_ref, yf_ref, ya_ref, wo_ref, gq_ref, wmq_ref, mk_ref, mv_ref, wmo_ref, gf_ref, wr_ref,
                x2_ref, hext_ref, aff_ref, zscr_ref):
    tm = x_ref.shape[0]
    rows_per_group = tm // MIX_GROUPS
    for grp in range(MIX_GROUPS):
        rows = pl.ds(grp * rows_per_group, rows_per_group)
        half = pl.ds(grp * rows_per_group // 2, rows_per_group // 2)
        wo_f = wo_ref[:F_WIDTH, :]
        z_even = jnp.dot(yf_ref[0, half, :], wo_f, preferred_element_type=F32)
        z_odd = jnp.dot(yf_ref[1, half, :], wo_f, preferred_element_type=F32)
        for c in range(zscr_ref.shape[0]):
            cols = slice(LANES * c, LANES * (c + 1))
            zscr_ref[c, pl.ds(grp * rows_per_group, rows_per_group // 2, stride=2), :] = z_even[:, cols]
            zscr_ref[c, pl.ds(grp * rows_per_group + 1, rows_per_group // 2, stride=2), :] = z_odd[:, cols]
        z = jnp.concatenate([zscr_ref[c, rows, :] for c in range(zscr_ref.shape[0])], axis=1)
        x1 = x_ref[rows, :] + z + jnp.dot(ya_ref[rows, :], wo_ref[F_WIDTH:, :], preferred_element_type=F32)

        hq = _rms(x1, gq_ref[...]).astype(BF16)
        qm = (jnp.dot(hq, wmq_ref[...], preferred_element_type=F32) * (MEM_HEAD_DIM ** -0.5)).astype(BF16)
        heads = []
        for hd in range(MEM_HEADS):
            sl = slice(MEM_HEAD_DIM * hd, MEM_HEAD_DIM * (hd + 1))
            s = lax.dot_general(qm[:, sl], mk_ref[:, sl], (((1,), (1,)), ((), ())),
                                preferred_element_type=F32)
            p = jnp.exp(s - jnp.max(s, axis=1, keepdims=True))
            l = jnp.sum(p, axis=1, keepdims=True)
            heads.append((jnp.dot(p.astype(BF16), mv_ref[:, sl], preferred_element_type=F32) / l).astype(BF16))
        o = jnp.concatenate(heads, axis=1)
        x2 = x1 + jnp.dot(o, wmo_ref[...], preferred_element_type=F32)
        x2_ref[rows, :] = x2

        h3 = _rms(x2, gf_ref[...])
        h3_hi = h3.astype(BF16)
        hext_ref[rows, :D_MODEL] = h3_hi
        h3_lo = (h3 - h3_hi.astype(F32)).astype(BF16)
        hi_terms = jnp.dot(h3_hi, wr_ref[...], preferred_element_type=F32)
        logits = (hi_terms[:, :LANES] + hi_terms[:, LANES:]
                  + jnp.dot(h3_lo, wr_ref[:, :LANES], preferred_element_type=F32))
        lane = lax.broadcasted_iota(I32, logits.shape, 1)
        logits = jnp.where(lane < N_EXPERTS, logits, -jnp.inf)
        e = jnp.exp(logits - jnp.max(logits, axis=1, keepdims=True))
        aff = e / jnp.sum(e, axis=1, keepdims=True)
        aff_ref[rows, :] = aff
        hi = aff.astype(BF16)
        r1 = aff - hi.astype(F32)
        mid = r1.astype(BF16)
        lo = (r1 - mid.astype(F32)).astype(BF16)
        hext_ref[rows, D_MODEL:] = jnp.where(
            lane < N_EXPERTS, hi,
            jnp.where(lane < 2 * N_EXPERTS, pltpu.roll(mid.astype(F32), N_EXPERTS, 1).astype(BF16),
                      pltpu.roll(lo.astype(F32), 2 * N_EXPERTS, 1).astype(BF16)))


def _mixing(x2d, y_f, y_a, w_out, g_mem_q, w_mem_q, mk, mv, w_mem_o, g_ffn, w_router):
    tm = TM_MIX
    wr = jnp.concatenate([w_router, jnp.zeros((D_MODEL, LANES - N_EXPERTS), F32)], axis=1)
    wr_hi = wr.astype(BF16)
    wr_cat = jnp.concatenate([wr_hi, (wr - wr_hi.astype(F32)).astype(BF16)], axis=1)
    full = lambda shape: pl.BlockSpec(shape, lambda i: (0,) * len(shape))
    tile = lambda w: pl.BlockSpec((tm, w), lambda i: (i, 0))
    per_half = HALF // tm
    per_seq = SEQ // tm
    per_batch = pl.BlockSpec((None, MEM_LEN, D_MODEL), lambda i: (i // per_seq, 0, 0))
    yf_spec = pl.BlockSpec((None, 2, None, tm // 2, F_WIDTH),
                           lambda i: (i // per_seq, 0, (i % per_seq) // per_half, i % per_half, 0))
    return pl.pallas_call(
        _mix_kernel,
        grid=(TOKENS // tm,),
        in_specs=[tile(D_MODEL), yf_spec, tile(F_WIDTH), full((D_MODEL, D_MODEL)), full((1, D_MODEL)),
                  full((D_MODEL, D_MODEL)), per_batch, per_batch, full((D_MODEL, D_MODEL)), full((1, D_MODEL)),
                  full((D_MODEL, 2 * LANES))],
        out_specs=[tile(D_MODEL), tile(D_MODEL + LANES), tile(LANES)],
        out_shape=[jax.ShapeDtypeStruct((TOKENS, D_MODEL), F32),
                   jax.ShapeDtypeStruct((TOKENS, D_MODEL + LANES), BF16),
                   jax.ShapeDtypeStruct((TOKENS, LANES), F32)],
        scratch_shapes=[pltpu.VMEM((D_MODEL // LANES, tm, LANES), F32)],
        compiler_params=_params("parallel"),
        name="mix_memattn_router",
    )(x2d, y_f, y_a, w_out.astype(BF16), g_mem_q.reshape(1, -1), w_mem_q.astype(BF16), mk, mv,
      w_mem_o.astype(BF16), g_ffn.reshape(1, -1), wr_cat)


def _topk_kernel(aff_ref, slot_ref, offs_ref):
    aff = aff_ref[...]
    rows = aff.shape[0]

    thr = jnp.zeros((rows, 1), I32)
    for bit in range(30, -1, -1):
        cand = thr | (1 << bit)
        cnt = jnp.sum(jnp.where(aff >= pltpu.bitcast(cand, F32), 1.0, 0.0), axis=1, keepdims=True)
        thr = jnp.where(cnt >= CAPACITY, cand, thr)
    thr_f = pltpu.bitcast(thr, F32)

    chunk = 256
    r = lax.broadcasted_iota(I32, (chunk, chunk), 0)
    c = lax.broadcasted_iota(I32, (chunk, chunk), 1)
    tri = jnp.where(r < c, 1.0, 0.0).astype(BF16)

    def exclusive_count(mask):
        off = jnp.zeros((rows, 1), F32)
        outs = []
        for j in range(SEQ // chunk):
            mj = mask[:, chunk * j:chunk * (j + 1)]
            outs.append(jnp.dot(mj.astype(BF16), tri, preferred_element_type=F32) + off)
            off = off + jnp.sum(mj, axis=1, keepdims=True)
        return jnp.concatenate(outs, axis=1), off

    gt = aff > thr_f
    tie = jnp.where(aff == thr_f, 1.0, 0.0)
    n_gt = jnp.sum(jnp.where(gt, 1.0, 0.0), axis=1, keepdims=True)
    tie_rank, _ = exclusive_count(tie)
    sel = jnp.where(gt | ((tie > 0.0) & (tie_rank < CAPACITY - n_gt)), 1.0, 0.0)
    slot, _ = exclusive_count(sel)
    slot_ref[...] = jnp.where(sel > 0.0, slot.astype(I32), -1)
    tok = lax.broadcasted_iota(I32, (SEQ, LANES), 0)
    j = lax.broadcasted_iota(I32, (SEQ, LANES), 1)
    before = jnp.where(tok < j * TOKEN_CHUNK, 1.0, 0.0).astype(BF16)
    offs_ref[...] = jnp.dot(sel.astype(BF16), before, preferred_element_type=F32).astype(I32)


def _expert_slots(aff):
    aff_t = aff[:, :N_EXPERTS].reshape(BATCH, SEQ, N_EXPERTS).transpose(0, 2, 1).reshape(BATCH * N_EXPERTS, SEQ)
    slots, offs = pl.pallas_call(
        _topk_kernel,
        out_shape=[jax.ShapeDtypeStruct((BATCH * N_EXPERTS, SEQ), I32),
                   jax.ShapeDtypeStruct((BATCH * N_EXPERTS, LANES), I32)],
        compiler_params=pltpu.CompilerParams(vmem_limit_bytes=VMEM_LIMIT),
        name="expert_topk",
    )(aff_t)
    return slots, offs[:, :OFFS_STRIDE].reshape(-1)


def _window_start(first, win):
    start = jnp.minimum((first >> SLOT_ALIGN_SHIFT) << SLOT_ALIGN_SHIFT, CAPACITY - win)
    return pl.multiple_of(start, 1 << SLOT_ALIGN_SHIFT)


def _gather_kernel(offs_ref, slot_ref, h_ref, x_ref):
    win = GATHER_WIN
    last = CAPACITY - win
    b = pl.program_id(0)
    e0 = pl.program_id(1) * GATHER_EXPERTS
    j = pl.program_id(2)

    @pl.when(j == 0)
    def _():
        x_ref[...] = jnp.zeros_like(x_ref)

    def bounds(e):
        base = (b * N_EXPERTS + e0 + e) * OFFS_STRIDE + j
        return offs_ref[base], offs_ref[base + 1]

    row = lax.broadcasted_iota(I32, (win, TOKEN_CHUNK), 0)
    h_c = h_ref[...]
    starts = [_window_start(bounds(e)[0], win) for e in range(GATHER_EXPERTS)]
    onehot = jnp.concatenate(
        [jnp.where(row + starts[e] == slot_ref[e:e + 1, :], 1.0, 0.0).astype(BF16)
         for e in range(GATHER_EXPERTS)], axis=0)
    picked = jnp.dot(onehot, h_c, preferred_element_type=F32).astype(BF16)
    for e in range(GATHER_EXPERTS):
        rows = pl.ds(starts[e], win)
        x_ref[e, rows, :] = x_ref[e, rows, :] + picked[e * win:(e + 1) * win]

    for e in range(GATHER_EXPERTS):
        _, end = bounds(e)
        covered = starts[e] + win
        slot_e = slot_ref[e:e + 1, :]

        def extra_window(i, carry):
            lo = covered + i * win
            r0 = pl.multiple_of(jnp.minimum(lo, last), 1 << SLOT_ALIGN_SHIFT)
            hot = jnp.where((row + r0 == slot_e) & (slot_e >= lo), 1.0, 0.0).astype(BF16)
            rows = pl.ds(r0, win)
            x_ref[e, rows, :] = x_ref[e, rows, :] + jnp.dot(hot, h_c, preferred_element_type=F32).astype(BF16)
            return carry

        lax.fori_loop(0, jnp.maximum(end - covered + win - 1, 0) // win, extra_window, 0)


def _gather(offs, slots, h_ext):
    slots = slots.reshape(BATCH, N_EXPERTS, SEQ)
    h_ext = h_ext.reshape(BATCH, SEQ, D_MODEL + LANES)
    return pl.pallas_call(
        _gather_kernel,
        grid_spec=pltpu.PrefetchScalarGridSpec(
            num_scalar_prefetch=1,
            grid=(BATCH, N_EXPERTS // GATHER_EXPERTS, SEQ // TOKEN_CHUNK),
            in_specs=[pl.BlockSpec((None, GATHER_EXPERTS, TOKEN_CHUNK), lambda b, g, j, offs: (b, g, j)),
                      pl.BlockSpec((None, TOKEN_CHUNK, D_MODEL + LANES), lambda b, g, j, offs: (b, j, 0))],
            out_specs=pl.BlockSpec((None, GATHER_EXPERTS, None, CAPACITY, D_MODEL + LANES),
                                   lambda b, g, j, offs: (b // FFN_SEQS, g, b % FFN_SEQS, 0, 0))),
        out_shape=jax.ShapeDtypeStruct((BATCH // FFN_SEQS, N_EXPERTS, FFN_SEQS, CAPACITY, D_MODEL + LANES),
                                       BF16),
        compiler_params=_params("parallel", "parallel", "arbitrary"),
        name="expert_gather",
    )(offs, slots, h_ext)


def _expert_kernel(x_ref, wg_ref, wu_ref, wd_ref, y_ref, wg_s, wu_s, wd_s):
    g = pl.program_id(0)
    part = pl.program_id(1)
    slab = wg_ref.shape[0]

    @pl.when(g < N_EXPERTS)
    def _():
        rows = pl.ds(pl.multiple_of(part * slab, slab), slab)
        wg_s[g % 2, rows, :] = wg_ref[...].astype(BF16)
        wu_s[g % 2, rows, :] = wu_ref[...].astype(BF16)
        wd_s[g % 2, rows, :] = wd_ref[...].astype(BF16)

    @pl.when(g == 0)
    def _():
        y_ref[...] = jnp.zeros_like(y_ref)

    @pl.when(g > 0)
    def _():
        e = g - 1
        cur = e % 2
        rows = x_ref.shape[0] * x_ref.shape[1]
        xin = x_ref[:, :, :D_MODEL].reshape(rows, D_MODEL)
        ext = x_ref[:, :, D_MODEL:].reshape(rows, LANES).astype(F32)
        lane = lax.broadcasted_iota(I32, ext.shape, 1)
        mine = (lane == e) | (lane == e + N_EXPERTS) | (lane == e + 2 * N_EXPERTS)
        gate = jnp.sum(jnp.where(mine, ext, 0.0), axis=1, keepdims=True)

        a = jnp.dot(xin, wg_s[cur], preferred_element_type=F32)
        b = jnp.dot(xin, wu_s[cur], preferred_element_type=F32)
        hid = (a / (1.0 + jnp.exp(-a)) * b).astype(BF16)
        y = (jnp.dot(hid, wd_s[cur], preferred_element_type=F32) * gate).astype(BF16)
        y_ref[...] = y.reshape(y_ref.shape)


def _experts(xin, w_gate, w_up, w_down):
    parts = BATCH // FFN_SEQS
    slab = D_MODEL // parts
    prev = lambda g: jnp.maximum(g - 1, 0)
    wspec = pl.BlockSpec((None, slab, D_MODEL), lambda g, p: (jnp.minimum(g, N_EXPERTS - 1), p, 0))
    return pl.pallas_call(
        _expert_kernel,
        grid=(N_EXPERTS + 1, parts),
        in_specs=[pl.BlockSpec((None, None, FFN_SEQS, CAPACITY, D_MODEL + LANES),
                               lambda g, p: (p, prev(g), 0, 0, 0)),
                  wspec, wspec, wspec],
        out_specs=pl.BlockSpec((None, None, FFN_SEQS, CAPACITY, D_MODEL),
                               lambda g, p: (p, jnp.where(g == 0, N_EXPERTS, g - 1), 0, 0, 0)),
        out_shape=jax.ShapeDtypeStruct((parts, N_EXPERTS + 1, FFN_SEQS, CAPACITY, D_MODEL), BF16),
        scratch_shapes=[pltpu.VMEM((2, D_MODEL, D_MODEL), BF16)] * 3,
        compiler_params=_params("arbitrary", "arbitrary"),
        name="expert_ffn",
    )(xin, w_gate, w_up, w_down)


def _combine_kernel(offs_ref, x2_ref, slot_ref, y_ref, g_ref, o_ref):
    b = pl.program_id(0)
    j = pl.program_id(1)
    win = SCATTER_WIN
    lane = lax.broadcasted_iota(I32, (TOKEN_CHUNK, win), 1)
    slot = slot_ref[...]

    def bounds(e):
        base = (b * N_EXPERTS + e) * OFFS_STRIDE + j
        return offs_ref[base], offs_ref[base + 1]

    starts = [_window_start(bounds(e)[0], win) for e in range(N_EXPERTS)]
    onehot = jnp.concatenate(
        [jnp.where(lane + starts[e] == slot[:, e:e + 1], 1.0, 0.0).astype(BF16) for e in range(N_EXPERTS)],
        axis=1)
    rows = jnp.concatenate([y_ref[e, pl.ds(starts[e], win), :] for e in range(N_EXPERTS)], axis=0)
    x3 = x2_ref[...] + jnp.dot(onehot, rows, preferred_element_type=F32)

    leftover = [bounds(e)[1] > starts[e] + win for e in range(N_EXPERTS)]
    any_leftover = functools.reduce(jnp.logical_or, leftover)

    @pl.when(jnp.logical_not(any_leftover))
    def _():
        o_ref[...] = _rms(x3, g_ref[...])

    @pl.when(any_leftover)
    def _():
        o_ref[...] = x3
        tail_lane = lax.broadcasted_iota(I32, (TOKEN_CHUNK, CAPACITY - win), 1) + win
        for e in range(N_EXPERTS):
            @pl.when(leftover[e])
            def _():
                sl = slot[:, e:e + 1]
                hot = jnp.where((tail_lane == sl) & (sl >= starts[e] + win), 1.0, 0.0).astype(BF16)
                o_ref[...] += jnp.dot(hot, y_ref[e, win:, :], preferred_element_type=F32)

        o_ref[...] = _rms(o_ref[...], g_ref[...])


def _combine(offs, x2, slots, y, g_final):
    slots_t = slots.reshape(BATCH, N_EXPERTS, SEQ).transpose(0, 2, 1)
    x2 = x2.reshape(BATCH, SEQ, D_MODEL)
    tm = TOKEN_CHUNK
    return pl.pallas_call(
        _combine_kernel,
        grid_spec=pltpu.PrefetchScalarGridSpec(
            num_scalar_prefetch=1,
            grid=(BATCH, SEQ // tm),
            in_specs=[pl.BlockSpec((None, tm, D_MODEL), lambda b, i, offs: (b, i, 0)),
                      pl.BlockSpec((None, tm, N_EXPERTS), lambda b, i, offs: (b, i, 0)),
                      pl.BlockSpec((None, N_EXPERTS, None, CAPACITY, D_MODEL),
                                   lambda b, i, offs: (b // FFN_SEQS, 0, b % FFN_SEQS, 0, 0)),
                      pl.BlockSpec((1, D_MODEL), lambda b, i, offs: (0, 0))],
            out_specs=pl.BlockSpec((None, tm, D_MODEL), lambda b, i, offs: (b, i, 0))),
        out_shape=jax.ShapeDtypeStruct((BATCH, SEQ, D_MODEL), F32),
        compiler_params=_params("parallel", "arbitrary"),
        name="combine_final_norm",
    )(offs, x2, slots_t, y, g_final.reshape(1, -1))


def kernel(x, mem, positions, g_mix, w_in, g_q_lat, w_q_up, g_kv_lat, w_kv_up, w_fourier, w_out, g_mem_q,
           g_mem_kv, w_mem_q, w_mem_kv, w_mem_o, g_ffn, w_router, w_exp_gate, w_exp_up, w_exp_down, g_final):
    assert x.shape == (BATCH, SEQ, D_MODEL) and g_mix.shape[0] == 1
    x2d = x.reshape(TOKENS, D_MODEL)
    cos, sin = _rope_tables(positions)
    ab = _channel_mats(w_fourier[0])
    v12, q, k, v, ksq = _in_projection(x2d, g_mix[0], w_in[0], g_q_lat[0], w_q_up[0], g_kv_lat[0], w_kv_up[0],
                                       ab, cos, sin)
    y_f = _sequence_dft(v12)
    y_a = _mla_attention(q, k, v, ksq)
    mk, mv = _memory_kv(mem, g_mem_kv[0], w_mem_kv[0])
    x2, h_ext, aff = _mixing(x2d, y_f, y_a, w_out[0], g_mem_q[0], w_mem_q[0], mk, mv, w_mem_o[0], g_ffn[0],
                             w_router[0])
    slots, offs = _expert_slots(aff)
    xin = _gather(offs, slots, h_ext)
    y = _experts(xin, w_exp_gate[0], w_exp_up[0], w_exp_down[0])
    return _combine(offs, x2, slots, y, g_final)
```

```python
import functools

import numpy as np
import jax
import jax.numpy as jnp
from jax import lax
from jax.experimental import pallas as pl
from jax.experimental.pallas import tpu as pltpu

F32 = jnp.float32
BF16 = jnp.bfloat16
I32 = jnp.int32

D_MODEL = 1024
BATCH = 4
SEQ = 4096
TOKENS = BATCH * SEQ
MEM_LEN = 256
RMS_EPS = 1e-6
F_GROUPS = 8
F_GROUP_DIM = 64
F_WIDTH = F_GROUPS * F_GROUP_DIM
MLA_HEADS = 8
QK_NOPE_DIM = 64
QK_ROPE_DIM = 32
V_HEAD_DIM = 64
Q_LORA_RANK = 384
KV_LORA_RANK = 256
ROPE_THETA = 10000.0
MEM_HEADS = 4
MEM_HEAD_DIM = D_MODEL // MEM_HEADS
N_EXPERTS = 16
CAPACITY = 2 * SEQ // N_EXPERTS

NORM_SLACK = 1.01
SAFE_SHIFT = 30.0

LANES = 128
HEAD_PAD = 128
IN_PAD = 1280
HALF = SEQ // 2
QUARTER = SEQ // 4
VMEM_LIMIT = 56 * 1024 * 1024

TM_IN = 1024
IN_GROUPS = 1
TM_MIX = 1024
MIX_GROUPS = 2
TQ = 1024
ATTN_HEADS = 2
TOKEN_CHUNK = 512
OFFS_STRIDE = SEQ // TOKEN_CHUNK + 1
GATHER_WIN = 96
SCATTER_WIN = 128
SLOT_ALIGN_SHIFT = 4
GATHER_EXPERTS = 16
FFN_SEQS = 2


def _rms(x, g):
    return x * lax.rsqrt(jnp.mean(x * x, axis=-1, keepdims=True) + RMS_EPS) * g


def _params(*sem):
    return pltpu.CompilerParams(dimension_semantics=sem, vmem_limit_bytes=VMEM_LIMIT)


def _rope_kernel(pos_ref, freq_ref, cos_ref, sin_ref):
    ang = pos_ref[...] * freq_ref[...]
    cos_ref[...] = jnp.cos(ang)
    sin_ref[...] = jnp.sin(ang)


def _rope_tables(positions):
    half = QK_ROPE_DIM // 2
    freqs = 1.0 / (ROPE_THETA ** (jnp.arange(0, QK_ROPE_DIM, 2, dtype=F32) / QK_ROPE_DIM))
    return pl.pallas_call(
        _rope_kernel,
        out_shape=(jax.ShapeDtypeStruct((half, TOKENS), F32),) * 2,
        name="rope_tables",
    )(positions.astype(F32).reshape(1, TOKENS), freqs.reshape(half, 1))


def _rope_expanders():
    half = QK_ROPE_DIM // 2
    lane = np.arange(LANES)
    hit = (lane[None, :] % half) == np.arange(half)[:, None]
    is_sin = (lane // QK_ROPE_DIM) % 2 == 1
    return (jnp.asarray((hit & ~is_sin[None, :]).astype(np.float32)).astype(BF16),
            jnp.asarray((hit & is_sin[None, :]).astype(np.float32)).astype(BF16))


def _chan_kernel(cbd_ref, sbd_ref, w_ref, ab_ref):
    w = w_ref[...]
    ab_ref[:, :F_WIDTH] = jnp.dot(cbd_ref[...], w, precision=lax.Precision.HIGHEST,
                                  preferred_element_type=F32).astype(BF16)
    ab_ref[:, F_WIDTH:] = jnp.dot(sbd_ref[...], w, precision=lax.Precision.HIGHEST,
                                  preferred_element_type=F32).astype(BF16)


def _channel_mats(w_fourier):
    c = np.arange(F_GROUP_DIM)
    ang = 2.0 * np.pi * np.outer(c, c) / F_GROUP_DIM
    scale = F_GROUP_DIM ** -0.5
    eye = np.eye(F_GROUPS)
    cbd = np.kron(eye, np.cos(ang) * scale).astype(np.float32)
    sbd = np.kron(eye, np.sin(ang) * scale).astype(np.float32)
    wbd = (jnp.eye(F_GROUPS, dtype=F32)[:, None, :, None] * w_fourier[:, :, None, :]).reshape(F_WIDTH, F_WIDTH)
    return pl.pallas_call(
        _chan_kernel,
        out_shape=jax.ShapeDtypeStruct((F_WIDTH, 2 * F_WIDTH), BF16),
        name="channel_dft_fold",
    )(jnp.asarray(cbd), jnp.asarray(sbd), wbd)


def _inproj_kernel(x_ref, gmix_ref, win_ref, gq_ref, wq_ref, gkv_ref, wkv_ref, ab_ref, cos_ref, sin_ref,
                   ec_ref, es_ref, hsum_ref, v12_ref, q_ref, k_ref, v_ref, ksq_ref, vscr_ref):
    rows_per_group = x_ref.shape[0] // IN_GROUPS
    half_rows = rows_per_group // 2
    ksq_max = None
    for grp in range(IN_GROUPS):
        r0 = grp * rows_per_group
        rows = pl.ds(r0, rows_per_group)
        half = pl.ds(r0 // 2, half_rows)
        h = _rms(x_ref[rows, :], gmix_ref[...]).astype(BF16)
        u = jnp.dot(h, win_ref[...], preferred_element_type=F32)

        n_pairs = ab_ref.shape[0]
        for gp in range(n_pairs):
            pair = jnp.dot(u[:, LANES * gp:LANES * (gp + 1)].astype(BF16), ab_ref[gp],
                           preferred_element_type=F32)
            for part in range(2):
                c = part * n_pairs + gp
                cols = slice(LANES * c, LANES * (c + 1))
                vscr_ref[c, rows, :] = pair[:, LANES * part:LANES * (part + 1)]
                v12_ref[0, half, cols] = vscr_ref[c, pl.ds(r0, half_rows, stride=2), :].astype(BF16)
                v12_ref[1, half, cols] = vscr_ref[c, pl.ds(r0 + 1, half_rows, stride=2), :].astype(BF16)

        tab = None
        tn = (((0,), (0,)), ((), ()))
        for col_ref, exp_ref in ((cos_ref, ec_ref), (sin_ref, es_ref)):
            col = col_ref[:, rows]
            hi = col.astype(BF16)
            lo = (col - hi.astype(F32)).astype(BF16)
            term = (lax.dot_general(hi, exp_ref[...], tn, preferred_element_type=F32)
                    + lax.dot_general(lo, exp_ref[...], tn, preferred_element_type=F32))
            tab = term if tab is None else tab + term
        lane = lax.broadcasted_iota(I32, tab.shape, 1)
        rope0, rope1 = QK_NOPE_DIM, QK_NOPE_DIM + QK_ROPE_DIM
        back = LANES - QK_ROPE_DIM
        c1 = jnp.where(lane < rope0, 1.0, jnp.where(lane < rope1, tab, 0.0))
        c2 = jnp.where(lane >= rope1, tab, 0.0)

        q0 = F_WIDTH
        qn = _rms(u[:, q0:q0 + Q_LORA_RANK], gq_ref[...]).astype(BF16)
        qa = jnp.dot(qn, wq_ref[...], preferred_element_type=F32)
        q_blocks = []
        for hd in range(MLA_HEADS):
            blk = qa[:, HEAD_PAD * hd:HEAD_PAD * (hd + 1)]
            q_blocks.append(blk * c1 + pltpu.roll(blk * c2, back, 1))
            q_ref[rows, HEAD_PAD * hd:HEAD_PAD * (hd + 1)] = q_blocks[hd].astype(BF16)

        kv0 = q0 + Q_LORA_RANK
        kvn = _rms(u[:, kv0:kv0 + KV_LORA_RANK], gkv_ref[...]).astype(BF16)
        kv = jnp.dot(kvn, wkv_ref[...], preferred_element_type=F32)
        kr0 = kv0 + KV_LORA_RANK
        t = u[:, kr0:kr0 + LANES] * jnp.where(lane < 2 * QK_ROPE_DIM, tab, 0.0)
        kr = t + pltpu.roll(t, back, 1)
        kr = jnp.where((lane >= rope0) & (lane < rope1), pltpu.roll(kr, rope0, 1), 0.0)
        k_blocks = [kv[:, HEAD_PAD * hd:HEAD_PAD * (hd + 1)] + kr for hd in range(MLA_HEADS)]
        for hd in range(MLA_HEADS):
            k_ref[rows, HEAD_PAD * hd:HEAD_PAD * (hd + 1)] = k_blocks[hd].astype(BF16)
        v_ref[rows, :] = kv[:, MLA_HEADS * HEAD_PAD:].astype(BF16)
        sq = jnp.concatenate(
            [jnp.concatenate([(blk * blk).astype(BF16) for blk in blocks], axis=1)
             for blocks in (k_blocks, q_blocks)], axis=0)
        sq = jnp.dot(sq, hsum_ref[...], preferred_element_type=F32)
        grp_max = jnp.concatenate([jnp.max(sq[:rows_per_group], axis=0, keepdims=True),
                                   jnp.max(sq[rows_per_group:], axis=0, keepdims=True)], axis=0)
        ksq_max = grp_max if ksq_max is None else jnp.maximum(ksq_max, grp_max)
    ksq_ref[...] = ksq_max


def _in_projection(x2d, g_mix, w_in, g_q, w_q_up, g_kv, w_kv_up, ab, cos, sin):
    w_kr = w_in[:, -QK_ROPE_DIM:]
    half = QK_ROPE_DIM // 2
    w_kr_rot = jnp.concatenate([-w_kr[:, half:], w_kr[:, :half]], axis=1)
    win = jnp.concatenate(
        [w_in, w_kr_rot, jnp.zeros((D_MODEL, IN_PAD - w_in.shape[1] - QK_ROPE_DIM), F32)], axis=1).astype(BF16)
    scale = (QK_NOPE_DIM + QK_ROPE_DIM) ** -0.5
    wq = w_q_up.reshape(Q_LORA_RANK, MLA_HEADS, QK_NOPE_DIM + QK_ROPE_DIM) * scale
    wq_rope = wq[:, :, QK_NOPE_DIM:]
    wq_rot = jnp.concatenate([-wq_rope[:, :, half:], wq_rope[:, :, :half]], axis=2)
    wq = jnp.concatenate([wq, wq_rot], axis=2).reshape(Q_LORA_RANK, MLA_HEADS * HEAD_PAD).astype(BF16)
    wkv = w_kv_up.reshape(KV_LORA_RANK, MLA_HEADS, QK_NOPE_DIM + V_HEAD_DIM)
    wk = jnp.concatenate([wkv[:, :, :QK_NOPE_DIM],
                          jnp.zeros((KV_LORA_RANK, MLA_HEADS, HEAD_PAD - QK_NOPE_DIM), F32)], axis=2)
    wkv = jnp.concatenate([wk.reshape(KV_LORA_RANK, MLA_HEADS * HEAD_PAD),
                           wkv[:, :, QK_NOPE_DIM:].reshape(KV_LORA_RANK, MLA_HEADS * V_HEAD_DIM)],
                          axis=1).astype(BF16)

    ab = jnp.stack([jnp.concatenate([ab[LANES * g:LANES * (g + 1), LANES * g:LANES * (g + 1)],
                                     ab[LANES * g:LANES * (g + 1), F_WIDTH + LANES * g:F_WIDTH + LANES * (g + 1)]],
                                    axis=1) for g in range(F_WIDTH // LANES)])
    tm = TM_IN
    ec, es = _rope_expanders()
    hsum = jnp.asarray((np.arange(MLA_HEADS * HEAD_PAD)[:, None] // HEAD_PAD
                        == np.arange(LANES)[None, :]).astype(np.float32)).astype(BF16)
    full = lambda shape: pl.BlockSpec(shape, lambda i: (0,) * len(shape), pipeline_mode=pl.Buffered(1))
    tile = lambda w: pl.BlockSpec((tm, w), lambda i: (i, 0))
    table = pl.BlockSpec((QK_ROPE_DIM // 2, tm), lambda i: (0, i))
    per_half = HALF // tm
    per_seq = SEQ // tm
    v12_spec = pl.BlockSpec((None, 2, None, tm // 2, 2 * F_WIDTH),
                            lambda i: (i // per_seq, 0, (i % per_seq) // per_half, i % per_half, 0))
    return pl.pallas_call(
        _inproj_kernel,
        grid=(TOKENS // tm,),
        in_specs=[tile(D_MODEL), full((1, D_MODEL)), full(win.shape), full((1, Q_LORA_RANK)), full(wq.shape),
                  full((1, KV_LORA_RANK)), full(wkv.shape), full(ab.shape), table, table,
                  full(ec.shape), full(es.shape), full(hsum.shape)],
        out_specs=[v12_spec, tile(MLA_HEADS * HEAD_PAD), tile(MLA_HEADS * HEAD_PAD),
                   tile(MLA_HEADS * V_HEAD_DIM), pl.BlockSpec((None, 2, LANES), lambda i: (i, 0, 0))],
        out_shape=[jax.ShapeDtypeStruct((BATCH, 2, 2, QUARTER, 2 * F_WIDTH), BF16),
                   jax.ShapeDtypeStruct((TOKENS, MLA_HEADS * HEAD_PAD), BF16),
                   jax.ShapeDtypeStruct((TOKENS, MLA_HEADS * HEAD_PAD), BF16),
                   jax.ShapeDtypeStruct((TOKENS, MLA_HEADS * V_HEAD_DIM), BF16),
                   jax.ShapeDtypeStruct((TOKENS // tm, 2, LANES), F32)],
        scratch_shapes=[pltpu.VMEM((2 * F_WIDTH // LANES, tm, LANES), F32)],
        compiler_params=_params("parallel"),
        name="in_projection",
    )(x2d, g_mix.reshape(1, -1), win, g_q.reshape(1, -1), wq, g_kv.reshape(1, -1), wkv, ab, cos, sin, ec, es, hsum)


def _seq_dft_kernel(v_ref, m_ref, y_ref):
    sign = jnp.where(pl.program_id(1) == 0, 1.0, -1.0)
    parts = []
    for q in range(2):
        ab = (v_ref[q, 0].astype(F32) + sign * v_ref[q, 1].astype(F32)).astype(BF16)
        parts.append(jnp.dot(m_ref[q, :, :QUARTER], ab[:, :F_WIDTH], preferred_element_type=F32)
                     + jnp.dot(m_ref[q, :, QUARTER:], ab[:, F_WIDTH:], preferred_element_type=F32))
    y_ref[0] = (parts[0] + parts[1]).astype(BF16)
    y_ref[1] = (parts[0] - parts[1]).astype(BF16)


def _seq_dft_mats():
    i = np.arange(QUARTER)
    out = np.zeros((2, 2, QUARTER, 2 * QUARTER), np.float32)
    for p in range(2):
        for q in range(2):
            prod = np.outer(2 * i + p, 2 * i + q) % SEQ
            ang = 2.0 * np.pi * prod / SEQ
            out[p, q, :, :QUARTER] = np.cos(ang) / np.sqrt(SEQ)
            out[p, q, :, QUARTER:] = -np.sin(ang) / np.sqrt(SEQ)
    return out


def _sequence_dft(v):
    mats = jnp.asarray(_seq_dft_mats()).astype(BF16)
    return pl.pallas_call(
        _seq_dft_kernel,
        grid=(BATCH, 2),
        in_specs=[pl.BlockSpec((None, 2, 2, QUARTER, 2 * F_WIDTH), lambda b, p: (b, 0, 0, 0, 0)),
                  pl.BlockSpec((None, 2, QUARTER, 2 * QUARTER), lambda b, p: (p, 0, 0, 0))],
        out_specs=pl.BlockSpec((None, None, 2, QUARTER, F_WIDTH), lambda b, p: (b, p, 0, 0, 0)),
        out_shape=jax.ShapeDtypeStruct((BATCH, 2, 2, QUARTER, F_WIDTH), BF16),
        compiler_params=_params("parallel", "arbitrary"),
        name="sequence_dft",
    )(v, mats)


def _mla_kernel(q_ref, k_ref, v_ref, sq_ref, o_ref):
    v = v_ref[...]
    hg = pl.program_id(1)
    sq = jnp.max(sq_ref[...], axis=0)
    bounds = jnp.sqrt(sq[0:1] * sq[1:2]) * NORM_SLACK
    head_lane = lax.broadcasted_iota(I32, bounds.shape, 1)
    head_bounds = [jnp.max(jnp.where(head_lane == ATTN_HEADS * hg + j, bounds, 0.0), axis=1, keepdims=True)
                   for j in range(ATTN_HEADS)]
    safe = jnp.max(functools.reduce(jnp.maximum, head_bounds)) <= SAFE_SHIFT

    def attend(j, row_shift):
        cols = slice(HEAD_PAD * j, HEAD_PAD * (j + 1))
        out_cols = slice(V_HEAD_DIM * j, V_HEAD_DIM * (j + 1))
        s = lax.dot_general(q_ref[:, cols], k_ref[:, cols], (((1,), (1,)), ((), ())),
                            preferred_element_type=F32)
        p = jnp.exp(s - row_shift(s))
        l = jnp.sum(p, axis=1, keepdims=True)
        o = jnp.dot(p.astype(BF16), v, preferred_element_type=F32) / l
        o_ref[:, out_cols] = o[:, out_cols].astype(BF16)

    @pl.when(safe)
    def _():
        for j in range(ATTN_HEADS):
            attend(j, lambda s: head_bounds[j])

    @pl.when(jnp.logical_not(safe))
    def _():
        for j in range(ATTN_HEADS):
            attend(j, lambda s: jnp.max(s, axis=1, keepdims=True))


def _mla_attention(q, k, v, ksq):
    q = q.reshape(BATCH, SEQ, -1)
    k = k.reshape(BATCH, SEQ, -1)
    v = v.reshape(BATCH, SEQ, -1)
    ksq = ksq.reshape(BATCH, SEQ // TM_IN, 2, LANES)
    out = pl.pallas_call(
        _mla_kernel,
        grid=(BATCH, MLA_HEADS // ATTN_HEADS, SEQ // TQ),
        in_specs=[pl.BlockSpec((None, TQ, ATTN_HEADS * HEAD_PAD), lambda b, hg, i: (b, i, hg)),
                  pl.BlockSpec((None, SEQ, ATTN_HEADS * HEAD_PAD), lambda b, hg, i: (b, 0, hg)),
                  pl.BlockSpec((None, SEQ, ATTN_HEADS * V_HEAD_DIM), lambda b, hg, i: (b, 0, hg)),
                  pl.BlockSpec((None, SEQ // TM_IN, 2, LANES), lambda b, hg, i: (b, 0, 0, 0))],
        out_specs=pl.BlockSpec((None, TQ, ATTN_HEADS * V_HEAD_DIM), lambda b, hg, i: (b, i, hg)),
        out_shape=jax.ShapeDtypeStruct((BATCH, SEQ, MLA_HEADS * V_HEAD_DIM), BF16),
        compiler_params=_params("parallel", "parallel", "arbitrary"),
        name="mla_attention",
    )(q, k, v, ksq)
    return out.reshape(TOKENS, MLA_HEADS * V_HEAD_DIM)


def _memkv_kernel(mem_ref, g_ref, w_ref, k_ref, v_ref):
    mn = _rms(mem_ref[...], g_ref[...]).astype(BF16)
    kv = jnp.dot(mn, w_ref[...], preferred_element_type=F32)
    k_ref[...] = kv[:, :D_MODEL].astype(BF16)
    v_ref[...] = kv[:, D_MODEL:].astype(BF16)


def _memory_kv(mem, g_mem_kv, w_mem_kv):
    blk = pl.BlockSpec((None, MEM_LEN, D_MODEL), lambda b: (b, 0, 0))
    return pl.pallas_call(
        _memkv_kernel,
        grid=(BATCH,),
        in_specs=[blk, pl.BlockSpec((1, D_MODEL), lambda b: (0, 0)),
                  pl.BlockSpec((D_MODEL, 2 * D_MODEL), lambda b: (0, 0))],
        out_specs=[blk, blk],
        out_shape=[jax.ShapeDtypeStruct((BATCH, MEM_LEN, D_MODEL), BF16)] * 2,
        compiler_params=_params("parallel"),
        name="memory_kv",
    )(mem, g_mem_kv.reshape(1, -1), w_mem_kv.astype(BF16))


def _mix_kernel(x_ref, yf_ref, ya_ref, wo_ref, gq_ref, wmq_ref, mk_ref, mv_ref, wmo_ref, gf_ref, wr_ref,
                x2_ref, hext_ref, aff_ref, zscr_ref):
    tm = x_ref.shape[0]
    rows_per_group = tm // MIX_GROUPS
    for grp in range(MIX_GROUPS):
        rows = pl.ds(grp * rows_per_group, rows_per_group)
        half = pl.ds(grp * rows_per_group // 2, rows_per_group // 2)
        wo_f = wo_ref[:F_WIDTH, :]
        z_even = jnp.dot(yf_ref[0, half, :], wo_f, preferred_element_type=F32)
        z_odd = jnp.dot(yf_ref[1, half, :], wo_f, preferred_element_type=F32)
        for c in range(zscr_ref.shape[0]):
            cols = slice(LANES * c, LANES * (c + 1))
            zscr_ref[c, pl.ds(grp * rows_per_group, rows_per_group // 2, stride=2), :] = z_even[:, cols]
            zscr_ref[c, pl.ds(grp * rows_per_group + 1, rows_per_group // 2, stride=2), :] = z_odd[:, cols]
        z = jnp.concatenate([zscr_ref[c, rows, :] for c in range(zscr_ref.shape[0])], axis=1)
        x1 = x_ref[rows, :] + z + jnp.dot(ya_ref[rows, :], wo_ref[F_WIDTH:, :], preferred_element_type=F32)

        hq = _rms(x1, gq_ref[...]).astype(BF16)
        qm = (jnp.dot(hq, wmq_ref[...], preferred_element_type=F32) * (MEM_HEAD_DIM ** -0.5)).astype(BF16)
        heads = []
        for hd in range(MEM_HEADS):
            sl = slice(MEM_HEAD_DIM * hd, MEM_HEAD_DIM * (hd + 1))
            s = lax.dot_general(qm[:, sl], mk_ref[:, sl], (((1,), (1,)), ((), ())),
                                preferred_element_type=F32)
            p = jnp.exp(s - jnp.max(s, axis=1, keepdims=True))
            l = jnp.sum(p, axis=1, keepdims=True)
            heads.append((jnp.dot(p.astype(BF16), mv_ref[:, sl], preferred_element_type=F32) / l).astype(BF16))
        o = jnp.concatenate(heads, axis=1)
        x2 = x1 + jnp.dot(o, wmo_ref[...], preferred_element_type=F32)
        x2_ref[rows, :] = x2

        h3 = _rms(x2, gf_ref[...])
        h3_hi = h3.astype(BF16)
        hext_ref[rows, :D_MODEL] = h3_hi
        h3_lo = (h3 - h3_hi.astype(F32)).astype(BF16)
        hi_terms = jnp.dot(h3_hi, wr_ref[...], preferred_element_type=F32)
        logits = (hi_terms[:, :LANES] + hi_terms[:, LANES:]
                  + jnp.dot(h3_lo, wr_ref[:, :LANES], preferred_element_type=F32))
        lane = lax.broadcasted_iota(I32, logits.shape, 1)
        logits = jnp.where(lane < N_EXPERTS, logits, -jnp.inf)
        e = jnp.exp(logits - jnp.max(logits, axis=1, keepdims=True))
        aff = e / jnp.sum(e, axis=1, keepdims=True)
        aff_ref[rows, :] = aff
        hi = aff.astype(BF16)
        r1 = aff - hi.astype(F32)
        mid = r1.astype(BF16)
        lo = (r1 - mid.astype(F32)).astype(BF16)
        hext_ref[rows, D_MODEL:] = jnp.where(
            lane < N_EXPERTS, hi,
            jnp.where(lane < 2 * N_EXPERTS, pltpu.roll(mid.astype(F32), N_EXPERTS, 1).astype(BF16),
                      pltpu.roll(lo.astype(F32), 2 * N_EXPERTS, 1).astype(BF16)))


def _mixing(x2d, y_f, y_a, w_out, g_mem_q, w_mem_q, mk, mv, w_mem_o, g_ffn, w_router):
    tm = TM_MIX
    wr = jnp.concatenate([w_router, jnp.zeros((D_MODEL, LANES - N_EXPERTS), F32)], axis=1)
    wr_hi = wr.astype(BF16)
    wr_cat = jnp.concatenate([wr_hi, (wr - wr_hi.astype(F32)).astype(BF16)], axis=1)
    full = lambda shape: pl.BlockSpec(shape, lambda i: (0,) * len(shape), pipeline_mode=pl.Buffered(1))
    tile = lambda w: pl.BlockSpec((tm, w), lambda i: (i, 0))
    per_half = HALF // tm
    per_seq = SEQ // tm
    per_batch = pl.BlockSpec((None, MEM_LEN, D_MODEL), lambda i: (i // per_seq, 0, 0))
    yf_spec = pl.BlockSpec((None, 2, None, tm // 2, F_WIDTH),
                           lambda i: (i // per_seq, 0, (i % per_seq) // per_half, i % per_half, 0))
    return pl.pallas_call(
        _mix_kernel,
        grid=(TOKENS // tm,),
        in_specs=[tile(D_MODEL), yf_spec, tile(F_WIDTH), full((D_MODEL, D_MODEL)), full((1, D_MODEL)),
                  full((D_MODEL, D_MODEL)), per_batch, per_batch, full((D_MODEL, D_MODEL)), full((1, D_MODEL)),
                  full((D_MODEL, 2 * LANES))],
        out_specs=[tile(D_MODEL), tile(D_MODEL + LANES), tile(LANES)],
        out_shape=[jax.ShapeDtypeStruct((TOKENS, D_MODEL), F32),
                   jax.ShapeDtypeStruct((TOKENS, D_MODEL + LANES), BF16),
                   jax.ShapeDtypeStruct((TOKENS, LANES), F32)],
        scratch_shapes=[pltpu.VMEM((D_MODEL // LANES, tm, LANES), F32)],
        compiler_params=_params("parallel"),
        name="mix_memattn_router",
    )(x2d, y_f, y_a, w_out.astype(BF16), g_mem_q.reshape(1, -1), w_mem_q.astype(BF16), mk, mv,
      w_mem_o.astype(BF16), g_ffn.reshape(1, -1), wr_cat)


def _topk_kernel(aff_ref, slot_ref, offs_ref):
    aff = aff_ref[...]
    rows = aff.shape[0]

    thr = jnp.zeros((rows, 1), I32)
    for bit in range(30, -1, -1):
        cand = thr | (1 << bit)
        cnt = jnp.sum(jnp.where(aff >= pltpu.bitcast(cand, F32), 1.0, 0.0), axis=1, keepdims=True)
        thr = jnp.where(cnt >= CAPACITY, cand, thr)
    thr_f = pltpu.bitcast(thr, F32)

    chunk = 256
    r = lax.broadcasted_iota(I32, (chunk, chunk), 0)
    c = lax.broadcasted_iota(I32, (chunk, chunk), 1)
    tri = jnp.where(r < c, 1.0, 0.0).astype(BF16)

    def exclusive_count(mask):
        off = jnp.zeros((rows, 1), F32)
        outs = []
        for j in range(SEQ // chunk):
            mj = mask[:, chunk * j:chunk * (j + 1)]
            outs.append(jnp.dot(mj.astype(BF16), tri, preferred_element_type=F32) + off)
            off = off + jnp.sum(mj, axis=1, keepdims=True)
        return jnp.concatenate(outs, axis=1), off

    gt = aff > thr_f
    tie = jnp.where(aff == thr_f, 1.0, 0.0)
    n_gt = jnp.sum(jnp.where(gt, 1.0, 0.0), axis=1, keepdims=True)
    tie_rank, _ = exclusive_count(tie)
    sel = jnp.where(gt | ((tie > 0.0) & (tie_rank < CAPACITY - n_gt)), 1.0, 0.0)
    slot, _ = exclusive_count(sel)
    slot_ref[...] = jnp.where(sel > 0.0, slot.astype(I32), -1)
    tok = lax.broadcasted_iota(I32, (SEQ, LANES), 0)
    j = lax.broadcasted_iota(I32, (SEQ, LANES), 1)
    before = jnp.where(tok < j * TOKEN_CHUNK, 1.0, 0.0).astype(BF16)
    offs_ref[...] = jnp.dot(sel.astype(BF16), before, preferred_element_type=F32).astype(I32)


def _expert_slots(aff):
    aff_t = aff[:, :N_EXPERTS].reshape(BATCH, SEQ, N_EXPERTS).transpose(0, 2, 1).reshape(BATCH * N_EXPERTS, SEQ)
    slots, offs = pl.pallas_call(
        _topk_kernel,
        out_shape=[jax.ShapeDtypeStruct((BATCH * N_EXPERTS, SEQ), I32),
                   jax.ShapeDtypeStruct((BATCH * N_EXPERTS, LANES), I32)],
        compiler_params=pltpu.CompilerParams(vmem_limit_bytes=VMEM_LIMIT),
        name="expert_topk",
    )(aff_t)
    return slots, offs[:, :OFFS_STRIDE].reshape(-1)


def _window_start(first, win):
    start = jnp.minimum((first >> SLOT_ALIGN_SHIFT) << SLOT_ALIGN_SHIFT, CAPACITY - win)
    return pl.multiple_of(start, 1 << SLOT_ALIGN_SHIFT)


def _gather_kernel(offs_ref, slot_ref, h_ref, x_ref):
    win = GATHER_WIN
    last = CAPACITY - win
    b = pl.program_id(0)
    e0 = pl.program_id(1) * GATHER_EXPERTS
    j = pl.program_id(2)

    @pl.when(j == 0)
    def _():
        x_ref[...] = jnp.zeros_like(x_ref)

    def bounds(e):
        base = (b * N_EXPERTS + e0 + e) * OFFS_STRIDE + j
        return offs_ref[base], offs_ref[base + 1]

    row = lax.broadcasted_iota(I32, (win, TOKEN_CHUNK), 0)
    h_c = h_ref[...]
    starts = [_window_start(bounds(e)[0], win) for e in range(GATHER_EXPERTS)]
    onehot = jnp.concatenate(
        [jnp.where(row + starts[e] == slot_ref[e:e + 1, :], 1.0, 0.0).astype(BF16)
         for e in range(GATHER_EXPERTS)], axis=0)
    picked = jnp.dot(onehot, h_c, preferred_element_type=F32).astype(BF16)
    for e in range(GATHER_EXPERTS):
        rows = pl.ds(starts[e], win)
        x_ref[e, rows, :] = x_ref[e, rows, :] + picked[e * win:(e + 1) * win]

    for e in range(GATHER_EXPERTS):
        _, end = bounds(e)
        covered = starts[e] + win
        slot_e = slot_ref[e:e + 1, :]

        def extra_window(i, carry):
            lo = covered + i * win
            r0 = pl.multiple_of(jnp.minimum(lo, last), 1 << SLOT_ALIGN_SHIFT)
            hot = jnp.where((row + r0 == slot_e) & (slot_e >= lo), 1.0, 0.0).astype(BF16)
            rows = pl.ds(r0, win)
            x_ref[e, rows, :] = x_ref[e, rows, :] + jnp.dot(hot, h_c, preferred_element_type=F32).astype(BF16)
            return carry

        lax.fori_loop(0, jnp.maximum(end - covered + win - 1, 0) // win, extra_window, 0)


def _gather(offs, slots, h_ext):
    slots = slots.reshape(BATCH, N_EXPERTS, SEQ)
    h_ext = h_ext.reshape(BATCH, SEQ, D_MODEL + LANES)
    return pl.pallas_call(
        _gather_kernel,
        grid_spec=pltpu.PrefetchScalarGridSpec(
            num_scalar_prefetch=1,
            grid=(BATCH, N_EXPERTS // GATHER_EXPERTS, SEQ // TOKEN_CHUNK),
            in_specs=[pl.BlockSpec((None, GATHER_EXPERTS, TOKEN_CHUNK), lambda b, g, j, offs: (b, g, j)),
                      pl.BlockSpec((None, TOKEN_CHUNK, D_MODEL + LANES), lambda b, g, j, offs: (b, j, 0))],
            out_specs=pl.BlockSpec((None, GATHER_EXPERTS, None, CAPACITY, D_MODEL + LANES),
                                   lambda b, g, j, offs: (b // FFN_SEQS, g, b % FFN_SEQS, 0, 0))),
        out_shape=jax.ShapeDtypeStruct((BATCH // FFN_SEQS, N_EXPERTS, FFN_SEQS, CAPACITY, D_MODEL + LANES),
                                       BF16),
        compiler_params=_params("parallel", "parallel", "arbitrary"),
        name="expert_gather",
    )(offs, slots, h_ext)


def _expert_kernel(x_ref, wg_ref, wu_ref, wd_ref, y_ref, wg_s, wu_s, wd_s):
    g = pl.program_id(0)
    part = pl.program_id(1)
    slab = wg_ref.shape[0]

    def stage():
        rows = pl.ds(pl.multiple_of(part * slab, slab), slab)
        wg_s[g % 2, rows, :] = wg_ref[...].astype(BF16)
        wu_s[g % 2, rows, :] = wu_ref[...].astype(BF16)
        wd_s[g % 2, rows, :] = wd_ref[...].astype(BF16)

    def run():
        e = g - 1
        cur = e % 2
        rows = x_ref.shape[0] * x_ref.shape[1]
        xin = x_ref[:, :, :D_MODEL].reshape(rows, D_MODEL)
        ext = x_ref[:, :, D_MODEL:].reshape(rows, LANES).astype(F32)
        lane = lax.broadcasted_iota(I32, ext.shape, 1)
        mine = (lane == e) | (lane == e + N_EXPERTS) | (lane == e + 2 * N_EXPERTS)
        gate = jnp.sum(jnp.where(mine, ext, 0.0), axis=1, keepdims=True)

        a = jnp.dot(xin, wg_s[cur], preferred_element_type=F32)
        b = jnp.dot(xin, wu_s[cur], preferred_element_type=F32)
        hid = (a / (1.0 + jnp.exp(-a)) * b).astype(BF16)
        y = (jnp.dot(hid, wd_s[cur], preferred_element_type=F32) * gate).astype(BF16)
        y_ref[...] = y.reshape(y_ref.shape)

    @pl.when(g == 0)
    def _():
        stage()
        y_ref[...] = jnp.zeros_like(y_ref)

    @pl.when((g > 0) & (g < N_EXPERTS))
    def _():
        stage()
        run()

    @pl.when(g == N_EXPERTS)
    def _():
        run()


def _experts(xin, w_gate, w_up, w_down):
    parts = BATCH // FFN_SEQS
    slab = D_MODEL // parts
    prev = lambda g: jnp.maximum(g - 1, 0)
    wspec = pl.BlockSpec((None, slab, D_MODEL), lambda g, p: (jnp.minimum(g, N_EXPERTS - 1), p, 0))
    return pl.pallas_call(
        _expert_kernel,
        grid=(N_EXPERTS + 1, parts),
        in_specs=[pl.BlockSpec((None, None, FFN_SEQS, CAPACITY, D_MODEL + LANES),
                               lambda g, p: (p, prev(g), 0, 0, 0)),
                  wspec, wspec, wspec],
        out_specs=pl.BlockSpec((None, None, FFN_SEQS, CAPACITY, D_MODEL),
                               lambda g, p: (p, jnp.where(g == 0, N_EXPERTS, g - 1), 0, 0, 0)),
        out_shape=jax.ShapeDtypeStruct((parts, N_EXPERTS + 1, FFN_SEQS, CAPACITY, D_MODEL), BF16),
        scratch_shapes=[pltpu.VMEM((2, D_MODEL, D_MODEL), BF16)] * 3,
        compiler_params=_params("arbitrary", "arbitrary"),
        name="expert_ffn",
    )(xin, w_gate, w_up, w_down)


def _combine_kernel(offs_ref, x2_ref, slot_ref, y_ref, g_ref, o_ref):
    b = pl.program_id(0)
    j = pl.program_id(1)
    win = SCATTER_WIN
    lane = lax.broadcasted_iota(I32, (TOKEN_CHUNK, win), 1)
    slot = slot_ref[...]

    def bounds(e):
        base = (b * N_EXPERTS + e) * OFFS_STRIDE + j
        return offs_ref[base], offs_ref[base + 1]

    starts = [_window_start(bounds(e)[0], win) for e in range(N_EXPERTS)]
    onehot = jnp.concatenate(
        [jnp.where(lane + starts[e] == slot[:, e:e + 1], 1.0, 0.0).astype(BF16) for e in range(N_EXPERTS)],
        axis=1)
    rows = jnp.concatenate([y_ref[e, pl.ds(starts[e], win), :] for e in range(N_EXPERTS)], axis=0)
    x3 = x2_ref[...] + jnp.dot(onehot, rows, preferred_element_type=F32)

    leftover = [bounds(e)[1] > starts[e] + win for e in range(N_EXPERTS)]
    any_leftover = functools.reduce(jnp.logical_or, leftover)

    @pl.when(jnp.logical_not(any_leftover))
    def _():
        o_ref[...] = _rms(x3, g_ref[...])

    @pl.when(any_leftover)
    def _():
        o_ref[...] = x3
        tail_lane = lax.broadcasted_iota(I32, (TOKEN_CHUNK, CAPACITY - win), 1) + win
        for e in range(N_EXPERTS):
            @pl.when(leftover[e])
            def _():
                sl = slot[:, e:e + 1]
                hot = jnp.where((tail_lane == sl) & (sl >= starts[e] + win), 1.0, 0.0).astype(BF16)
                o_ref[...] += jnp.dot(hot, y_ref[e, win:, :], preferred_element_type=F32)

        o_ref[...] = _rms(o_ref[...], g_ref[...])


def _combine(offs, x2, slots, y, g_final):
    slots_t = slots.reshape(BATCH, N_EXPERTS, SEQ).transpose(0, 2, 1)
    x2 = x2.reshape(BATCH, SEQ, D_MODEL)
    tm = TOKEN_CHUNK
    return pl.pallas_call(
        _combine_kernel,
        grid_spec=pltpu.PrefetchScalarGridSpec(
            num_scalar_prefetch=1,
            grid=(BATCH, SEQ // tm),
            in_specs=[pl.BlockSpec((None, tm, D_MODEL), lambda b, i, offs: (b, i, 0)),
                      pl.BlockSpec((None, tm, N_EXPERTS), lambda b, i, offs: (b, i, 0)),
                      pl.BlockSpec((None, N_EXPERTS, None, CAPACITY, D_MODEL),
                                   lambda b, i, offs: (b // FFN_SEQS, 0, b % FFN_SEQS, 0, 0)),
                      pl.BlockSpec((1, D_MODEL), lambda b, i, offs: (0, 0))],
            out_specs=pl.BlockSpec((None, tm, D_MODEL), lambda b, i, offs: (b, i, 0))),
        out_shape=jax.ShapeDtypeStruct((BATCH, SEQ, D_MODEL), F32),
        compiler_params=_params("parallel", "arbitrary"),
        name="combine_final_norm",
    )(offs, x2, slots_t, y, g_final.reshape(1, -1))


def kernel(x, mem, positions, g_mix, w_in, g_q_lat, w_q_up, g_kv_lat, w_kv_up, w_fourier, w_out, g_mem_q,
           g_mem_kv, w_mem_q, w_mem_kv, w_mem_o, g_ffn, w_router, w_exp_gate, w_exp_up, w_exp_down, g_final):
    assert x.shape == (BATCH, SEQ, D_MODEL) and g_mix.shape[0] == 1
    x2d = x.reshape(TOKENS, D_MODEL)
    cos, sin = _rope_tables(positions)
    ab = _channel_mats(w_fourier[0])
    v12, q, k, v, ksq = _in_projection(x2d, g_mix[0], w_in[0], g_q_lat[0], w_q_up[0], g_kv_lat[0], w_kv_up[0],
                                       ab, cos, sin)
    y_f = _sequence_dft(v12)
    y_a = _mla_attention(q, k, v, ksq)
    mk, mv = _memory_kv(mem, g_mem_kv[0], w_mem_kv[0])
    x2, h_ext, aff = _mixing(x2d, y_f, y_a, w_out[0], g_mem_q[0], w_mem_q[0], mk, mv, w_mem_o[0], g_ffn[0],
                             w_router[0])
    slots, offs = _expert_slots(aff)
    xin = _gather(offs, slots, h_ext)
    y = _experts(xin, w_exp_gate[0], w_exp_up[0], w_exp_down[0])
    return _combine(offs, x2, slots, y, g_final)
```
